```python
import math
import jax
import jax.numpy as jnp
from jax import lax
import numpy as np

D_MODEL = 1024
BATCH = 8
SEQ = 8192
DEPTH = 4

HEAD_DIM = 64
MIX_WIDTH = D_MODEL
WIDTH_A = MIX_WIDTH // 4
N_HEADS_A = WIDTH_A // HEAD_DIM
DILATED_PAIRS = ((128, 1), (512, 4), (2048, 16))
WIDTH_B = MIX_WIDTH // 4
N_HEADS_B = WIDTH_B // HEAD_DIM
N_KV_B = 2
GROUP_B = N_HEADS_B // N_KV_B
SWA_HALF = 128
WIDTH_C = MIX_WIDTH // 2
DIFF_DIM = HEAD_DIM
N_HEADS_C = WIDTH_C // (2 * DIFF_DIM)
Q_BLOCK = 128
IN_WIDTH = 3 * WIDTH_A + WIDTH_B + 2 * N_KV_B * HEAD_DIM + 3 * WIDTH_C
N_EXPERTS = 32
TOP_K = 4
D_EXPERT = D_MODEL
SWIGLU_LIMIT = 7.0
SWIGLU_ALPHA = 1.702
MOE_BLOCK = 256
N_MOD = 6
RMS_EPS = 1e-6

kernel_name = "hybrid_dilated_swa_diff_moe_encoder"


def rmsnorm(x, g):
    xf = x.astype(jnp.float32)
    y = xf * lax.rsqrt(jnp.mean(xf * xf, axis=-1, keepdims=True) + RMS_EPS)
    return (y * g.astype(jnp.float32)).astype(x.dtype)


def alibi_slopes(n):
    return jnp.exp2(-8.0 * jnp.arange(1, n + 1, dtype=jnp.float32) / n)


def split_points():
    sizes = [WIDTH_A, WIDTH_A, WIDTH_A, WIDTH_B, N_KV_B * HEAD_DIM, N_KV_B * HEAD_DIM,
             WIDTH_C, WIDTH_C, WIDTH_C]
    return [int(v) for v in np.cumsum(sizes)[:-1]]


def banded_attention(q, k, v, half_w, slopes, step, sink=None):
    n, L, hk, g, dk = q.shape
    dv = v.shape[-1]
    blk = half_w
    nb = -(-L // blk)
    lp = nb * blk
    qf = jnp.pad(q.astype(jnp.float32), ((0, 0), (0, lp - L), (0, 0), (0, 0), (0, 0)))
    qf = qf.reshape(n, nb, blk, hk, g, dk)
    pad_kv = ((0, 0), (blk, lp - L + blk), (0, 0), (0, 0))
    kp = jnp.pad(k.astype(jnp.float32), pad_kv)
    vp = jnp.pad(v.astype(jnp.float32), pad_kv)

    def neighbourhood(t):
        return jnp.concatenate(
            [t[:, j * blk: j * blk + lp].reshape(n, nb, blk, hk, t.shape[-1]) for j in range(3)], axis=2)

    kn = neighbourhood(kp)
    vn = neighbourhood(vp)
    s = jnp.einsum("nbqhgd,nbkhd->nbhgqk", qf, kn) * (dk ** -0.5)
    rel = jnp.arange(3 * blk)[None, :] - blk - jnp.arange(blk)[:, None]
    key_pos = (jnp.arange(nb) * blk - blk)[:, None, None] + jnp.arange(3 * blk)[None, None, :]
    valid = (jnp.abs(rel) <= half_w)[None] & (key_pos >= 0) & (key_pos < L)
    dist = (step * jnp.abs(rel)).astype(jnp.float32)
    s = s - slopes.astype(jnp.float32)[:, :, None, None] * dist
    s = jnp.where(valid[None, :, None, None], s, -jnp.inf)
    m = jnp.max(s, axis=-1)
    if sink is not None:
        m = jnp.maximum(m, sink[:, :, None])
    p = jnp.exp(s - m[..., None])
    l = jnp.sum(p, axis=-1)
    if sink is not None:
        l = l + jnp.exp(sink[:, :, None] - m)
    o = jnp.einsum("nbhgqk,nbkhe->nbqhge", p, vn) / jnp.moveaxis(l, -1, 2)[..., None]
    o = o.reshape(n, lp, hk, g, dv)[:, :L]
    m = jnp.moveaxis(m, -1, 2).reshape(n, lp, hk, g)[:, :L]
    l = jnp.moveaxis(l, -1, 2).reshape(n, lp, hk, g)[:, :L]
    return o, m, l


def dilated_attention(q, k, v, slopes):
    b, s, h, dh = q.shape
    outs, maxes, denoms = [], [], []
    for window, dil in DILATED_PAIRS:
        L = s // dil

        def to_res(t):
            return t.reshape(b, L, dil, h, t.shape[-1]).transpose(0, 2, 1, 3, 4).reshape(b * dil, L, h, t.shape[-1])

        def from_res(t):
            t = t.reshape((b, dil, L) + t.shape[2:])
            return jnp.swapaxes(t, 1, 2).reshape((b, s) + t.shape[3:])

        o, m, l = banded_attention(to_res(q)[:, :, :, None, :], to_res(k), to_res(v),
                                   window // (2 * dil), slopes[:, None], dil)
        outs.append(from_res(o[:, :, :, 0]))
        maxes.append(from_res(m[..., 0]))
        denoms.append(from_res(l[..., 0]))
    m_all = jnp.stack(maxes)
    l_all = jnp.stack(denoms)
    o_all = jnp.stack(outs)
    w = l_all * jnp.exp(m_all - jnp.max(m_all, axis=0))
    return jnp.sum(w[..., None] * o_all, axis=0) / jnp.sum(w, axis=0)[..., None]


def diff_attention(q, k, v, lam, slopes):
    b, s, h, _, dk = q.shape
    dv = v.shape[-1]
    nq = s // Q_BLOCK
    qf = q.astype(jnp.float32).reshape(b, nq, Q_BLOCK, h, 2, dk).transpose(1, 0, 2, 3, 4, 5)
    kf = k.astype(jnp.float32)
    vf = v.astype(jnp.float32)
    key_pos = jnp.arange(s)

    def block(args):
        qb, start = args
        sc = jnp.einsum("bqhcd,bkhcd->bhcqk", qb, kf) * (dk ** -0.5)
        dist = jnp.abs(start + jnp.arange(Q_BLOCK)[:, None] - key_pos[None, :]).astype(jnp.float32)
        sc = sc - slopes[:, None, None, None] * dist
        p = jax.nn.softmax(sc, axis=-1)
        a = p[:, :, 0] - lam * p[:, :, 1]
        return jnp.einsum("bhqk,bkhe->bqhe", a, vf)

    o = lax.map(block, (qf, jnp.arange(nq) * Q_BLOCK))
    return o.transpose(1, 0, 2, 3, 4).reshape(b, s, h, dv)


def moe_ffn(h, w_router, b_router, w_gate, b_gate, w_up, b_up, w_down, b_down):
    bsz, s, d = h.shape
    t = bsz * s
    tk = t * TOP_K
    ht = h.reshape(t, d)
    logits = (ht @ w_router + b_router).astype(jnp.float32)
    top_val, top_idx = lax.top_k(logits, TOP_K)
    gates = jax.nn.softmax(top_val, axis=-1)
    flat_e = top_idx.reshape(-1).astype(jnp.int32)
    flat_tok = jnp.repeat(jnp.arange(t, dtype=jnp.int32), TOP_K)
    flat_g = gates.reshape(-1)
    order = jnp.argsort(flat_e)
    se, st, sg = flat_e[order], flat_tok[order], flat_g[order]
    counts = jnp.zeros((N_EXPERTS,), jnp.int32).at[flat_e].add(1)
    padded = (counts + MOE_BLOCK - 1) // MOE_BLOCK * MOE_BLOCK
    pad_end = jnp.cumsum(padded)
    pad_start = pad_end - padded
    start = jnp.cumsum(counts) - counts
    dest = pad_start[se] + (jnp.arange(tk, dtype=jnp.int32) - start[se])
    nb = -(-tk // MOE_BLOCK) + N_EXPERTS
    row_tok = jnp.full((nb * MOE_BLOCK,), t, jnp.int32).at[dest].set(st)
    row_g = jnp.zeros((nb * MOE_BLOCK,), jnp.float32).at[dest].set(sg)
    blk_e = jnp.minimum(jnp.searchsorted(pad_end, jnp.arange(nb) * MOE_BLOCK, side="right"),
                        N_EXPERTS - 1)
    ht_pad = jnp.concatenate([ht, jnp.zeros((1, d), ht.dtype)], axis=0)

    def step(y, inp):
        rows, g, e = inp
        xb = ht_pad[rows]
        a = xb @ w_gate[e] + b_gate[e]
        u = xb @ w_up[e] + b_up[e]
        a = jnp.minimum(a, SWIGLU_LIMIT)
        u = jnp.clip(u, -SWIGLU_LIMIT, SWIGLU_LIMIT)
        hid = a * jax.nn.sigmoid(SWIGLU_ALPHA * a) * (u + 1.0)
        out = (hid @ w_down[e] + b_down[e]) * g[:, None].astype(hid.dtype)
        return y.at[rows].add(out.astype(y.dtype)), None

    y0 = jnp.zeros((t + 1, d), h.dtype)
    y, _ = lax.scan(step, y0, (row_tok.reshape(nb, MOE_BLOCK), row_g.reshape(nb, MOE_BLOCK), blk_e))
    return y[:t].reshape(bsz, s, d)


def setup_inputs(seed: int = 0) -> dict:
    key = jax.random.key(seed)
    ks = jax.random.split(key, 24)
    nrm = jax.random.normal
    f32 = jnp.float32
    d, e, f = D_MODEL, N_EXPERTS, D_EXPERT
    return {
        "x": nrm(ks[0], (BATCH, SEQ, d), f32),
        "c": nrm(ks[1], (BATCH, d), f32),
        "w_ada": nrm(ks[2], (DEPTH, d, N_MOD * d), f32) * (0.5 * d ** -0.5),
        "b_ada": nrm(ks[3], (DEPTH, N_MOD * d), f32) * 0.01,
        "norm1": 1.0 + 0.01 * nrm(ks[4], (DEPTH, d), f32),
        "w_in": nrm(ks[5], (DEPTH, d, IN_WIDTH), f32) * d ** -0.5,
        "sinks": nrm(ks[6], (DEPTH, N_HEADS_B), f32),
        "lambda_q1": nrm(ks[7], (DEPTH, DIFF_DIM), f32) * 0.1,
        "lambda_k1": nrm(ks[8], (DEPTH, DIFF_DIM), f32) * 0.1,
        "lambda_q2": nrm(ks[9], (DEPTH, DIFF_DIM), f32) * 0.1,
        "lambda_k2": nrm(ks[10], (DEPTH, DIFF_DIM), f32) * 0.1,
        "subln": 1.0 + 0.01 * nrm(ks[11], (DEPTH, 2 * DIFF_DIM), f32),
        "w_out": nrm(ks[12], (DEPTH, MIX_WIDTH, d), f32) * MIX_WIDTH ** -0.5,
        "norm2": 1.0 + 0.01 * nrm(ks[13], (DEPTH, d), f32),
        "w_router": nrm(ks[14], (DEPTH, d, e), f32) * d ** -0.5,
        "b_router": nrm(ks[15], (DEPTH, e), f32) * 0.01,
        "w_gate": nrm(ks[16], (DEPTH, e, d, f), f32) * d ** -0.5,
        "b_gate": nrm(ks[17], (DEPTH, e, f), f32) * 0.01,
        "w_up": nrm(ks[18], (DEPTH, e, d, f), f32) * d ** -0.5,
        "b_up": nrm(ks[19], (DEPTH, e, f), f32) * 0.01,
        "w_down": nrm(ks[20], (DEPTH, e, f, d), f32) * f ** -0.5,
        "b_down": nrm(ks[21], (DEPTH, e, d), f32) * 0.01,
        "norm_f": 1.0 + 0.01 * nrm(ks[22], (d,), f32),
    }


def reference(x, c, w_ada, b_ada, norm1, w_in, sinks, lambda_q1, lambda_k1, lambda_q2, lambda_k2,
              subln, w_out, norm2, w_router, b_router, w_gate, b_gate, w_up, b_up, w_down, b_down,
              norm_f):
    b, s, _ = x.shape
    slopes_a = alibi_slopes(N_HEADS_A)
    slopes_b = alibi_slopes(N_HEADS_B).reshape(N_KV_B, GROUP_B)
    slopes_c = alibi_slopes(N_HEADS_C)
    cond = jax.nn.silu(c)
    cuts = split_points()
    for layer in range(DEPTH):
        mod = (cond @ w_ada[layer] + b_ada[layer])[:, None, :]
        sh1, sc1, g1, sh2, sc2, g2 = jnp.split(mod, N_MOD, axis=-1)

        h = rmsnorm(x, norm1[layer]) * (1.0 + sc1) + sh1
        proj = h @ w_in[layer]
        qa, ka, va, qb, kb, vb, qc, kc, vc = jnp.split(proj, cuts, axis=-1)

        o_a = dilated_attention(qa.reshape(b, s, N_HEADS_A, HEAD_DIM), ka.reshape(b, s, N_HEADS_A, HEAD_DIM),
                                va.reshape(b, s, N_HEADS_A, HEAD_DIM), slopes_a)

        o_b, _, _ = banded_attention(qb.reshape(b, s, N_KV_B, GROUP_B, HEAD_DIM),
                                     kb.reshape(b, s, N_KV_B, HEAD_DIM), vb.reshape(b, s, N_KV_B, HEAD_DIM),
                                     SWA_HALF, slopes_b, 1,
                                     sinks[layer].astype(jnp.float32).reshape(N_KV_B, GROUP_B))

        lam_init = 0.8 - 0.6 * math.exp(-0.3 * layer)
        lam = (jnp.exp(jnp.sum(lambda_q1[layer].astype(jnp.float32) * lambda_k1[layer].astype(jnp.float32)))
               - jnp.exp(jnp.sum(lambda_q2[layer].astype(jnp.float32) * lambda_k2[layer].astype(jnp.float32)))
               + lam_init)
        o_c = diff_attention(qc.reshape(b, s, N_HEADS_C, 2, DIFF_DIM), kc.reshape(b, s, N_HEADS_C, 2, DIFF_DIM),
                             vc.reshape(b, s, N_HEADS_C, 2 * DIFF_DIM), lam, slopes_c)
        o_c = rmsnorm(o_c, subln[layer]) * (1.0 - lam_init)

        mixed = jnp.concatenate([o_a.reshape(b, s, WIDTH_A), o_b.reshape(b, s, WIDTH_B),
                                 o_c.reshape(b, s, WIDTH_C)], axis=-1).astype(x.dtype)
        x = x + g1 * (mixed @ w_out[layer])

        h = rmsnorm(x, norm2[layer]) * (1.0 + sc2) + sh2
        x = x + g2 * moe_ffn(h, w_router[layer], b_router[layer], w_gate[layer], b_gate[layer],
                             w_up[layer], b_up[layer], w_down[layer], b_down[layer])
    return rmsnorm(x, norm_f)
```

```python
import functools
import math

import jax
import jax.numpy as jnp
from jax import lax
from jax.experimental import pallas as pl
from jax.experimental.pallas import tpu as pltpu

F32 = jnp.float32
BF16 = jnp.bfloat16
I32 = jnp.int32
U32 = jnp.uint32

HEAD_DIM = 64
DILATED_PAIRS = ((128, 1), (512, 4), (2048, 16))
N_HEADS_A = 4
N_HEADS_B = 4
N_KV_B = 2
SWA_HALF = 128
N_HEADS_C = 4
N_EXPERTS = 32
TOP_K = 4
SWIGLU_LIMIT = 7.0
SWIGLU_ALPHA = 1.702
RMS_EPS = 1e-6
N_MOD = 6

LANES = 128
VMEM_LIMIT_BYTES = 56 * 1024 * 1024

TM_PROJ = 512
TQ_BAND = 256
T_FLASH = 1024
TM_ROUTE = 512
TM_DISPATCH = 512
BLK_FFN = 512
TM_COMBINE = 256

HIGHEST = lax.Precision.HIGHEST
_NT = (((1,), (1,)), ((), ()))


def _cparams(sem):
    return pltpu.CompilerParams(dimension_semantics=sem, vmem_limit_bytes=VMEM_LIMIT_BYTES)


def _pack_halves(a, b):
    ua = lax.bitcast_convert_type(a.astype(BF16).astype(F32), U32)
    ub = lax.bitcast_convert_type(b.astype(BF16).astype(F32), U32)
    return ua | (ub >> 16)


def _unpack_halves(w):
    a = lax.bitcast_convert_type(w & jnp.uint32(0xFFFF0000), F32)
    b = lax.bitcast_convert_type(w << 16, F32)
    return a, b


def _rms_scale(x):
    return lax.rsqrt(jnp.mean(x * x, axis=-1, keepdims=True) + RMS_EPS)


def _ada_body(c_ref, w_ref, b_ref, o_ref):
    c = c_ref[...]
    cond = c / (1.0 + jnp.exp(-c))
    o_ref[0] = jnp.dot(cond, w_ref[0], preferred_element_type=F32, precision=HIGHEST) + b_ref[0]


def ada_modulation(c, w_ada, b_ada):
    depth, d, n = w_ada.shape
    bsz = c.shape[0]
    tn = 1536
    assert n % tn == 0
    return pl.pallas_call(
        _ada_body,
        grid=(depth, n // tn),
        in_specs=[
            pl.BlockSpec((bsz, d), lambda l, j: (0, 0)),
            pl.BlockSpec((1, d, tn), lambda l, j: (l, 0, j)),
            pl.BlockSpec((1, 1, tn), lambda l, j: (l, 0, j)),
        ],
        out_specs=pl.BlockSpec((1, bsz, tn), lambda l, j: (l, 0, j)),
        out_shape=jax.ShapeDtypeStruct((depth, bsz, n), F32),
        compiler_params=_cparams(("parallel", "parallel")),
        name="ada_modulation",
    )(c, w_ada, b_ada.reshape(depth, 1, n))


def _inproj_body(x_ref, sc_ref, sh_ref, g_ref, w_ref, o_ref):
    x = x_ref[...]
    h = x * _rms_scale(x) * g_ref[...]
    h = h * (1.0 + sc_ref[0]) + sh_ref[0]
    o_ref[...] = jnp.dot(h.astype(BF16), w_ref[...], preferred_element_type=F32).astype(o_ref.dtype)


def in_projection(x2d, scale, shift, gain, w_bf16, seq):
    t, d = x2d.shape
    n = w_bf16.shape[1]
    tm = min(TM_PROJ, seq)
    assert seq % tm == 0 and t % tm == 0
    mod_spec = pl.BlockSpec((1, 1, d), lambda i: ((i * tm) // seq, 0, 0))
    return pl.pallas_call(
        _inproj_body,
        grid=(t // tm,),
        in_specs=[
            pl.BlockSpec((tm, d), lambda i: (i, 0)),
            mod_spec, mod_spec,
            pl.BlockSpec((1, d), lambda i: (0, 0)),
            pl.BlockSpec((d, n), lambda i: (0, 0)),
        ],
        out_specs=pl.BlockSpec((tm, n), lambda i: (i, 0)),
        out_shape=jax.ShapeDtypeStruct((t, n), BF16),
        compiler_params=_cparams(("parallel",)),
        name="in_projection",
    )(x2d, scale, shift, gain, w_bf16)


def _banded_body(slope_ref, sink_ref, q_ref, k_ref, v_ref, *out_refs, tq, half_w, seq, win, stats):
    j = pl.program_id(1)
    i = pl.program_id(2)
    q0 = i * tq
    ks = pl.multiple_of(jnp.clip(q0 - half_w, 0, seq - win), 16)
    q = q_ref[0]
    k = k_ref[0, pl.ds(ks, win), :]
    v = v_ref[0, pl.ds(ks, win), :]
    lane_lo = lax.broadcasted_iota(I32, (1, LANES), 1) < HEAD_DIM
    qpos = q0 + lax.broadcasted_iota(I32, (tq, 1), 0)
    kpos = ks + lax.broadcasted_iota(I32, (1, win), 1)
    dist = jnp.abs(kpos - qpos)
    valid = dist <= half_w
    distf = dist.astype(F32)
    res = []
    for hf in range(2):
        sel = lane_lo if hf == 0 else jnp.logical_not(lane_lo)
        qm = jnp.where(sel, q, jnp.zeros_like(q))
        s = lax.dot_general(qm, k, _NT, preferred_element_type=F32)
        slope = slope_ref[j, hf]
        sink = sink_ref[j, hf]
        s = jnp.where(valid, s - slope * distf, -jnp.inf)
        m = jnp.maximum(jnp.max(s, axis=-1, keepdims=True), sink)
        p = jnp.exp(s - m)
        l = jnp.sum(p, axis=-1, keepdims=True) + jnp.exp(sink - m)
        o = jnp.dot(p.astype(BF16), v, preferred_element_type=F32) / l
        res.append((o, m, l))
    o_ref = out_refs[0]
    o_ref[0] = jnp.where(lane_lo, res[0][0], res[1][0]).astype(o_ref.dtype)
    if stats:
        m_ref, l_ref = out_refs[1], out_refs[2]
        m_ref[0] = jnp.where(lane_lo, res[0][1], res[1][1])
        l_ref[0] = jnp.where(lane_lo, res[0][2], res[1][2])


def banded_attention(arr_q, arr_k, arr_v, q_blk0, k_blk0, v_blk0, kv_step, n_blk, half_w,
                     slopes, sinks, stats):
    n, seq, _ = arr_q.shape
    tq = min(TQ_BAND, seq - 2 * half_w)
    assert tq > 0 and seq % tq == 0 and tq % 16 == 0 and half_w % 16 == 0
    win = tq + 2 * half_w
    out_sds = [jax.ShapeDtypeStruct((n, seq, n_blk * LANES), BF16)]
    out_specs = [pl.BlockSpec((1, tq, LANES), lambda b, j, i: (b, i, j))]
    if stats:
        out_sds += [jax.ShapeDtypeStruct((n, seq, n_blk * LANES), F32)] * 2
        out_specs += [pl.BlockSpec((1, tq, LANES), lambda b, j, i: (b, i, j))] * 2
    smem = pl.BlockSpec(memory_space=pltpu.SMEM)
    body = functools.partial(_banded_body, tq=tq, half_w=half_w, seq=seq, win=win, stats=stats)
    out = pl.pallas_call(
        body,
        grid=(n, n_blk, seq // tq),
        in_specs=[
            smem, smem,
            pl.BlockSpec((1, tq, LANES), lambda b, j, i: (b, i, q_blk0 + j)),
            pl.BlockSpec((1, seq, LANES), lambda b, j, i: (b, 0, k_blk0 + kv_step * j)),
            pl.BlockSpec((1, seq, LANES), lambda b, j, i: (b, 0, v_blk0 + kv_step * j)),
        ],
        out_specs=out_specs,
        out_shape=out_sds,
        compiler_params=_cparams(("parallel", "parallel", "parallel")),
        name="banded_attention",
    )(slopes, sinks, arr_q, arr_k, arr_v)
    return out


def _merge_branches_body(*refs):
    o_ref = refs[-1]
    n = (len(refs) - 1) // 3
    os_ = [refs[3 * i][...].astype(F32) for i in range(n)]
    ms = [refs[3 * i + 1][...] for i in range(n)]
    ls = [refs[3 * i + 2][...] for i in range(n)]
    m_all = functools.reduce(jnp.maximum, ms)
    ws = [l * jnp.exp(m - m_all) for m, l in zip(ms, ls)]
    num = functools.reduce(lambda a, b: a + b, [w * o for w, o in zip(ws, os_)])
    den = functools.reduce(lambda a, b: a + b, ws)
    o_ref[...] = (num / den).astype(o_ref.dtype)


def merge_branches(branches):
    t, w = branches[0][0].shape
    tm = min(TM_PROJ, t)
    flat = [a for br in branches for a in br]
    spec = pl.BlockSpec((tm, w), lambda i: (i, 0))
    return pl.pallas_call(
        _merge_branches_body,
        grid=(t // tm,),
        in_specs=[spec] * len(flat),
        out_specs=spec,
        out_shape=jax.ShapeDtypeStruct((t, w), BF16),
        compiler_params=_cparams(("parallel",)),
        name="merge_branches",
    )(*flat)


def _flash_body(slope_ref, lami_ref, q_ref, k_ref, v_ref, bias_ref, lq1_ref, lk1_ref, lq2_ref, lk2_ref,
                sub_ref, o_ref, m_s, l_s, acc_s, *, t):
    h = pl.program_id(1)
    qi = pl.program_id(2)
    ki = pl.program_id(3)

    @pl.when(ki == 0)
    def _():
        m_s[...] = jnp.full(m_s.shape, -jnp.inf, F32)
        l_s[...] = jnp.zeros(l_s.shape, F32)
        acc_s[...] = jnp.zeros(acc_s.shape, F32)

    c_tile = -(slope_ref[h] * (jnp.abs(qi - ki) * t).astype(F32))
    q = q_ref[0]
    k = k_ref[0]
    v = v_ref[0]
    bias = bias_ref[0, 0]
    lane_lo = lax.broadcasted_iota(I32, (1, LANES), 1) < HEAD_DIM
    for mp in range(2):
        sel = lane_lo if mp == 0 else jnp.logical_not(lane_lo)
        qm = jnp.where(sel, q, jnp.zeros_like(q))
        s = lax.dot_general(qm, k, _NT, preferred_element_type=F32) + bias
        m_old = m_s[mp]
        m_new = jnp.maximum(m_old, jnp.max(s, axis=-1, keepdims=True) + c_tile)
        alpha = jnp.exp(m_old - m_new)
        p = jnp.exp(s - (m_new - c_tile))
        l_s[mp] = alpha * l_s[mp] + jnp.sum(p, axis=-1, keepdims=True)
        acc_s[mp] = alpha * acc_s[mp] + jnp.dot(p.astype(BF16), v, preferred_element_type=F32)
        m_s[mp] = m_new

    @pl.when(ki == pl.num_programs(3) - 1)
    def _():
        lam_init = lami_ref[0]
        lam = (jnp.exp(jnp.sum(lq1_ref[...] * lk1_ref[...], axis=-1, keepdims=True))
               - jnp.exp(jnp.sum(lq2_ref[...] * lk2_ref[...], axis=-1, keepdims=True)) + lam_init)
        o = acc_s[0] / l_s[0] - lam * (acc_s[1] / l_s[1])
        y = o * _rms_scale(o) * sub_ref[...]
        o_ref[0] = (y * (1.0 - lam_init)).astype(o_ref.dtype)


def _alibi_tile_bias(slopes, t):
    r = jnp.arange(t, dtype=F32)
    d = r[:, None] - r[None, :]
    base = jnp.stack([-d, -jnp.abs(d), d])
    return slopes[:, None, None, None] * base[None]


def diff_attention(proj3d, q_blk0, k_blk0, v_blk0, slopes, lam_init, lq1, lk1, lq2, lk2, subln):
    bsz, seq, _ = proj3d.shape
    t = min(T_FLASH, seq)
    assert seq % t == 0
    nt = seq // t
    bias = _alibi_tile_bias(slopes, t)
    smem = pl.BlockSpec(memory_space=pltpu.SMEM)
    vec = lambda n: pl.BlockSpec((1, n), lambda b, h, qi, ki: (0, 0))
    return pl.pallas_call(
        functools.partial(_flash_body, t=t),
        grid=(bsz, N_HEADS_C, nt, nt),
        in_specs=[
            smem, smem,
            pl.BlockSpec((1, t, LANES), lambda b, h, qi, ki: (b, qi, q_blk0 + h)),
            pl.BlockSpec((1, t, LANES), lambda b, h, qi, ki: (b, ki, k_blk0 + h)),
            pl.BlockSpec((1, t, LANES), lambda b, h, qi, ki: (b, ki, v_blk0 + h)),
            pl.BlockSpec((1, 1, t, t), lambda b, h, qi, ki: (h, 1 + jnp.clip(ki - qi, -1, 1), 0, 0)),
            vec(HEAD_DIM), vec(HEAD_DIM), vec(HEAD_DIM), vec(HEAD_DIM), vec(2 * HEAD_DIM),
        ],
        out_specs=pl.BlockSpec((1, t, LANES), lambda b, h, qi, ki: (b, qi, h)),
        out_shape=jax.ShapeDtypeStruct((bsz, seq, N_HEADS_C * LANES), BF16),
        scratch_shapes=[
            pltpu.VMEM((2, t, 1), F32),
            pltpu.VMEM((2, t, 1), F32),
            pltpu.VMEM((2, t, LANES), F32),
        ],
        compiler_params=_cparams(("parallel", "parallel", "parallel", "arbitrary")),
        name="diff_attention",
    )(slopes, lam_init, proj3d, proj3d, proj3d, bias, lq1, lk1, lq2, lk2, subln)


def _outproj_body(x_ref, oa_ref, ob_ref, oc_ref, wa_ref, wb_ref, wc_ref, g1_ref, sc_ref, sh_ref,
                  n2_ref, wr_ref, br_ref, x1_ref, hp_ref, lg_ref):
    mix = (jnp.dot(oa_ref[...], wa_ref[...], preferred_element_type=F32)
           + jnp.dot(ob_ref[...], wb_ref[...], preferred_element_type=F32)
           + jnp.dot(oc_ref[...], wc_ref[...], preferred_element_type=F32))
    x1 = x_ref[...] + g1_ref[0] * mix
    x1_ref[...] = x1
    h = x1 * _rms_scale(x1) * n2_ref[...]
    h = h * (1.0 + sc_ref[0]) + sh_ref[0]
    half = h.shape[1] // 2
    hp_ref[...] = _pack_halves(h[:, :half], h[:, half:])
    lg_ref[...] = lax.dot_general(wr_ref[...], h, _NT, preferred_element_type=F32,
                                  precision=HIGHEST) + br_ref[...]


def out_projection(x2d, oa, ob, oc, wa, wb, wc, gate1, scale2, shift2, gain2, w_router_t, b_router, seq):
    t, d = x2d.shape
    e = w_router_t.shape[0]
    tm = min(TM_PROJ, seq)
    mod_spec = pl.BlockSpec((1, 1, d), lambda i: ((i * tm) // seq, 0, 0))
    row = lambda w: pl.BlockSpec((tm, w), lambda i: (i, 0))
    full = lambda a: pl.BlockSpec(a.shape, lambda i: (0, 0))
    return pl.pallas_call(
        _outproj_body,
        grid=(t // tm,),
        in_specs=[row(d), row(oa.shape[1]), row(ob.shape[1]), row(oc.shape[1]),
                  full(wa), full(wb), full(wc), mod_spec, mod_spec, mod_spec,
                  pl.BlockSpec((1, d), lambda i: (0, 0)), full(w_router_t),
                  pl.BlockSpec((e, 1), lambda i: (0, 0))],
        out_specs=[row(d), row(d // 2), pl.BlockSpec((e, tm), lambda i: (0, i))],
        out_shape=[jax.ShapeDtypeStruct((t, d), F32),
                   jax.ShapeDtypeStruct((t, d // 2), U32),
                   jax.ShapeDtypeStruct((e, t), F32)],
        compiler_params=_cparams(("parallel",)),
        name="out_projection",
    )(x2d, oa, ob, oc, wa, wb, wc, gate1, scale2, shift2, gain2, w_router_t, b_router)


def _router_body(lg_ref, tri_ref, eidx_ref, gate_ref, rank_ref, cnt_ref, base_s):
    step = pl.program_id(0)

    @pl.when(step == 0)
    def _():
        base_s[...] = jnp.zeros(base_s.shape, F32)

    work = lg_ref[...]
    n_e, tm = work.shape
    eio = lax.broadcasted_iota(I32, (n_e, tm), 0).astype(F32)
    vals, hots = [], []
    for k in range(TOP_K):
        mx = jnp.max(work, axis=0, keepdims=True)
        idx = jnp.min(jnp.where(work == mx, eio, float(n_e)), axis=0, keepdims=True)
        hot = eio == idx
        vals.append(mx)
        hots.append(hot)
        eidx_ref[k:k + 1, :] = idx.astype(I32)
        work = jnp.where(hot, -jnp.inf, work)
    exps = [jnp.exp(v - vals[0]) for v in vals]
    den = functools.reduce(lambda a, b: a + b, exps)
    for k in range(TOP_K):
        gate_ref[k:k + 1, :] = exps[k] / den
    any_hot = functools.reduce(lambda a, b: a + b, [h.astype(F32) for h in hots])
    prefix = jnp.dot(any_hot.astype(BF16), tri_ref[...], preferred_element_type=F32)
    pos = base_s[...] + prefix
    for k in range(TOP_K):
        rank_ref[k:k + 1, :] = jnp.sum(jnp.where(hots[k], pos, 0.0), axis=0, keepdims=True).astype(I32)
    base_s[...] = base_s[...] + jnp.sum(any_hot, axis=1, keepdims=True)
    cnt_ref[...] = base_s[...].astype(I32)


def route(logits_t):
    n_e, t = logits_t.shape
    tm = min(TM_ROUTE, t)
    assert t % tm == 0
    r = jnp.arange(tm)
    tri = (r[:, None] < r[None, :]).astype(BF16)
    out4 = lambda dt: jax.ShapeDtypeStruct((TOP_K, t), dt)
    spec4 = pl.BlockSpec((TOP_K, tm), lambda i: (0, i))
    return pl.pallas_call(
        _router_body,
        grid=(t // tm,),
        in_specs=[pl.BlockSpec((n_e, tm), lambda i: (0, i)),
                  pl.BlockSpec((tm, tm), lambda i: (0, 0))],
        out_specs=[spec4, spec4, spec4, pl.BlockSpec((n_e, 1), lambda i: (0, 0))],
        out_shape=[out4(I32), out4(F32), out4(I32), jax.ShapeDtypeStruct((n_e, 1), I32)],
        scratch_shapes=[pltpu.VMEM((n_e, 1), F32)],
        compiler_params=_cparams(("arbitrary",)),
        name="route",
    )(logits_t, tri)


def _dispatch_body(dest_ref, h_ref, init_ref, xs_ref, sem, *, tm):
    del init_ref

    def issue(tok, carry):
        for k in range(TOP_K):
            pltpu.make_async_copy(h_ref.at[pl.ds(tok, 1)], xs_ref.at[pl.ds(dest_ref[k, tok], 1)], sem).start()
        return carry

    lax.fori_loop(0, tm, issue, 0)
    for k in range(TOP_K):
        pltpu.make_async_copy(h_ref, xs_ref.at[pl.ds(0, tm)], sem).wait()


def dispatch_rows(h_packed, dest, n_rows):
    t, w = h_packed.shape
    tm = min(TM_DISPATCH, t)
    return pl.pallas_call(
        functools.partial(_dispatch_body, tm=tm),
        grid=(t // tm,),
        in_specs=[pl.BlockSpec((TOP_K, tm), lambda i: (0, i), memory_space=pltpu.SMEM),
                  pl.BlockSpec((tm, w), lambda i: (i, 0)),
                  pl.BlockSpec(memory_space=pl.ANY)],
        out_specs=pl.BlockSpec(memory_space=pl.ANY),
        out_shape=jax.ShapeDtypeStruct((n_rows, w), U32),
        input_output_aliases={2: 0},
        scratch_shapes=[pltpu.SemaphoreType.DMA],
        compiler_params=_cparams(("arbitrary",)),
        name="dispatch_rows",
    )(dest, h_packed, jnp.zeros((n_rows, w), U32))


def _ffn_body(be_ref, bv_ref, nu_ref, xs_ref, wgu_ref, bgu_ref, wd_ref, bd_ref, o_ref):
    i = pl.program_id(0)

    @pl.when(i < nu_ref[0])
    def _():
        xa, xb = _unpack_halves(xs_ref[...])
        blk, half = xa.shape
        valid = lax.broadcasted_iota(I32, (blk, 1), 0) < bv_ref[i]
        xa = jnp.where(valid, xa, 0.0).astype(BF16)
        xb = jnp.where(valid, xb, 0.0).astype(BF16)
        au = (jnp.dot(xa, wgu_ref[0, :half, :], preferred_element_type=F32)
              + jnp.dot(xb, wgu_ref[0, half:, :], preferred_element_type=F32) + bgu_ref[0])
        f = au.shape[1] // 2
        a = jnp.minimum(au[:, :f], SWIGLU_LIMIT)
        u = jnp.clip(au[:, f:], -SWIGLU_LIMIT, SWIGLU_LIMIT)
        hid = a * (1.0 / (1.0 + jnp.exp(-SWIGLU_ALPHA * a))) * (u + 1.0)
        out = jnp.dot(hid.astype(BF16), wd_ref[0], preferred_element_type=F32) + bd_ref[0]
        d = out.shape[1] // 2
        o_ref[...] = _pack_halves(out[:, :d], out[:, d:])

    @pl.when(i >= nu_ref[0])
    def _():
        o_ref[...] = jnp.zeros(o_ref.shape, o_ref.dtype)


def expert_ffn(xs, blk_expert, blk_valid, n_used, wgu, bgu, wd, bd):
    n_rows, w = xs.shape
    n_e, d, f2 = wgu.shape
    nb = n_rows // BLK_FFN
    row_map = lambda i, be, bv, nu: (jnp.minimum(i, nu[0] - 1), 0)
    exp_map = lambda i, be, bv, nu: (be[i], 0, 0)
    grid_spec = pltpu.PrefetchScalarGridSpec(
        num_scalar_prefetch=3,
        grid=(nb,),
        in_specs=[pl.BlockSpec((BLK_FFN, w), row_map),
                  pl.BlockSpec((1, d, f2), exp_map),
                  pl.BlockSpec((1, 1, f2), exp_map),
                  pl.BlockSpec((1, f2 // 2, d), exp_map),
                  pl.BlockSpec((1, 1, d), exp_map)],
        out_specs=pl.BlockSpec((BLK_FFN, d // 2), lambda i, be, bv, nu: (i, 0)),
    )
    return pl.pallas_call(
        _ffn_body,
        grid_spec=grid_spec,
        out_shape=jax.ShapeDtypeStruct((n_rows, d // 2), U32),
        compiler_params=_cparams(("arbitrary",)),
        name="expert_ffn",
    )(blk_expert, blk_valid, n_used, xs, wgu, bgu, wd, bd)


def _combine_body(dest_ref, x_ref, gt_ref, g2_ref, nf_ref, ys_ref, o_ref, buf, sem, *, tm, final):
    def issue(tok, carry):
        for k in range(TOP_K):
            pltpu.make_async_copy(ys_ref.at[pl.ds(dest_ref[k, tok], 1)], buf.at[k, pl.ds(tok, 1)], sem).start()
        return carry

    lax.fori_loop(0, tm, issue, 0)
    for k in range(TOP_K):
        pltpu.make_async_copy(ys_ref.at[pl.ds(0, tm)], buf.at[k], sem).wait()
    gt = gt_ref[...]
    lo = hi = None
    for k in range(TOP_K):
        a, b = _unpack_halves(buf[k])
        g = gt[:, k:k + 1]
        lo = g * a if lo is None else lo + g * a
        hi = g * b if hi is None else hi + g * b
    half = lo.shape[1]
    g2 = g2_ref[0]
    x_lo = x_ref[:, :half] + g2[:, :half] * lo
    x_hi = x_ref[:, half:] + g2[:, half:] * hi
    if final:
        ms = (jnp.sum(x_lo * x_lo, axis=-1, keepdims=True)
              + jnp.sum(x_hi * x_hi, axis=-1, keepdims=True)) / (2 * half)
        r = lax.rsqrt(ms + RMS_EPS)
        x_lo = x_lo * r * nf_ref[:, :half]
        x_hi = x_hi * r * nf_ref[:, half:]
    o_ref[:, :half] = x_lo
    o_ref[:, half:] = x_hi


def combine(x1, ys, dest, gates_t, gate2, norm_f, seq, final):
    t, d = x1.shape
    tm = min(TM_COMBINE, seq)
    return pl.pallas_call(
        functools.partial(_combine_body, tm=tm, final=final),
        grid=(t // tm,),
        in_specs=[pl.BlockSpec((TOP_K, tm), lambda i: (0, i), memory_space=pltpu.SMEM),
                  pl.BlockSpec((tm, d), lambda i: (i, 0)),
                  pl.BlockSpec((tm, TOP_K), lambda i: (i, 0)),
                  pl.BlockSpec((1, 1, d), lambda i: ((i * tm) // seq, 0, 0)),
                  pl.BlockSpec((1, d), lambda i: (0, 0)),
                  pl.BlockSpec(memory_space=pl.ANY)],
        out_specs=pl.BlockSpec((tm, d), lambda i: (i, 0)),
        out_shape=jax.ShapeDtypeStruct((t, d), F32),
        scratch_shapes=[pltpu.VMEM((TOP_K, tm, d // 2), U32), pltpu.SemaphoreType.DMA],
        compiler_params=_cparams(("arbitrary",)),
        name="combine",
    )(dest, x1, gates_t, gate2, norm_f, ys)


def _alibi_slopes(n):
    return jnp.exp2(-8.0 * jnp.arange(1, n + 1, dtype=F32) / n)


def _routing_tables(counts, eidx, rank, nb):
    counts = counts.reshape(-1)
    padded = (counts + BLK_FFN - 1) // BLK_FFN * BLK_FFN
    pad_end = jnp.cumsum(padded)
    pad_start = pad_end - padded
    dest = pad_start[eidx] + rank
    blk_row0 = jnp.arange(nb, dtype=I32) * BLK_FFN
    blk_expert = jnp.minimum(jnp.searchsorted(pad_end, blk_row0, side="right"), N_EXPERTS - 1).astype(I32)
    blk_valid = jnp.clip(counts[blk_expert] - (blk_row0 - pad_start[blk_expert]), 0, BLK_FFN).astype(I32)
    n_used = (pad_end[-1:] // BLK_FFN).astype(I32)
    return dest.astype(I32), blk_expert, blk_valid, n_used


def kernel(x, c, w_ada, b_ada, norm1, w_in, sinks, lambda_q1, lambda_k1, lambda_q2, lambda_k2, subln,
           w_out, norm2, w_router, b_router, w_gate, b_gate, w_up, b_up, w_down, b_down, norm_f):
    bsz, seq, d = x.shape
    depth = w_ada.shape[0]
    t = bsz * seq
    wa_w, wb_w = N_HEADS_A * HEAD_DIM, N_HEADS_B * HEAD_DIM
    kvb_w = N_KV_B * HEAD_DIM
    wc_w = N_HEADS_C * 2 * HEAD_DIM
    off = [0]
    for wdt in (wa_w, wa_w, wa_w, wb_w, kvb_w, kvb_w, wc_w, wc_w, wc_w):
        off.append(off[-1] + wdt)
    blk = [o // LANES for o in off]

    group = N_HEADS_B // N_KV_B
    b_perm = [kv * group + g for g in range(group) for kv in range(N_KV_B)]
    col_perm = jnp.concatenate([jnp.arange(h * HEAD_DIM, (h + 1) * HEAD_DIM) for h in b_perm])
    q_scale = jnp.ones((off[-1],), F32)
    for s0, s1 in ((off[0], off[1]), (off[3], off[4]), (off[6], off[7])):
        q_scale = q_scale.at[s0:s1].set(HEAD_DIM ** -0.5)

    slopes_a = _alibi_slopes(N_HEADS_A)
    slopes_b = _alibi_slopes(N_HEADS_B)[jnp.array(b_perm)]
    slopes_c = _alibi_slopes(N_HEADS_C)
    neg_inf = jnp.full((N_HEADS_A // 2, 2), -jnp.inf, F32)

    mod_all = ada_modulation(c, w_ada, b_ada)
    x2d = x.reshape(t, d)
    nb = (t * TOP_K) // BLK_FFN + N_EXPERTS
    n_rows = nb * BLK_FFN

    for layer in range(depth):
        mod = mod_all[layer].reshape(bsz, N_MOD, 1, d)
        sh1, sc1, g1, sh2, sc2, g2 = [mod[:, i] for i in range(N_MOD)]

        w_l = w_in[layer] * q_scale[None, :]
        w_l = jnp.concatenate([w_l[:, :off[3]], w_l[:, off[3]:off[4]][:, col_perm], w_l[:, off[4]:]], axis=1)
        proj = in_projection(x2d, sc1, sh1, norm1[layer].reshape(1, d), w_l.astype(BF16), seq)
        proj3d = proj.reshape(bsz, seq, off[-1])

        branches = []
        for window, dil in DILATED_PAIRS:
            half_w = window // (2 * dil)
            ln = seq // dil
            if dil == 1:
                arr, b0 = proj3d, blk[0]
            else:
                qkv = proj3d[:, :, off[0]:off[3]].reshape(bsz, ln, dil, off[3])
                arr, b0 = qkv.transpose(0, 2, 1, 3).reshape(bsz * dil, ln, off[3]), 0
            nblk = wa_w // LANES
            o, m, l = banded_attention(arr, arr, arr, b0, b0 + nblk, b0 + 2 * nblk, 1, nblk, half_w,
                                       (slopes_a * dil).reshape(nblk, 2), neg_inf, True)
            if dil != 1:
                o, m, l = [a.reshape(bsz, dil, ln, wa_w).transpose(0, 2, 1, 3) for a in (o, m, l)]
            branches.append(tuple(a.reshape(t, wa_w) for a in (o, m, l)))
        o_a = merge_branches(branches)

        sink_l = sinks[layer].astype(F32)[jnp.array(b_perm)].reshape(group, N_KV_B)
        (o_b,) = banded_attention(proj3d, proj3d, proj3d, blk[3], blk[4], blk[5], 0, group, SWA_HALF,
                                  slopes_b.reshape(group, N_KV_B), sink_l, False)
        o_b = o_b.reshape(t, wb_w)

        lam_init = jnp.full((1,), 0.8 - 0.6 * math.exp(-0.3 * layer), F32)
        o_c = diff_attention(proj3d, blk[6], blk[7], blk[8], slopes_c, lam_init,
                             lambda_q1[layer].reshape(1, -1), lambda_k1[layer].reshape(1, -1),
                             lambda_q2[layer].reshape(1, -1), lambda_k2[layer].reshape(1, -1),
                             subln[layer].reshape(1, -1)).reshape(t, wc_w)

        wo = w_out[layer]
        wo_b = wo[wa_w:wa_w + wb_w].reshape(N_HEADS_B, HEAD_DIM, d)[jnp.array(b_perm)].reshape(wb_w, d)
        x1, h_packed, logits_t = out_projection(
            x2d, o_a, o_b, o_c, wo[:wa_w].astype(BF16), wo_b.astype(BF16), wo[wa_w + wb_w:].astype(BF16),
            g1, sc2, sh2, norm2[layer].reshape(1, d), w_router[layer].T, b_router[layer].reshape(-1, 1), seq)

        eidx, gates, rank, counts = route(logits_t)
        dest, blk_expert, blk_valid, n_used = _routing_tables(counts, eidx, rank, nb)
        xs = dispatch_rows(h_packed, dest, n_rows)
        wgu = jnp.concatenate([w_gate[layer], w_up[layer]], axis=-1).astype(BF16)
        bgu = jnp.concatenate([b_gate[layer], b_up[layer]], axis=-1).reshape(N_EXPERTS, 1, -1)
        ys = expert_ffn(xs, blk_expert, blk_valid, n_used, wgu, bgu, w_down[layer].astype(BF16),
                        b_down[layer].reshape(N_EXPERTS, 1, d))
        x2d = combine(x1, ys, dest, gates.T, g2, norm_f.reshape(1, d), seq, layer == depth - 1)

    return x2d.reshape(bsz, seq, d)
```

```python
import functools
import math

import jax
import jax.numpy as jnp
from jax import lax
from jax.experimental import pallas as pl
from jax.experimental.pallas import tpu as pltpu

F32 = jnp.float32
BF16 = jnp.bfloat16
I32 = jnp.int32
U32 = jnp.uint32

HEAD_DIM = 64
DILATED_PAIRS = ((128, 1), (512, 4), (2048, 16))
N_HEADS_A = 4
N_HEADS_B = 4
N_KV_B = 2
SWA_HALF = 128
N_HEADS_C = 4
N_EXPERTS = 32
TOP_K = 4
SWIGLU_LIMIT = 7.0
SWIGLU_ALPHA = 1.702
RMS_EPS = 1e-6
N_MOD = 6

LANES = 128
VMEM_LIMIT_BYTES = 56 * 1024 * 1024

TM_PROJ = 512
TQ_BAND = 256
T_FLASH = 1024
FLASH_PANEL = 256
FLASH_CHUNK = 256
FLASH_SUB = 128
FLASH_DV = 2 * HEAD_DIM
FLASH_ONES = 16
FLASH_AUG = 3
FLASH_SAFE_LOG2 = 60.0
LOG2E = math.log2(math.e)
TM_ROUTE = 512
TM_DISPATCH = 512
BLK_FFN = 512
TM_COMBINE = 256

HIGHEST = lax.Precision.HIGHEST
_NT = (((1,), (1,)), ((), ()))


def _cparams(sem):
    return pltpu.CompilerParams(dimension_semantics=sem, vmem_limit_bytes=VMEM_LIMIT_BYTES)


def _pack_halves(a, b):
    ua = lax.bitcast_convert_type(a.astype(BF16).astype(F32), U32)
    ub = lax.bitcast_convert_type(b.astype(BF16).astype(F32), U32)
    return ua | (ub >> 16)


def _unpack_halves(w):
    a = lax.bitcast_convert_type(w & jnp.uint32(0xFFFF0000), F32)
    b = lax.bitcast_convert_type(w << 16, F32)
    return a, b


def _rms_scale(x):
    return lax.rsqrt(jnp.mean(x * x, axis=-1, keepdims=True) + RMS_EPS)


def _ada_body(c_ref, w_ref, b_ref, o_ref):
    c = c_ref[...]
    cond = c / (1.0 + jnp.exp(-c))
    o_ref[0] = jnp.dot(cond, w_ref[0], preferred_element_type=F32, precision=HIGHEST) + b_ref[0]


def ada_modulation(c, w_ada, b_ada):
    depth, d, n = w_ada.shape
    bsz = c.shape[0]
    tn = 1536
    assert n % tn == 0
    return pl.pallas_call(
        _ada_body,
        grid=(depth, n // tn),
        in_specs=[
            pl.BlockSpec((bsz, d), lambda l, j: (0, 0)),
            pl.BlockSpec((1, d, tn), lambda l, j: (l, 0, j)),
            pl.BlockSpec((1, 1, tn), lambda l, j: (l, 0, j)),
        ],
        out_specs=pl.BlockSpec((1, bsz, tn), lambda l, j: (l, 0, j)),
        out_shape=jax.ShapeDtypeStruct((depth, bsz, n), F32),
        compiler_params=_cparams(("parallel", "parallel")),
        name="ada_modulation",
    )(c, w_ada, b_ada.reshape(depth, 1, n))


def _inproj_body(x_ref, sc_ref, sh_ref, g_ref, w_ref, o_ref):
    x = x_ref[...]
    h = x * _rms_scale(x) * g_ref[...]
    h = h * (1.0 + sc_ref[0]) + sh_ref[0]
    o_ref[...] = jnp.dot(h.astype(BF16), w_ref[...], preferred_element_type=F32).astype(o_ref.dtype)


def in_projection(x2d, scale, shift, gain, w_bf16, seq):
    t, d = x2d.shape
    n = w_bf16.shape[1]
    tm = min(TM_PROJ, seq)
    assert seq % tm == 0 and t % tm == 0
    mod_spec = pl.BlockSpec((1, 1, d), lambda i: ((i * tm) // seq, 0, 0))
    return pl.pallas_call(
        _inproj_body,
        grid=(t // tm,),
        in_specs=[
            pl.BlockSpec((tm, d), lambda i: (i, 0)),
            mod_spec, mod_spec,
            pl.BlockSpec((1, d), lambda i: (0, 0)),
            pl.BlockSpec((d, n), lambda i: (0, 0)),
        ],
        out_specs=pl.BlockSpec((tm, n), lambda i: (i, 0)),
        out_shape=jax.ShapeDtypeStruct((t, n), BF16),
        compiler_params=_cparams(("parallel",)),
        name="in_projection",
    )(x2d, scale, shift, gain, w_bf16)


def _banded_body(slope_ref, sink_ref, q_ref, k_ref, v_ref, *out_refs, tq, half_w, seq, win, stats):
    j = pl.program_id(1)
    i = pl.program_id(2)
    q0 = i * tq
    ks = pl.multiple_of(jnp.clip(q0 - half_w, 0, seq - win), 16)
    q = q_ref[0]
    k = k_ref[0, pl.ds(ks, win), :]
    v = v_ref[0, pl.ds(ks, win), :]
    lane_lo = lax.broadcasted_iota(I32, (1, LANES), 1) < HEAD_DIM
    qpos = q0 + lax.broadcasted_iota(I32, (tq, 1), 0)
    kpos = ks + lax.broadcasted_iota(I32, (1, win), 1)
    dist = jnp.abs(kpos - qpos)
    valid = dist <= half_w
    distf = dist.astype(F32)
    res = []
    for hf in range(2):
        sel = lane_lo if hf == 0 else jnp.logical_not(lane_lo)
        qm = jnp.where(sel, q, jnp.zeros_like(q))
        s = lax.dot_general(qm, k, _NT, preferred_element_type=F32)
        slope = slope_ref[j, hf]
        sink = sink_ref[j, hf]
        s = jnp.where(valid, s - slope * distf, -jnp.inf)
        m = jnp.maximum(jnp.max(s, axis=-1, keepdims=True), sink)
        p = jnp.exp(s - m)
        l = jnp.sum(p, axis=-1, keepdims=True) + jnp.exp(sink - m)
        o = jnp.dot(p.astype(BF16), v, preferred_element_type=F32) / l
        res.append((o, m, l))
    o_ref = out_refs[0]
    o_ref[0] = jnp.where(lane_lo, res[0][0], res[1][0]).astype(o_ref.dtype)
    if stats:
        m_ref, l_ref = out_refs[1], out_refs[2]
        m_ref[0] = jnp.where(lane_lo, res[0][1], res[1][1])
        l_ref[0] = jnp.where(lane_lo, res[0][2], res[1][2])


def banded_attention(arr_q, arr_k, arr_v, q_blk0, k_blk0, v_blk0, kv_step, n_blk, half_w,
                     slopes, sinks, stats):
    n, seq, _ = arr_q.shape
    tq = min(TQ_BAND, seq - 2 * half_w)
    assert tq > 0 and seq % tq == 0 and tq % 16 == 0 and half_w % 16 == 0
    win = tq + 2 * half_w
    out_sds = [jax.ShapeDtypeStruct((n, seq, n_blk * LANES), BF16)]
    out_specs = [pl.BlockSpec((1, tq, LANES), lambda b, j, i: (b, i, j))]
    if stats:
        out_sds += [jax.ShapeDtypeStruct((n, seq, n_blk * LANES), F32)] * 2
        out_specs += [pl.BlockSpec((1, tq, LANES), lambda b, j, i: (b, i, j))] * 2
    smem = pl.BlockSpec(memory_space=pltpu.SMEM)
    body = functools.partial(_banded_body, tq=tq, half_w=half_w, seq=seq, win=win, stats=stats)
    out = pl.pallas_call(
        body,
        grid=(n, n_blk, seq // tq),
        in_specs=[
            smem, smem,
            pl.BlockSpec((1, tq, LANES), lambda b, j, i: (b, i, q_blk0 + j)),
            pl.BlockSpec((1, seq, LANES), lambda b, j, i: (b, 0, k_blk0 + kv_step * j)),
            pl.BlockSpec((1, seq, LANES), lambda b, j, i: (b, 0, v_blk0 + kv_step * j)),
        ],
        out_specs=out_specs,
        out_shape=out_sds,
        compiler_params=_cparams(("parallel", "parallel", "parallel")),
        name="banded_attention",
    )(slopes, sinks, arr_q, arr_k, arr_v)
    return out


def _merge_branches_body(*refs):
    o_ref = refs[-1]
    n = (len(refs) - 1) // 3
    os_ = [refs[3 * i][...].astype(F32) for i in range(n)]
    ms = [refs[3 * i + 1][...] for i in range(n)]
    ls = [refs[3 * i + 2][...] for i in range(n)]
    m_all = functools.reduce(jnp.maximum, ms)
    ws = [l * jnp.exp(m - m_all) for m, l in zip(ms, ls)]
    num = functools.reduce(lambda a, b: a + b, [w * o for w, o in zip(ws, os_)])
    den = functools.reduce(lambda a, b: a + b, ws)
    o_ref[...] = (num / den).astype(o_ref.dtype)


def merge_branches(branches):
    t, w = branches[0][0].shape
    tm = min(TM_PROJ, t)
    flat = [a for br in branches for a in br]
    spec = pl.BlockSpec((tm, w), lambda i: (i, 0))
    return pl.pallas_call(
        _merge_branches_body,
        grid=(t // tm,),
        in_specs=[spec] * len(flat),
        out_specs=spec,
        out_shape=jax.ShapeDtypeStruct((t, w), BF16),
        compiler_params=_cparams(("parallel",)),
        name="merge_branches",
    )(*flat)


def _flash_body(slope_ref, lami_ref, qn_ref, kn_ref, qt_ref, k_ref, vt_ref, diag_ref, lq1_ref, lk1_ref,
                lq2_ref, lk2_ref, sub_ref, o_ref, m_s, acc_s, s_scr, mmin_s, *, t):
    b = pl.program_id(0)
    h = pl.program_id(1)
    qi = pl.program_id(2)
    ki = pl.program_id(3)
    nt = pl.num_programs(3)
    kt = lax.rem(qi + ki, nt)

    @pl.when(ki == 0)
    def _():
        m_s[...] = jnp.full(m_s.shape, -jnp.inf, F32)
        acc_s[...] = jnp.zeros(acc_s.shape, F32)
        mmin_s[0] = jnp.float32(0.0)

    slope = slope_ref[h]
    n_chunk = t // FLASH_CHUNK
    n_panel = t // FLASH_PANEL
    row = lax.broadcasted_iota(I32, (2 * FLASH_DV, 1), 0)
    row_lo = row < HEAD_DIM
    row_hi = jnp.logical_and(row >= HEAD_DIM, row < FLASH_DV)
    row_aug = jnp.logical_and(row >= FLASH_DV, row < FLASH_DV + FLASH_AUG)
    lane_pos = lax.broadcasted_iota(I32, (1, FLASH_PANEL), 1).astype(F32)
    blocks = [(qp, mp) for qp in range(n_panel) for mp in range(2)]

    def fold8(x, op):
        return functools.reduce(op, [x[r:r + 8] for r in range(0, x.shape[0], 8)])

    def pair_terms(qp, kc):
        gap = (qi * n_panel + qp) - (kt * n_chunk + kc)
        sign = jnp.sign(gap).astype(F32)
        lane_shift = -slope * ((jnp.abs(gap) * FLASH_CHUNK).astype(F32) + sign * lane_pos)
        return sign, lane_shift

    def masked_queries(bi):
        qp, mp = blocks[bi]
        qt = qt_ref[0, 0, :, qp * FLASH_PANEL:(qp + 1) * FLASH_PANEL]
        return jnp.where(row_lo if mp == 0 else row_hi, qt, jnp.zeros_like(qt))

    def score_chunk(bi, kc, qtm, mx):
        qp, _ = blocks[bi]
        k0 = kc * FLASH_CHUNK
        sign, lane_shift = pair_terms(qp, kc)
        qta = jnp.where(row_aug, sign.astype(BF16), qtm)
        s = jnp.dot(k_ref[0, k0:k0 + FLASH_CHUNK, :], qta, preferred_element_type=F32)
        if kc == qp:
            s = s + diag_ref[0, 0]
        s_scr[bi % 2, k0:k0 + FLASH_CHUNK, :] = s
        cm = fold8(s, jnp.maximum) + lane_shift
        return cm if mx is None else jnp.maximum(mx, cm)

    def prob_chunk(bi, kc, m_new, acc):
        qp, _ = blocks[bi]
        k0 = kc * FLASH_CHUNK
        shift = m_new - pair_terms(qp, kc)[1]
        parts = []
        for r0 in range(k0, k0 + FLASH_CHUNK, FLASH_SUB):
            parts.append(jnp.exp2((s_scr[bi % 2, r0:r0 + FLASH_SUB, :] - shift).astype(BF16)))
        pv = jnp.dot(vt_ref[0, 0, :, k0:k0 + FLASH_CHUNK], jnp.concatenate(parts, axis=0),
                     preferred_element_type=F32)
        return pv if acc is None else acc + pv

    def two_pass_tile():
        qtm_next = masked_queries(0)
        mx_next = None
        for kc in range(n_chunk):
            mx_next = score_chunk(0, kc, qtm_next, mx_next)
        for bi, (qp, mp) in enumerate(blocks):
            qs = slice(qp * FLASH_PANEL, (qp + 1) * FLASH_PANEL)
            m_old = m_s[mp, :, qs]
            m_new = jnp.maximum(m_old, jnp.max(mx_next, axis=0, keepdims=True))
            alpha = jnp.exp2(m_old - m_new)
            acc = None
            has_next = bi + 1 < len(blocks)
            if has_next:
                qtm_next = masked_queries(bi + 1)
            mx_next = None
            for kc in range(n_chunk):
                if has_next:
                    mx_next = score_chunk(bi + 1, kc, qtm_next, mx_next)
                acc = prob_chunk(bi, kc, m_new, acc)
            acc_s[mp, :, qs] = alpha * acc_s[mp, :, qs] + acc
            m_s[mp, :, qs] = m_new
        mmin_s[0] = jnp.min(m_s[...])

    def one_pass_tile():
        for bi, (qp, mp) in enumerate(blocks):
            qs = slice(qp * FLASH_PANEL, (qp + 1) * FLASH_PANEL)
            qtm = masked_queries(bi)
            m_ref = m_s[mp, :, qs]
            acc = None
            for kc in range(n_chunk):
                k0 = kc * FLASH_CHUNK
                sign, lane_shift = pair_terms(qp, kc)
                qta = jnp.where(row_aug, sign.astype(BF16), qtm)
                s = jnp.dot(k_ref[0, k0:k0 + FLASH_CHUNK, :], qta, preferred_element_type=F32)
                p = jnp.exp2(s - (m_ref - lane_shift)).astype(BF16)
                pv = jnp.dot(vt_ref[0, 0, :, k0:k0 + FLASH_CHUNK], p, preferred_element_type=F32)
                acc = pv if acc is None else acc + pv
            acc_s[mp, :, qs] = acc_s[mp, :, qs] + acc

    bound = qn_ref[(b * pl.num_programs(1) + h) * nt + qi] * kn_ref[(b * pl.num_programs(1) + h) * nt + kt]
    safe = jnp.logical_and(ki > 0, bound - mmin_s[0] <= FLASH_SAFE_LOG2)
    pl.when(safe)(one_pass_tile)
    pl.when(jnp.logical_not(safe))(two_pass_tile)

    @pl.when(ki == pl.num_programs(3) - 1)
    def _():
        lam_init = lami_ref[0]
        lam = (jnp.exp(jnp.sum(lq1_ref[...] * lk1_ref[...], axis=-1, keepdims=True))
               - jnp.exp(jnp.sum(lq2_ref[...] * lk2_ref[...], axis=-1, keepdims=True)) + lam_init)
        o1 = acc_s[0, :FLASH_DV, :] / acc_s[0, FLASH_DV:FLASH_DV + 1, :]
        o2 = acc_s[1, :FLASH_DV, :] / acc_s[1, FLASH_DV:FLASH_DV + 1, :]
        o = o1 - lam * o2
        r = lax.rsqrt(jnp.mean(o * o, axis=0, keepdims=True) + RMS_EPS)
        y = o * r * sub_ref[...] * (1.0 - lam_init)
        o_ref[0] = y.T.astype(o_ref.dtype)


def _split_bf16(x, n):
    parts = []
    for _ in range(n):
        p = x.astype(BF16)
        parts.append(p)
        x = x - p.astype(F32)
    return parts


def flash_operands(q, k, v, slopes):
    bsz, seq, nh, dv = q.shape
    qt = jnp.concatenate([q.transpose(0, 2, 3, 1), jnp.zeros((bsz, nh, dv, seq), BF16)], axis=2)
    pos = (jnp.arange(seq) % FLASH_CHUNK).astype(F32)
    aug = jnp.stack(_split_bf16(slopes[:, None] * pos[None, :], FLASH_AUG), axis=-1)
    aug = jnp.concatenate([aug, jnp.zeros((nh, seq, dv - FLASH_AUG), BF16)], axis=-1)
    k_aug = jnp.concatenate([k, jnp.broadcast_to(aug.transpose(1, 0, 2)[None], k.shape)], axis=-1)
    vt_ones = jnp.concatenate([v.transpose(0, 2, 3, 1), jnp.ones((bsz, nh, FLASH_ONES, seq), BF16)], axis=2)
    return qt, k_aug.reshape(bsz, seq, nh * 2 * dv), vt_ones


def _tile_norms_body(q_ref, k_ref, o_ref, *, nh):
    lane = lax.broadcasted_iota(I32, (1, 2 * nh), 1)
    out = jnp.zeros((1, 2 * nh), F32)
    for j, ref in enumerate((q_ref, k_ref)):
        for hh in range(nh):
            x = ref[0, :, hh * FLASH_DV:(hh + 1) * FLASH_DV].astype(F32)
            n2 = jnp.max(jnp.sum(x * x, axis=1, keepdims=True), axis=0, keepdims=True)
            out = jnp.where(lane == j * nh + hh, jnp.sqrt(n2), out)
    o_ref[0, 0] = out


def tile_norms(q3, k3, t):
    bsz, seq, w = q3.shape
    nh = w // FLASH_DV
    spec = pl.BlockSpec((1, t, w), lambda b, i: (b, i, 0))
    return pl.pallas_call(
        functools.partial(_tile_norms_body, nh=nh),
        grid=(bsz, seq // t),
        in_specs=[spec, spec],
        out_specs=pl.BlockSpec((1, 1, 1, 2 * nh), lambda b, i: (b, i, 0, 0)),
        out_shape=jax.ShapeDtypeStruct((bsz, seq // t, 1, 2 * nh), F32),
        compiler_params=_cparams(("parallel", "parallel")),
        name="tile_norms",
    )(q3, k3)


def diff_attention(qt, k_aug, vt_ones, norms, slopes, lam_init, lq1, lk1, lq2, lk2, subln):
    bsz, nh, _, seq = qt.shape
    t = min(T_FLASH, seq)
    assert seq % t == 0 and t % FLASH_PANEL == 0 and FLASH_PANEL == FLASH_CHUNK
    nt = seq // t
    norms = norms.reshape(bsz, nt, 2, nh).transpose(2, 0, 3, 1).reshape(2, bsz * nh * nt)
    dv = FLASH_DV
    dva = FLASH_DV + FLASH_ONES
    r = jnp.arange(FLASH_CHUNK, dtype=F32)
    diag = -slopes[:, None, None] * jnp.abs(r[None, :] - r[:, None])[None]
    diag = jnp.stack([jnp.zeros_like(diag), diag], axis=1)
    smem = pl.BlockSpec(memory_space=pltpu.SMEM)
    vec = lambda n: pl.BlockSpec((1, n), lambda b, h, qi, ki: (0, 0))
    return pl.pallas_call(
        functools.partial(_flash_body, t=t),
        grid=(bsz, nh, nt, nt),
        in_specs=[
            smem, smem, smem, smem,
            pl.BlockSpec((1, 1, 2 * dv, t), lambda b, h, qi, ki: (b, h, 0, qi)),
            pl.BlockSpec((1, t, 2 * dv), lambda b, h, qi, ki: (b, (qi + ki) % nt, h)),
            pl.BlockSpec((1, 1, dva, t), lambda b, h, qi, ki: (b, h, 0, (qi + ki) % nt)),
            pl.BlockSpec((1, 1, FLASH_CHUNK, FLASH_PANEL),
                         lambda b, h, qi, ki: (h, (ki == 0).astype(I32), 0, 0)),
            vec(HEAD_DIM), vec(HEAD_DIM), vec(HEAD_DIM), vec(HEAD_DIM),
            pl.BlockSpec((dv, 1), lambda b, h, qi, ki: (0, 0)),
        ],
        out_specs=pl.BlockSpec((1, t, LANES), lambda b, h, qi, ki: (b, qi, h)),
        out_shape=jax.ShapeDtypeStruct((bsz, seq, nh * dv), BF16),
        scratch_shapes=[
            pltpu.VMEM((2, 1, t), F32),
            pltpu.VMEM((2, dva, t), F32),
            pltpu.VMEM((2, t, FLASH_PANEL), F32),
            pltpu.SMEM((1,), F32),
        ],
        compiler_params=_cparams(("parallel", "parallel", "parallel", "arbitrary")),
        name="diff_attention",
    )(slopes, lam_init, norms[0], norms[1], qt, k_aug, vt_ones, diag, lq1, lk1, lq2, lk2, subln)


def _outproj_body(x_ref, oa_ref, ob_ref, oc_ref, wa_ref, wb_ref, wc_ref, g1_ref, sc_ref, sh_ref,
                  n2_ref, wr_ref, br_ref, x1_ref, hp_ref, lg_ref):
    mix = (jnp.dot(oa_ref[...], wa_ref[...], preferred_element_type=F32)
           + jnp.dot(ob_ref[...], wb_ref[...], preferred_element_type=F32)
           + jnp.dot(oc_ref[...], wc_ref[...], preferred_element_type=F32))
    x1 = x_ref[...] + g1_ref[0] * mix
    x1_ref[...] = x1
    h = x1 * _rms_scale(x1) * n2_ref[...]
    h = h * (1.0 + sc_ref[0]) + sh_ref[0]
    half = h.shape[1] // 2
    hp_ref[...] = _pack_halves(h[:, :half], h[:, half:])
    lg_ref[...] = lax.dot_general(wr_ref[...], h, _NT, preferred_element_type=F32,
                                  precision=HIGHEST) + br_ref[...]


def out_projection(x2d, oa, ob, oc, wa, wb, wc, gate1, scale2, shift2, gain2, w_router_t, b_router, seq):
    t, d = x2d.shape
    e = w_router_t.shape[0]
    tm = min(TM_PROJ, seq)
    mod_spec = pl.BlockSpec((1, 1, d), lambda i: ((i * tm) // seq, 0, 0))
    row = lambda w: pl.BlockSpec((tm, w), lambda i: (i, 0))
    full = lambda a: pl.BlockSpec(a.shape, lambda i: (0, 0))
    return pl.pallas_call(
        _outproj_body,
        grid=(t // tm,),
        in_specs=[row(d), row(oa.shape[1]), row(ob.shape[1]), row(oc.shape[1]),
                  full(wa), full(wb), full(wc), mod_spec, mod_spec, mod_spec,
                  pl.BlockSpec((1, d), lambda i: (0, 0)), full(w_router_t),
                  pl.BlockSpec((e, 1), lambda i: (0, 0))],
        out_specs=[row(d), row(d // 2), pl.BlockSpec((e, tm), lambda i: (0, i))],
        out_shape=[jax.ShapeDtypeStruct((t, d), F32),
                   jax.ShapeDtypeStruct((t, d // 2), U32),
                   jax.ShapeDtypeStruct((e, t), F32)],
        compiler_params=_cparams(("parallel",)),
        name="out_projection",
    )(x2d, oa, ob, oc, wa, wb, wc, gate1, scale2, shift2, gain2, w_router_t, b_router)


def _router_body(lg_ref, tri_ref, eidx_ref, gate_ref, rank_ref, cnt_ref, base_s):
    step = pl.program_id(0)

    @pl.when(step == 0)
    def _():
        base_s[...] = jnp.zeros(base_s.shape, F32)

    work = lg_ref[...]
    n_e, tm = work.shape
    eio = lax.broadcasted_iota(I32, (n_e, tm), 0).astype(F32)
    vals, hots = [], []
    for k in range(TOP_K):
        mx = jnp.max(work, axis=0, keepdims=True)
        idx = jnp.min(jnp.where(work == mx, eio, float(n_e)), axis=0, keepdims=True)
        hot = eio == idx
        vals.append(mx)
        hots.append(hot)
        eidx_ref[k:k + 1, :] = idx.astype(I32)
        work = jnp.where(hot, -jnp.inf, work)
    exps = [jnp.exp(v - vals[0]) for v in vals]
    den = functools.reduce(lambda a, b: a + b, exps)
    for k in range(TOP_K):
        gate_ref[k:k + 1, :] = exps[k] / den
    any_hot = functools.reduce(lambda a, b: a + b, [h.astype(F32) for h in hots])
    prefix = jnp.dot(any_hot.astype(BF16), tri_ref[...], preferred_element_type=F32)
    pos = base_s[...] + prefix
    for k in range(TOP_K):
        rank_ref[k:k + 1, :] = jnp.sum(jnp.where(hots[k], pos, 0.0), axis=0, keepdims=True).astype(I32)
    base_s[...] = base_s[...] + jnp.sum(any_hot, axis=1, keepdims=True)
    cnt_ref[...] = base_s[...].astype(I32)


def route(logits_t):
    n_e, t = logits_t.shape
    tm = min(TM_ROUTE, t)
    assert t % tm == 0
    r = jnp.arange(tm)
    tri = (r[:, None] < r[None, :]).astype(BF16)
    out4 = lambda dt: jax.ShapeDtypeStruct((TOP_K, t), dt)
    spec4 = pl.BlockSpec((TOP_K, tm), lambda i: (0, i))
    return pl.pallas_call(
        _router_body,
        grid=(t // tm,),
        in_specs=[pl.BlockSpec((n_e, tm), lambda i: (0, i)),
                  pl.BlockSpec((tm, tm), lambda i: (0, 0))],
        out_specs=[spec4, spec4, spec4, pl.BlockSpec((n_e, 1), lambda i: (0, 0))],
        out_shape=[out4(I32), out4(F32), out4(I32), jax.ShapeDtypeStruct((n_e, 1), I32)],
        scratch_shapes=[pltpu.VMEM((n_e, 1), F32)],
        compiler_params=_cparams(("arbitrary",)),
        name="route",
    )(logits_t, tri)


def _dispatch_body(dest_ref, h_ref, init_ref, xs_ref, sem, *, tm):
    del init_ref

    def issue(tok, carry):
        for k in range(TOP_K):
            pltpu.make_async_copy(h_ref.at[pl.ds(tok, 1)], xs_ref.at[pl.ds(dest_ref[k, tok], 1)], sem).start()
        return carry

    lax.fori_loop(0, tm, issue, 0)
    for k in range(TOP_K):
        pltpu.make_async_copy(h_ref, xs_ref.at[pl.ds(0, tm)], sem).wait()


def dispatch_rows(h_packed, dest, n_rows):
    t, w = h_packed.shape
    tm = min(TM_DISPATCH, t)
    return pl.pallas_call(
        functools.partial(_dispatch_body, tm=tm),
        grid=(t // tm,),
        in_specs=[pl.BlockSpec((TOP_K, tm), lambda i: (0, i), memory_space=pltpu.SMEM),
                  pl.BlockSpec((tm, w), lambda i: (i, 0)),
                  pl.BlockSpec(memory_space=pl.ANY)],
        out_specs=pl.BlockSpec(memory_space=pl.ANY),
        out_shape=jax.ShapeDtypeStruct((n_rows, w), U32),
        input_output_aliases={2: 0},
        scratch_shapes=[pltpu.SemaphoreType.DMA],
        compiler_params=_cparams(("arbitrary",)),
        name="dispatch_rows",
    )(dest, h_packed, jnp.zeros((n_rows, w), U32))


def _ffn_body(be_ref, bv_ref, nu_ref, xs_ref, wg_ref, bg_ref, wu_ref, bu_ref, wd_ref, bd_ref, o_ref,
              wg_s, wu_s, wd_s):
    i = pl.program_id(0)

    @pl.when(i < nu_ref[0])
    def _():
        @pl.when(jnp.logical_or(i == 0, be_ref[i] != be_ref[jnp.maximum(i - 1, 0)]))
        def _():
            wg_s[...] = wg_ref[0, 0].astype(BF16)
            wu_s[...] = wu_ref[0, 0].astype(BF16)
            wd_s[...] = wd_ref[0, 0].astype(BF16)

        xa, xb = _unpack_halves(xs_ref[...])
        blk, half = xa.shape
        valid = lax.broadcasted_iota(I32, (blk, 1), 0) < bv_ref[i]
        xa = jnp.where(valid, xa, 0.0).astype(BF16)
        xb = jnp.where(valid, xb, 0.0).astype(BF16)

        def proj(w_s, b_ref):
            return (jnp.dot(xa, w_s[:half, :], preferred_element_type=F32)
                    + jnp.dot(xb, w_s[half:, :], preferred_element_type=F32) + b_ref[0, 0])

        a = jnp.minimum(proj(wg_s, bg_ref), SWIGLU_LIMIT)
        u = jnp.clip(proj(wu_s, bu_ref), -SWIGLU_LIMIT, SWIGLU_LIMIT)
        hid = a * (1.0 / (1.0 + jnp.exp(-SWIGLU_ALPHA * a))) * (u + 1.0)
        out = jnp.dot(hid.astype(BF16), wd_s[...], preferred_element_type=F32) + bd_ref[0, 0]
        d = out.shape[1] // 2
        o_ref[...] = _pack_halves(out[:, :d], out[:, d:])

    @pl.when(i >= nu_ref[0])
    def _():
        o_ref[...] = jnp.zeros(o_ref.shape, o_ref.dtype)


def expert_ffn(xs, blk_expert, blk_valid, n_used, layer, wg, bg, wu, bu, wd, bd):
    n_rows, w = xs.shape
    _, n_e, d, f = wg.shape
    nb = n_rows // BLK_FFN
    row_map = lambda i, be, bv, nu: (jnp.minimum(i, nu[0] - 1), 0)
    exp_map = lambda i, be, bv, nu: (layer, be[i], 0, 0)
    grid_spec = pltpu.PrefetchScalarGridSpec(
        num_scalar_prefetch=3,
        grid=(nb,),
        in_specs=[pl.BlockSpec((BLK_FFN, w), row_map),
                  pl.BlockSpec((1, 1, d, f), exp_map), pl.BlockSpec((1, 1, 1, f), exp_map),
                  pl.BlockSpec((1, 1, d, f), exp_map), pl.BlockSpec((1, 1, 1, f), exp_map),
                  pl.BlockSpec((1, 1, f, d), exp_map), pl.BlockSpec((1, 1, 1, d), exp_map)],
        out_specs=pl.BlockSpec((BLK_FFN, d // 2), lambda i, be, bv, nu: (i, 0)),
        scratch_shapes=[pltpu.VMEM((d, f), BF16), pltpu.VMEM((d, f), BF16), pltpu.VMEM((f, d), BF16)],
    )
    return pl.pallas_call(
        _ffn_body,
        grid_spec=grid_spec,
        out_shape=jax.ShapeDtypeStruct((n_rows, d // 2), U32),
        compiler_params=_cparams(("arbitrary",)),
        name="expert_ffn",
    )(blk_expert, blk_valid, n_used, xs, wg, bg, wu, bu, wd, bd)


def _combine_body(dest_ref, x_ref, gt_ref, g2_ref, nf_ref, ys_ref, o_ref, buf, sem, *, tm, final):
    def issue(tok, carry):
        for k in range(TOP_K):
            pltpu.make_async_copy(ys_ref.at[pl.ds(dest_ref[k, tok], 1)], buf.at[k, pl.ds(tok, 1)], sem).start()
        return carry

    lax.fori_loop(0, tm, issue, 0)
    for k in range(TOP_K):
        pltpu.make_async_copy(ys_ref.at[pl.ds(0, tm)], buf.at[k], sem).wait()
    gt = gt_ref[...]
    lo = hi = None
    for k in range(TOP_K):
        a, b = _unpack_halves(buf[k])
        g = gt[:, k:k + 1]
        lo = g * a if lo is None else lo + g * a
        hi = g * b if hi is None else hi + g * b
    half = lo.shape[1]
    g2 = g2_ref[0]
    x_lo = x_ref[:, :half] + g2[:, :half] * lo
    x_hi = x_ref[:, half:] + g2[:, half:] * hi
    if final:
        ms = (jnp.sum(x_lo * x_lo, axis=-1, keepdims=True)
              + jnp.sum(x_hi * x_hi, axis=-1, keepdims=True)) / (2 * half)
        r = lax.rsqrt(ms + RMS_EPS)
        x_lo = x_lo * r * nf_ref[:, :half]
        x_hi = x_hi * r * nf_ref[:, half:]
    o_ref[:, :half] = x_lo
    o_ref[:, half:] = x_hi


def combine(x1, ys, dest, gates_t, gate2, norm_f, seq, final):
    t, d = x1.shape
    tm = min(TM_COMBINE, seq)
    return pl.pallas_call(
        functools.partial(_combine_body, tm=tm, final=final),
        grid=(t // tm,),
        in_specs=[pl.BlockSpec((TOP_K, tm), lambda i: (0, i), memory_space=pltpu.SMEM),
                  pl.BlockSpec((tm, d), lambda i: (i, 0)),
                  pl.BlockSpec((tm, TOP_K), lambda i: (i, 0)),
                  pl.BlockSpec((1, 1, d), lambda i: ((i * tm) // seq, 0, 0)),
                  pl.BlockSpec((1, d), lambda i: (0, 0)),
                  pl.BlockSpec(memory_space=pl.ANY)],
        out_specs=pl.BlockSpec((tm, d), lambda i: (i, 0)),
        out_shape=jax.ShapeDtypeStruct((t, d), F32),
        scratch_shapes=[pltpu.VMEM((TOP_K, tm, d // 2), U32), pltpu.SemaphoreType.DMA],
        compiler_params=_cparams(("arbitrary",)),
        name="combine",
    )(dest, x1, gates_t, gate2, norm_f, ys)


def _alibi_slopes(n):
    return jnp.exp2(-8.0 * jnp.arange(1, n + 1, dtype=F32) / n)


def _routing_tables(counts, eidx, rank, nb):
    counts = counts.reshape(-1)
    padded = (counts + BLK_FFN - 1) // BLK_FFN * BLK_FFN
    pad_end = jnp.cumsum(padded)
    pad_start = pad_end - padded
    e_ids = jnp.arange(N_EXPERTS, dtype=I32)[:, None, None]
    dest = rank + jnp.sum(jnp.where(eidx[None] == e_ids, pad_start[:, None, None], 0), axis=0)
    blk_row0 = jnp.arange(nb, dtype=I32) * BLK_FFN
    blk_expert = jnp.minimum(jnp.sum(pad_end[None, :] <= blk_row0[:, None], axis=1), N_EXPERTS - 1).astype(I32)
    blk_valid = jnp.clip(counts[blk_expert] - (blk_row0 - pad_start[blk_expert]), 0, BLK_FFN).astype(I32)
    n_used = (pad_end[-1:] // BLK_FFN).astype(I32)
    return dest.astype(I32), blk_expert, blk_valid, n_used


def kernel(x, c, w_ada, b_ada, norm1, w_in, sinks, lambda_q1, lambda_k1, lambda_q2, lambda_k2, subln,
           w_out, norm2, w_router, b_router, w_gate, b_gate, w_up, b_up, w_down, b_down, norm_f):
    bsz, seq, d = x.shape
    depth = w_ada.shape[0]
    t = bsz * seq
    wa_w, wb_w = N_HEADS_A * HEAD_DIM, N_HEADS_B * HEAD_DIM
    kvb_w = N_KV_B * HEAD_DIM
    wc_w = N_HEADS_C * 2 * HEAD_DIM
    off = [0]
    for wdt in (wa_w, wa_w, wa_w, wb_w, kvb_w, kvb_w, wc_w, wc_w, wc_w):
        off.append(off[-1] + wdt)
    blk = [o // LANES for o in off]

    group = N_HEADS_B // N_KV_B
    b_perm = [kv * group + g for g in range(group) for kv in range(N_KV_B)]
    col_perm = jnp.concatenate([jnp.arange(h * HEAD_DIM, (h + 1) * HEAD_DIM) for h in b_perm])
    q_scale = jnp.ones((off[-1],), F32)
    for s0, s1, extra in ((off[0], off[1], 1.0), (off[3], off[4], 1.0), (off[6], off[7], LOG2E)):
        q_scale = q_scale.at[s0:s1].set(HEAD_DIM ** -0.5 * extra)

    slopes_a = _alibi_slopes(N_HEADS_A)
    slopes_b = _alibi_slopes(N_HEADS_B)[jnp.array(b_perm)]
    slopes_c = _alibi_slopes(N_HEADS_C) * LOG2E
    neg_inf = jnp.full((N_HEADS_A // 2, 2), -jnp.inf, F32)

    mod_all = ada_modulation(c, w_ada, b_ada)
    x2d = x.reshape(t, d)
    nb = (t * TOP_K) // BLK_FFN + N_EXPERTS
    n_rows = nb * BLK_FFN

    for layer in range(depth):
        mod = mod_all[layer].reshape(bsz, N_MOD, 1, d)
        sh1, sc1, g1, sh2, sc2, g2 = [mod[:, i] for i in range(N_MOD)]

        w_l = w_in[layer] * q_scale[None, :]
        w_l = jnp.concatenate([w_l[:, :off[3]], w_l[:, off[3]:off[4]][:, col_perm], w_l[:, off[4]:]], axis=1)
        proj = in_projection(x2d, sc1, sh1, norm1[layer].reshape(1, d), w_l.astype(BF16), seq)
        proj3d = proj.reshape(bsz, seq, off[-1])

        branches = []
        for window, dil in DILATED_PAIRS:
            half_w = window // (2 * dil)
            ln = seq // dil
            if dil == 1:
                arr, b0 = proj3d, blk[0]
            else:
                qkv = proj3d[:, :, off[0]:off[3]].reshape(bsz, ln, dil, off[3])
                arr, b0 = qkv.transpose(0, 2, 1, 3).reshape(bsz * dil, ln, off[3]), 0
            nblk = wa_w // LANES
            o, m, l = banded_attention(arr, arr, arr, b0, b0 + nblk, b0 + 2 * nblk, 1, nblk, half_w,
                                       (slopes_a * dil).reshape(nblk, 2), neg_inf, True)
            if dil != 1:
                o, m, l = [a.reshape(bsz, dil, ln, wa_w).transpose(0, 2, 1, 3) for a in (o, m, l)]
            branches.append(tuple(a.reshape(t, wa_w) for a in (o, m, l)))
        o_a = merge_branches(branches)

        sink_l = sinks[layer].astype(F32)[jnp.array(b_perm)].reshape(group, N_KV_B)
        (o_b,) = banded_attention(proj3d, proj3d, proj3d, blk[3], blk[4], blk[5], 0, group, SWA_HALF,
                                  slopes_b.reshape(group, N_KV_B), sink_l, False)
        o_b = o_b.reshape(t, wb_w)

        lam_init = jnp.full((1,), 0.8 - 0.6 * math.exp(-0.3 * layer), F32)
        heads_c = lambda i: proj3d[:, :, off[i]:off[i + 1]].reshape(bsz, seq, N_HEADS_C, FLASH_DV)
        qt, k_aug, vt_ones = flash_operands(heads_c(6), heads_c(7), heads_c(8), slopes_c)
        norms = tile_norms(proj3d[:, :, off[6]:off[7]], proj3d[:, :, off[7]:off[8]], min(T_FLASH, seq))
        o_c = diff_attention(qt, k_aug, vt_ones, norms, slopes_c, lam_init,
                             lambda_q1[layer].reshape(1, -1), lambda_k1[layer].reshape(1, -1),
                             lambda_q2[layer].reshape(1, -1), lambda_k2[layer].reshape(1, -1),
                             subln[layer].reshape(-1, 1)).reshape(t, wc_w)

        wo = w_out[layer]
        wo_b = wo[wa_w:wa_w + wb_w].reshape(N_HEADS_B, HEAD_DIM, d)[jnp.array(b_perm)].reshape(wb_w, d)
        x1, h_packed, logits_t = out_projection(
            x2d, o_a, o_b, o_c, wo[:wa_w].astype(BF16), wo_b.astype(BF16), wo[wa_w + wb_w:].astype(BF16),
            g1, sc2, sh2, norm2[layer].reshape(1, d), w_router[layer].T, b_router[layer].reshape(-1, 1), seq)

        eidx, gates, rank, counts = route(logits_t)
        dest, blk_expert, blk_valid, n_used = _routing_tables(counts, eidx, rank, nb)
        xs = dispatch_rows(h_packed, dest, n_rows)
        ys = expert_ffn(xs, blk_expert, blk_valid, n_used, layer,
                        w_gate, b_gate[:, :, None, :], w_up, b_up[:, :, None, :], w_down, b_down[:, :, None, :])
        x2d = combine(x1, ys, dest, gates.T, g2, norm_f.reshape(1, d), seq, layer == depth - 1)

    return x2d.reshape(bsz, seq, d)
```

```python
import functools
import math

import jax
import jax.numpy as jnp
from jax import lax
from jax.experimental import pallas as pl
from jax.experimental.pallas import tpu as pltpu

F32 = jnp.float32
BF16 = jnp.bfloat16
I32 = jnp.int32
U32 = jnp.uint32

HEAD_DIM = 64
DILATED_PAIRS = ((128, 1), (512, 4), (2048, 16))
N_HEADS_A = 4
N_HEADS_B = 4
N_KV_B = 2
SWA_HALF = 128
N_HEADS_C = 4
N_EXPERTS = 32
TOP_K = 4
SWIGLU_LIMIT = 7.0
SWIGLU_ALPHA = 1.702
RMS_EPS = 1e-6
N_MOD = 6

LANES = 128
VMEM_LIMIT_BYTES = 56 * 1024 * 1024

TM_PROJ = 512
TQ_BAND = 256
T_FLASH = 1024
FLASH_PANEL = 512
FLASH_CHUNK = 256
FLASH_SUB = 128
FLASH_DV = 2 * HEAD_DIM
FLASH_ONES = 16
FLASH_AUG = 3
FLASH_LOOKAHEAD = 2
FLASH_SAFE_LOG2 = 60.0
LOG2E = math.log2(math.e)
TM_ROUTE = 512
TM_DISPATCH = 512
BLK_FFN = 512
TM_COMBINE = 256

HIGHEST = lax.Precision.HIGHEST
_NT = (((1,), (1,)), ((), ()))


def _cparams(sem):
    return pltpu.CompilerParams(dimension_semantics=sem, vmem_limit_bytes=VMEM_LIMIT_BYTES)


def _pack_halves(a, b):
    ua = lax.bitcast_convert_type(a.astype(BF16).astype(F32), U32)
    ub = lax.bitcast_convert_type(b.astype(BF16).astype(F32), U32)
    return ua | (ub >> 16)


def _unpack_halves(w):
    a = lax.bitcast_convert_type(w & jnp.uint32(0xFFFF0000), F32)
    b = lax.bitcast_convert_type(w << 16, F32)
    return a, b


def _rms_scale(x):
    return lax.rsqrt(jnp.mean(x * x, axis=-1, keepdims=True) + RMS_EPS)


def _ada_body(c_ref, w_ref, b_ref, o_ref):
    c = c_ref[...]
    cond = c / (1.0 + jnp.exp(-c))
    o_ref[0] = jnp.dot(cond, w_ref[0], preferred_element_type=F32, precision=HIGHEST) + b_ref[0]


def ada_modulation(c, w_ada, b_ada):
    depth, d, n = w_ada.shape
    bsz = c.shape[0]
    tn = 1536
    assert n % tn == 0
    return pl.pallas_call(
        _ada_body,
        grid=(depth, n // tn),
        in_specs=[
            pl.BlockSpec((bsz, d), lambda l, j: (0, 0)),
            pl.BlockSpec((1, d, tn), lambda l, j: (l, 0, j)),
            pl.BlockSpec((1, 1, tn), lambda l, j: (l, 0, j)),
        ],
        out_specs=pl.BlockSpec((1, bsz, tn), lambda l, j: (l, 0, j)),
        out_shape=jax.ShapeDtypeStruct((depth, bsz, n), F32),
        compiler_params=_cparams(("parallel", "parallel")),
        name="ada_modulation",
    )(c, w_ada, b_ada.reshape(depth, 1, n))


def _inproj_body(x_ref, sc_ref, sh_ref, g_ref, w_ref, o_ref):
    x = x_ref[...]
    h = x * _rms_scale(x) * g_ref[...]
    h = h * (1.0 + sc_ref[0]) + sh_ref[0]
    o_ref[...] = jnp.dot(h.astype(BF16), w_ref[...], preferred_element_type=F32).astype(o_ref.dtype)


def in_projection(x2d, scale, shift, gain, w_bf16, seq):
    t, d = x2d.shape
    n = w_bf16.shape[1]
    tm = min(TM_PROJ, seq)
    assert seq % tm == 0 and t % tm == 0
    mod_spec = pl.BlockSpec((1, 1, d), lambda i: ((i * tm) // seq, 0, 0))
    return pl.pallas_call(
        _inproj_body,
        grid=(t // tm,),
        in_specs=[
            pl.BlockSpec((tm, d), lambda i: (i, 0)),
            mod_spec, mod_spec,
            pl.BlockSpec((1, d), lambda i: (0, 0)),
            pl.BlockSpec((d, n), lambda i: (0, 0)),
        ],
        out_specs=pl.BlockSpec((tm, n), lambda i: (i, 0)),
        out_shape=jax.ShapeDtypeStruct((t, n), BF16),
        compiler_params=_cparams(("parallel",)),
        name="in_projection",
    )(x2d, scale, shift, gain, w_bf16)


def _offset_bias_body(c_ref, slope_ref, o_ref, *, pairs):
    vi = pl.program_id(0)
    hd = pl.program_id(1)
    tq, win = o_ref.shape[2], o_ref.shape[3]
    off = (lax.broadcasted_iota(I32, (tq, win), 1) - lax.broadcasted_iota(I32, (tq, win), 0)) + c_ref[vi]
    dist = jnp.abs(off)
    mult = jnp.zeros((tq, win), F32)
    for window, dil in pairs:
        hit = jnp.logical_and(dist <= window // 2, (dist & (dil - 1)) == 0)
        mult = mult + jnp.where(hit, 1.0, 0.0)
    o_ref[0, 0] = jnp.log(mult) - slope_ref[hd] * dist.astype(F32)


def _window_variants(seq, tq, reach):
    win = tq + 2 * reach
    nq, edge = seq // tq, -(-reach // tq)
    assert seq % tq == 0 and reach % 16 == 0 and win <= seq and nq >= 2 * edge + 1
    rel = [min(max(i * tq - reach, 0), seq - win) - i * tq for i in range(nq)]
    assert all(r == -reach for r in rel[edge:nq - edge])
    starts = rel[:edge] + [-reach] + rel[nq - edge:]

    def variant(i):
        return jnp.where(i < edge, i, jnp.where(i >= nq - edge, i - (nq - 2 * edge - 1), edge))

    return win, starts, variant


def _window_body(c_ref, sink_ref, q_ref, k_ref, v_ref, tab_ref, o_ref, *, tq, win, variant):
    j = pl.program_id(1)
    i = pl.program_id(2)
    ks = pl.multiple_of(i * tq + c_ref[variant(i)], 16)
    q = q_ref[0]
    k = k_ref[0, pl.ds(ks, win), :]
    v = v_ref[0, pl.ds(ks, win), :]
    lane_lo = lax.broadcasted_iota(I32, (1, LANES), 1) < HEAD_DIM
    outs = []
    for hf in range(2):
        qm = jnp.where(lane_lo if hf == 0 else jnp.logical_not(lane_lo), q, jnp.zeros_like(q))
        s = lax.dot_general(qm, k, _NT, preferred_element_type=F32) + tab_ref[0, hf]
        sink = sink_ref[j, hf]
        m = jnp.maximum(jnp.max(s, axis=-1, keepdims=True), sink)
        p = jnp.exp(s - m)
        l = jnp.sum(p, axis=-1, keepdims=True) + jnp.exp(sink - m)
        outs.append(jnp.dot(p.astype(BF16), v, preferred_element_type=F32) / l)
    o_ref[0] = jnp.where(lane_lo, outs[0], outs[1]).astype(o_ref.dtype)


def window_attention(proj3d, q_blk0, k_blk0, v_blk0, kv_step, n_blk, slopes, sinks, pairs):
    bsz, seq, _ = proj3d.shape
    assert all(d & (d - 1) == 0 for _, d in pairs)
    reach = max(w // 2 for w, _ in pairs)
    tq = TQ_BAND
    win, starts, variant = _window_variants(seq, tq, reach)
    starts = jnp.array(starts, I32)
    smem = pl.BlockSpec(memory_space=pltpu.SMEM)
    tables = pl.pallas_call(
        functools.partial(_offset_bias_body, pairs=pairs),
        grid=(starts.shape[0], 2 * n_blk),
        in_specs=[smem, smem],
        out_specs=pl.BlockSpec((1, 1, tq, win), lambda vi, hd: (vi, hd, 0, 0)),
        out_shape=jax.ShapeDtypeStruct((starts.shape[0], 2 * n_blk, tq, win), F32),
        compiler_params=_cparams(("parallel", "parallel")),
        name="offset_bias_tables",
    )(starts, slopes)
    return pl.pallas_call(
        functools.partial(_window_body, tq=tq, win=win, variant=variant),
        grid=(bsz, n_blk, seq // tq),
        in_specs=[
            smem, smem,
            pl.BlockSpec((1, tq, LANES), lambda b, j, i: (b, i, q_blk0 + j)),
            pl.BlockSpec((1, seq, LANES), lambda b, j, i: (b, 0, k_blk0 + kv_step * j)),
            pl.BlockSpec((1, seq, LANES), lambda b, j, i: (b, 0, v_blk0 + kv_step * j)),
            pl.BlockSpec((1, 2, tq, win), lambda b, j, i: (variant(i), j, 0, 0)),
        ],
        out_specs=pl.BlockSpec((1, tq, LANES), lambda b, j, i: (b, i, j)),
        out_shape=jax.ShapeDtypeStruct((bsz, seq, n_blk * LANES), BF16),
        compiler_params=_cparams(("parallel", "parallel", "parallel")),
        name="window_attention",
    )(starts, sinks, proj3d, proj3d, proj3d, tables)


def _flash_body(slope_ref, lami_ref, qn_ref, kn_ref, qt_ref, k_ref, vt_ref, diag_ref, lq1_ref, lk1_ref,
                lq2_ref, lk2_ref, sub_ref, o_ref, m_s, acc_s, s_scr, mmin_s, *, t):
    b = pl.program_id(0)
    h = pl.program_id(1)
    qi = pl.program_id(2)
    ki = pl.program_id(3)
    nt = pl.num_programs(3)
    kt = lax.rem(qi + ki, nt)

    @pl.when(ki == 0)
    def _():
        m_s[...] = jnp.full(m_s.shape, -jnp.inf, F32)
        acc_s[...] = jnp.zeros(acc_s.shape, F32)
        mmin_s[0] = jnp.float32(0.0)

    slope = slope_ref[h]
    n_chunk = t // FLASH_CHUNK
    n_panel = t // FLASH_PANEL
    sub_per_panel = FLASH_PANEL // FLASH_CHUNK
    row = lax.broadcasted_iota(I32, (2 * FLASH_DV, 1), 0)
    row_lo = row < HEAD_DIM
    row_hi = jnp.logical_and(row >= HEAD_DIM, row < FLASH_DV)
    row_aug = jnp.logical_and(row >= FLASH_DV, row < FLASH_DV + FLASH_AUG)
    lane_pos = lax.broadcasted_iota(I32, (1, FLASH_CHUNK), 1).astype(F32)
    blocks = [(qp, mp) for qp in range(n_panel) for mp in range(2)]

    def fold8(x, op):
        return functools.reduce(op, [x[r:r + 8] for r in range(0, x.shape[0], 8)])

    def pair_terms(qp, kc):
        signs, shifts = [], []
        for j in range(sub_per_panel):
            gap = (qi * n_chunk + qp * sub_per_panel + j) - (kt * n_chunk + kc)
            sign = jnp.sign(gap).astype(F32)
            signs.append(jnp.full((1, FLASH_CHUNK), sign, F32))
            shifts.append(-slope * ((jnp.abs(gap) * FLASH_CHUNK).astype(F32) + sign * lane_pos))
        return jnp.concatenate(signs, axis=1), jnp.concatenate(shifts, axis=1)

    def masked_queries(bi):
        qp, mp = blocks[bi]
        qt = qt_ref[0, 0, :, qp * FLASH_PANEL:(qp + 1) * FLASH_PANEL]
        return jnp.where(row_lo if mp == 0 else row_hi, qt, jnp.zeros_like(qt))

    def score_chunk(bi, kc, qtm, mx):
        qp, _ = blocks[bi]
        k0 = kc * FLASH_CHUNK
        sign, lane_shift = pair_terms(qp, kc)
        qta = jnp.where(row_aug, sign.astype(BF16), qtm)
        s = jnp.dot(k_ref[0, k0:k0 + FLASH_CHUNK, :], qta, preferred_element_type=F32)
        j = kc - qp * sub_per_panel
        if 0 <= j < sub_per_panel:
            cols = [s[:, i * FLASH_CHUNK:(i + 1) * FLASH_CHUNK] for i in range(sub_per_panel)]
            cols[j] = cols[j] + diag_ref[0, 0]
            s = jnp.concatenate(cols, axis=1)
        s_scr[bi % 2, k0:k0 + FLASH_CHUNK, :] = s
        cm = fold8(s, jnp.maximum) + lane_shift
        return cm if mx is None else jnp.maximum(mx, cm)

    def prob_chunk(bi, kc, m_new, acc):
        qp, _ = blocks[bi]
        k0 = kc * FLASH_CHUNK
        shift = m_new - pair_terms(qp, kc)[1]
        parts = []
        for r0 in range(k0, k0 + FLASH_CHUNK, FLASH_SUB):
            parts.append(jnp.exp2((s_scr[bi % 2, r0:r0 + FLASH_SUB, :] - shift).astype(BF16)))
        pv = jnp.dot(vt_ref[0, 0, :, k0:k0 + FLASH_CHUNK], jnp.concatenate(parts, axis=0),
                     preferred_element_type=F32)
        return pv if acc is None else acc + pv

    def two_pass_tile():
        qtm_next = masked_queries(0)
        mx_next = None
        for kc in range(n_chunk):
            mx_next = score_chunk(0, kc, qtm_next, mx_next)
        for bi, (qp, mp) in enumerate(blocks):
            qs = slice(qp * FLASH_PANEL, (qp + 1) * FLASH_PANEL)
            m_old = m_s[mp, :, qs]
            m_new = jnp.maximum(m_old, jnp.max(mx_next, axis=0, keepdims=True))
            alpha = jnp.exp2(m_old - m_new)
            acc = None
            has_next = bi + 1 < len(blocks)
            if has_next:
                qtm_next = masked_queries(bi + 1)
            mx_next = None
            for kc in range(n_chunk):
                if has_next:
                    mx_next = score_chunk(bi + 1, kc, qtm_next, mx_next)
                acc = prob_chunk(bi, kc, m_new, acc)
            acc_s[mp, :, qs] = alpha * acc_s[mp, :, qs] + acc
            m_s[mp, :, qs] = m_new
        mmin_s[0] = jnp.min(m_s[...])

    def one_pass_tile():
        items = [(bi, kc) for bi in range(len(blocks)) for kc in range(n_chunk)]

        def scores(bi, kc):
            qp, mp = blocks[bi]
            sign, lane_shift = pair_terms(qp, kc)
            qta = jnp.where(row_aug, sign.astype(BF16), masked_queries(bi))
            s = jnp.dot(k_ref[0, kc * FLASH_CHUNK:(kc + 1) * FLASH_CHUNK, :], qta, preferred_element_type=F32)
            return s, m_s[mp, :, qp * FLASH_PANEL:(qp + 1) * FLASH_PANEL] - lane_shift

        ahead = [scores(*items[i]) for i in range(FLASH_LOOKAHEAD)]
        acc = None
        for n, (bi, kc) in enumerate(items):
            s, shift = ahead.pop(0)
            if n + FLASH_LOOKAHEAD < len(items):
                ahead.append(scores(*items[n + FLASH_LOOKAHEAD]))
            p = jnp.exp2(s - shift).astype(BF16)
            pv = jnp.dot(vt_ref[0, 0, :, kc * FLASH_CHUNK:(kc + 1) * FLASH_CHUNK], p, preferred_element_type=F32)
            acc = pv if kc == 0 else acc + pv
            if kc == n_chunk - 1:
                qp, mp = blocks[bi]
                qs = slice(qp * FLASH_PANEL, (qp + 1) * FLASH_PANEL)
                acc_s[mp, :, qs] = acc_s[mp, :, qs] + acc

    bound = qn_ref[(b * pl.num_programs(1) + h) * nt + qi] * kn_ref[(b * pl.num_programs(1) + h) * nt + kt]
    safe = jnp.logical_and(ki > 0, bound - mmin_s[0] <= FLASH_SAFE_LOG2)
    pl.when(safe)(one_pass_tile)
    pl.when(jnp.logical_not(safe))(two_pass_tile)

    @pl.when(ki == pl.num_programs(3) - 1)
    def _():
        lam_init = lami_ref[0]
        lam = (jnp.exp(jnp.sum(lq1_ref[...] * lk1_ref[...], axis=-1, keepdims=True))
               - jnp.exp(jnp.sum(lq2_ref[...] * lk2_ref[...], axis=-1, keepdims=True)) + lam_init)
        o1 = acc_s[0, :FLASH_DV, :] / acc_s[0, FLASH_DV:FLASH_DV + 1, :]
        o2 = acc_s[1, :FLASH_DV, :] / acc_s[1, FLASH_DV:FLASH_DV + 1, :]
        o = o1 - lam * o2
        r = lax.rsqrt(jnp.mean(o * o, axis=0, keepdims=True) + RMS_EPS)
        y = o * r * sub_ref[...] * (1.0 - lam_init)
        o_ref[0] = y.T.astype(o_ref.dtype)


def _split_bf16(x, n):
    parts = []
    for _ in range(n):
        p = x.astype(BF16)
        parts.append(p)
        x = x - p.astype(F32)
    return parts


def flash_operands(q, k, v, slopes):
    bsz, seq, nh, dv = q.shape
    qt = jnp.concatenate([q.transpose(0, 2, 3, 1), jnp.zeros((bsz, nh, dv, seq), BF16)], axis=2)
    pos = (jnp.arange(seq) % FLASH_CHUNK).astype(F32)
    aug = jnp.stack(_split_bf16(slopes[:, None] * pos[None, :], FLASH_AUG), axis=-1)
    aug = jnp.concatenate([aug, jnp.zeros((nh, seq, dv - FLASH_AUG), BF16)], axis=-1)
    k_aug = jnp.concatenate([k, jnp.broadcast_to(aug.transpose(1, 0, 2)[None], k.shape)], axis=-1)
    vt_ones = jnp.concatenate([v.transpose(0, 2, 3, 1), jnp.ones((bsz, nh, FLASH_ONES, seq), BF16)], axis=2)
    return qt, k_aug.reshape(bsz, seq, nh * 2 * dv), vt_ones


def _tile_norms_body(q_ref, k_ref, o_ref, *, nh):
    lane = lax.broadcasted_iota(I32, (1, 2 * nh), 1)
    out = jnp.zeros((1, 2 * nh), F32)
    for j, ref in enumerate((q_ref, k_ref)):
        for hh in range(nh):
            x = ref[0, :, hh * FLASH_DV:(hh + 1) * FLASH_DV].astype(F32)
            n2 = jnp.max(jnp.sum(x * x, axis=1, keepdims=True), axis=0, keepdims=True)
            out = jnp.where(lane == j * nh + hh, jnp.sqrt(n2), out)
    o_ref[0, 0] = out


def tile_norms(q3, k3, t):
    bsz, seq, w = q3.shape
    nh = w // FLASH_DV
    spec = pl.BlockSpec((1, t, w), lambda b, i: (b, i, 0))
    return pl.pallas_call(
        functools.partial(_tile_norms_body, nh=nh),
        grid=(bsz, seq // t),
        in_specs=[spec, spec],
        out_specs=pl.BlockSpec((1, 1, 1, 2 * nh), lambda b, i: (b, i, 0, 0)),
        out_shape=jax.ShapeDtypeStruct((bsz, seq // t, 1, 2 * nh), F32),
        compiler_params=_cparams(("parallel", "parallel")),
        name="tile_norms",
    )(q3, k3)


def diff_attention(qt, k_aug, vt_ones, norms, slopes, lam_init, lq1, lk1, lq2, lk2, subln):
    bsz, nh, _, seq = qt.shape
    t = min(T_FLASH, seq)
    assert seq % t == 0 and t % FLASH_PANEL == 0 and FLASH_PANEL % FLASH_CHUNK == 0
    nt = seq // t
    norms = norms.reshape(bsz, nt, 2, nh).transpose(2, 0, 3, 1).reshape(2, bsz * nh * nt)
    dv = FLASH_DV
    dva = FLASH_DV + FLASH_ONES
    r = jnp.arange(FLASH_CHUNK, dtype=F32)
    diag = -slopes[:, None, None] * jnp.abs(r[None, :] - r[:, None])[None]
    diag = jnp.stack([jnp.zeros_like(diag), diag], axis=1)
    smem = pl.BlockSpec(memory_space=pltpu.SMEM)
    vec = lambda n: pl.BlockSpec((1, n), lambda b, h, qi, ki: (0, 0))
    return pl.pallas_call(
        functools.partial(_flash_body, t=t),
        grid=(bsz, nh, nt, nt),
        in_specs=[
            smem, smem, smem, smem,
            pl.BlockSpec((1, 1, 2 * dv, t), lambda b, h, qi, ki: (b, h, 0, qi)),
            pl.BlockSpec((1, t, 2 * dv), lambda b, h, qi, ki: (b, (qi + ki) % nt, h)),
            pl.BlockSpec((1, 1, dva, t), lambda b, h, qi, ki: (b, h, 0, (qi + ki) % nt)),
            pl.BlockSpec((1, 1, FLASH_CHUNK, FLASH_CHUNK),
                         lambda b, h, qi, ki: (h, (ki == 0).astype(I32), 0, 0)),
            vec(HEAD_DIM), vec(HEAD_DIM), vec(HEAD_DIM), vec(HEAD_DIM),
            pl.BlockSpec((dv, 1), lambda b, h, qi, ki: (0, 0)),
        ],
        out_specs=pl.BlockSpec((1, t, LANES), lambda b, h, qi, ki: (b, qi, h)),
        out_shape=jax.ShapeDtypeStruct((bsz, seq, nh * dv), BF16),
        scratch_shapes=[
            pltpu.VMEM((2, 1, t), F32),
            pltpu.VMEM((2, dva, t), F32),
            pltpu.VMEM((2, t, FLASH_PANEL), F32),
            pltpu.SMEM((1,), F32),
        ],
        compiler_params=_cparams(("parallel", "parallel", "parallel", "arbitrary")),
        name="diff_attention",
    )(slopes, lam_init, norms[0], norms[1], qt, k_aug, vt_ones, diag, lq1, lk1, lq2, lk2, subln)


def _outproj_body(x_ref, oa_ref, ob_ref, oc_ref, wa_ref, wb_ref, wc_ref, g1_ref, sc_ref, sh_ref,
                  n2_ref, wr_ref, br_ref, x1_ref, hp_ref, lg_ref):
    mix = (jnp.dot(oa_ref[...], wa_ref[...], preferred_element_type=F32)
           + jnp.dot(ob_ref[...], wb_ref[...], preferred_element_type=F32)
           + jnp.dot(oc_ref[...], wc_ref[...], preferred_element_type=F32))
    x1 = x_ref[...] + g1_ref[0] * mix
    x1_ref[...] = x1
    h = x1 * _rms_scale(x1) * n2_ref[...]
    h = h * (1.0 + sc_ref[0]) + sh_ref[0]
    half = h.shape[1] // 2
    hp_ref[...] = _pack_halves(h[:, :half], h[:, half:])
    lg_ref[...] = lax.dot_general(wr_ref[...], h, _NT, preferred_element_type=F32,
                                  precision=HIGHEST) + br_ref[...]


def out_projection(x2d, oa, ob, oc, wa, wb, wc, gate1, scale2, shift2, gain2, w_router_t, b_router, seq):
    t, d = x2d.shape
    e = w_router_t.shape[0]
    tm = min(TM_PROJ, seq)
    mod_spec = pl.BlockSpec((1, 1, d), lambda i: ((i * tm) // seq, 0, 0))
    row = lambda w: pl.BlockSpec((tm, w), lambda i: (i, 0))
    full = lambda a: pl.BlockSpec(a.shape, lambda i: (0, 0))
    return pl.pallas_call(
        _outproj_body,
        grid=(t // tm,),
        in_specs=[row(d), row(oa.shape[1]), row(ob.shape[1]), row(oc.shape[1]),
                  full(wa), full(wb), full(wc), mod_spec, mod_spec, mod_spec,
                  pl.BlockSpec((1, d), lambda i: (0, 0)), full(w_router_t),
                  pl.BlockSpec((e, 1), lambda i: (0, 0))],
        out_specs=[row(d), row(d // 2), pl.BlockSpec((e, tm), lambda i: (0, i))],
        out_shape=[jax.ShapeDtypeStruct((t, d), F32),
                   jax.ShapeDtypeStruct((t, d // 2), U32),
                   jax.ShapeDtypeStruct((e, t), F32)],
        compiler_params=_cparams(("parallel",)),
        name="out_projection",
    )(x2d, oa, ob, oc, wa, wb, wc, gate1, scale2, shift2, gain2, w_router_t, b_router)


def _router_body(lg_ref, tri_ref, eidx_ref, gate_ref, rank_ref, cnt_ref, base_s):
    step = pl.program_id(0)

    @pl.when(step == 0)
    def _():
        base_s[...] = jnp.zeros(base_s.shape, F32)

    work = lg_ref[...]
    n_e, tm = work.shape
    eio = lax.broadcasted_iota(I32, (n_e, tm), 0).astype(F32)
    vals, hots = [], []
    for k in range(TOP_K):
        mx = jnp.max(work, axis=0, keepdims=True)
        idx = jnp.min(jnp.where(work == mx, eio, float(n_e)), axis=0, keepdims=True)
        hot = eio == idx
        vals.append(mx)
        hots.append(hot)
        eidx_ref[k:k + 1, :] = idx.astype(I32)
        work = jnp.where(hot, -jnp.inf, work)
    exps = [jnp.exp(v - vals[0]) for v in vals]
    den = functools.reduce(lambda a, b: a + b, exps)
    for k in range(TOP_K):
        gate_ref[k:k + 1, :] = exps[k] / den
    any_hot = functools.reduce(lambda a, b: a + b, [h.astype(F32) for h in hots])
    prefix = jnp.dot(any_hot.astype(BF16), tri_ref[...], preferred_element_type=F32)
    pos = base_s[...] + prefix
    for k in range(TOP_K):
        rank_ref[k:k + 1, :] = jnp.sum(jnp.where(hots[k], pos, 0.0), axis=0, keepdims=True).astype(I32)
    base_s[...] = base_s[...] + jnp.sum(any_hot, axis=1, keepdims=True)
    cnt_ref[...] = base_s[...].astype(I32)


def route(logits_t):
    n_e, t = logits_t.shape
    tm = min(TM_ROUTE, t)
    assert t % tm == 0
    r = jnp.arange(tm)
    tri = (r[:, None] < r[None, :]).astype(BF16)
    out4 = lambda dt: jax.ShapeDtypeStruct((TOP_K, t), dt)
    spec4 = pl.BlockSpec((TOP_K, tm), lambda i: (0, i))
    return pl.pallas_call(
        _router_body,
        grid=(t // tm,),
        in_specs=[pl.BlockSpec((n_e, tm), lambda i: (0, i)),
                  pl.BlockSpec((tm, tm), lambda i: (0, 0))],
        out_specs=[spec4, spec4, spec4, pl.BlockSpec((n_e, 1), lambda i: (0, 0))],
        out_shape=[out4(I32), out4(F32), out4(I32), jax.ShapeDtypeStruct((n_e, 1), I32)],
        scratch_shapes=[pltpu.VMEM((n_e, 1), F32)],
        compiler_params=_cparams(("arbitrary",)),
        name="route",
    )(logits_t, tri)


def _dispatch_body(dest_ref, h_ref, init_ref, xs_ref, sem, *, tm):
    del init_ref

    def issue(tok, carry):
        for k in range(TOP_K):
            pltpu.make_async_copy(h_ref.at[pl.ds(tok, 1)], xs_ref.at[pl.ds(dest_ref[k, tok], 1)], sem).start()
        return carry

    lax.fori_loop(0, tm, issue, 0)
    for k in range(TOP_K):
        pltpu.make_async_copy(h_ref, xs_ref.at[pl.ds(0, tm)], sem).wait()


def dispatch_rows(h_packed, dest, n_rows):
    t, w = h_packed.shape
    tm = min(TM_DISPATCH, t)
    return pl.pallas_call(
        functools.partial(_dispatch_body, tm=tm),
        grid=(t // tm,),
        in_specs=[pl.BlockSpec((TOP_K, tm), lambda i: (0, i), memory_space=pltpu.SMEM),
                  pl.BlockSpec((tm, w), lambda i: (i, 0)),
                  pl.BlockSpec(memory_space=pl.ANY)],
        out_specs=pl.BlockSpec(memory_space=pl.ANY),
        out_shape=jax.ShapeDtypeStruct((n_rows, w), U32),
        input_output_aliases={2: 0},
        scratch_shapes=[pltpu.SemaphoreType.DMA],
        compiler_params=_cparams(("arbitrary",)),
        name="dispatch_rows",
    )(dest, h_packed, jnp.zeros((n_rows, w), U32))


def _ffn_body(be_ref, bv_ref, nu_ref, xs_ref, wg_ref, bg_ref, wu_ref, bu_ref, wd_ref, bd_ref, o_ref,
              wg_s, wu_s, wd_s):
    i = pl.program_id(0)

    @pl.when(i < nu_ref[0])
    def _():
        @pl.when(jnp.logical_or(i == 0, be_ref[i] != be_ref[jnp.maximum(i - 1, 0)]))
        def _():
            wg_s[...] = wg_ref[0, 0].astype(BF16)
            wu_s[...] = wu_ref[0, 0].astype(BF16)
            wd_s[...] = wd_ref[0, 0].astype(BF16)

        xa, xb = _unpack_halves(xs_ref[...])
        blk, half = xa.shape
        valid = lax.broadcasted_iota(I32, (blk, 1), 0) < bv_ref[i]
        xa = jnp.where(valid, xa, 0.0).astype(BF16)
        xb = jnp.where(valid, xb, 0.0).astype(BF16)

        def proj(w_s, b_ref):
            return (jnp.dot(xa, w_s[:half, :], preferred_element_type=F32)
                    + jnp.dot(xb, w_s[half:, :], preferred_element_type=F32) + b_ref[0, 0])

        a = jnp.minimum(proj(wg_s, bg_ref), SWIGLU_LIMIT)
        u = jnp.clip(proj(wu_s, bu_ref), -SWIGLU_LIMIT, SWIGLU_LIMIT)
        hid = a * (1.0 / (1.0 + jnp.exp(-SWIGLU_ALPHA * a))) * (u + 1.0)
        out = jnp.dot(hid.astype(BF16), wd_s[...], preferred_element_type=F32) + bd_ref[0, 0]
        d = out.shape[1] // 2
        o_ref[...] = _pack_halves(out[:, :d], out[:, d:])

    @pl.when(i >= nu_ref[0])
    def _():
        o_ref[...] = jnp.zeros(o_ref.shape, o_ref.dtype)


def expert_ffn(xs, blk_expert, blk_valid, n_used, layer, wg, bg, wu, bu, wd, bd):
    n_rows, w = xs.shape
    _, n_e, d, f = wg.shape
    nb = n_rows // BLK_FFN
    row_map = lambda i, be, bv, nu: (jnp.minimum(i, nu[0] - 1), 0)
    exp_map = lambda i, be, bv, nu: (layer, be[i], 0, 0)
    grid_spec = pltpu.PrefetchScalarGridSpec(
        num_scalar_prefetch=3,
        grid=(nb,),
        in_specs=[pl.BlockSpec((BLK_FFN, w), row_map),
                  pl.BlockSpec((1, 1, d, f), exp_map), pl.BlockSpec((1, 1, 1, f), exp_map),
                  pl.BlockSpec((1, 1, d, f), exp_map), pl.BlockSpec((1, 1, 1, f), exp_map),
                  pl.BlockSpec((1, 1, f, d), exp_map), pl.BlockSpec((1, 1, 1, d), exp_map)],
        out_specs=pl.BlockSpec((BLK_FFN, d // 2), lambda i, be, bv, nu: (i, 0)),
        scratch_shapes=[pltpu.VMEM((d, f), BF16), pltpu.VMEM((d, f), BF16), pltpu.VMEM((f, d), BF16)],
    )
    return pl.pallas_call(
        _ffn_body,
        grid_spec=grid_spec,
        out_shape=jax.ShapeDtypeStruct((n_rows, d // 2), U32),
        compiler_params=_cparams(("arbitrary",)),
        name="expert_ffn",
    )(blk_expert, blk_valid, n_used, xs, wg, bg, wu, bu, wd, bd)


def _combine_body(dest_ref, x_ref, gt_ref, g2_ref, nf_ref, ys_ref, o_ref, buf, sem, *, tm, final):
    def issue(tok, carry):
        for k in range(TOP_K):
            pltpu.make_async_copy(ys_ref.at[pl.ds(dest_ref[k, tok], 1)], buf.at[k, pl.ds(tok, 1)], sem).start()
        return carry

    lax.fori_loop(0, tm, issue, 0)
    for k in range(TOP_K):
        pltpu.make_async_copy(ys_ref.at[pl.ds(0, tm)], buf.at[k], sem).wait()
    gt = gt_ref[...]
    lo = hi = None
    for k in range(TOP_K):
        a, b = _unpack_halves(buf[k])
        g = gt[:, k:k + 1]
        lo = g * a if lo is None else lo + g * a
        hi = g * b if hi is None else hi + g * b
    half = lo.shape[1]
    g2 = g2_ref[0]
    x_lo = x_ref[:, :half] + g2[:, :half] * lo
    x_hi = x_ref[:, half:] + g2[:, half:] * hi
    if final:
        ms = (jnp.sum(x_lo * x_lo, axis=-1, keepdims=True)
              + jnp.sum(x_hi * x_hi, axis=-1, keepdims=True)) / (2 * half)
        r = lax.rsqrt(ms + RMS_EPS)
        x_lo = x_lo * r * nf_ref[:, :half]
        x_hi = x_hi * r * nf_ref[:, half:]
    o_ref[:, :half] = x_lo
    o_ref[:, half:] = x_hi


def combine(x1, ys, dest, gates_t, gate2, norm_f, seq, final):
    t, d = x1.shape
    tm = min(TM_COMBINE, seq)
    return pl.pallas_call(
        functools.partial(_combine_body, tm=tm, final=final),
        grid=(t // tm,),
        in_specs=[pl.BlockSpec((TOP_K, tm), lambda i: (0, i), memory_space=pltpu.SMEM),
                  pl.BlockSpec((tm, d), lambda i: (i, 0)),
                  pl.BlockSpec((tm, TOP_K), lambda i: (i, 0)),
                  pl.BlockSpec((1, 1, d), lambda i: ((i * tm) // seq, 0, 0)),
                  pl.BlockSpec((1, d), lambda i: (0, 0)),
                  pl.BlockSpec(memory_space=pl.ANY)],
        out_specs=pl.BlockSpec((tm, d), lambda i: (i, 0)),
        out_shape=jax.ShapeDtypeStruct((t, d), F32),
        scratch_shapes=[pltpu.VMEM((TOP_K, tm, d // 2), U32), pltpu.SemaphoreType.DMA],
        compiler_params=_cparams(("arbitrary",)),
        name="combine",
    )(dest, x1, gates_t, gate2, norm_f, ys)


def _alibi_slopes(n):
    return jnp.exp2(-8.0 * jnp.arange(1, n + 1, dtype=F32) / n)


def _routing_tables(counts, eidx, rank, nb):
    counts = counts.reshape(-1)
    padded = (counts + BLK_FFN - 1) // BLK_FFN * BLK_FFN
    pad_end = jnp.cumsum(padded)
    pad_start = pad_end - padded
    e_ids = jnp.arange(N_EXPERTS, dtype=I32)[:, None, None]
    dest = rank + jnp.sum(jnp.where(eidx[None] == e_ids, pad_start[:, None, None], 0), axis=0)
    blk_row0 = jnp.arange(nb, dtype=I32) * BLK_FFN
    blk_expert = jnp.minimum(jnp.sum(pad_end[None, :] <= blk_row0[:, None], axis=1), N_EXPERTS - 1).astype(I32)
    blk_valid = jnp.clip(counts[blk_expert] - (blk_row0 - pad_start[blk_expert]), 0, BLK_FFN).astype(I32)
    n_used = (pad_end[-1:] // BLK_FFN).astype(I32)
    return dest.astype(I32), blk_expert, blk_valid, n_used


def kernel(x, c, w_ada, b_ada, norm1, w_in, sinks, lambda_q1, lambda_k1, lambda_q2, lambda_k2, subln,
           w_out, norm2, w_router, b_router, w_gate, b_gate, w_up, b_up, w_down, b_down, norm_f):
    bsz, seq, d = x.shape
    depth = w_ada.shape[0]
    t = bsz * seq
    wa_w, wb_w = N_HEADS_A * HEAD_DIM, N_HEADS_B * HEAD_DIM
    kvb_w = N_KV_B * HEAD_DIM
    wc_w = N_HEADS_C * 2 * HEAD_DIM
    off = [0]
    for wdt in (wa_w, wa_w, wa_w, wb_w, kvb_w, kvb_w, wc_w, wc_w, wc_w):
        off.append(off[-1] + wdt)
    blk = [o // LANES for o in off]

    group = N_HEADS_B // N_KV_B
    b_perm = [kv * group + g for g in range(group) for kv in range(N_KV_B)]
    col_perm = jnp.concatenate([jnp.arange(h * HEAD_DIM, (h + 1) * HEAD_DIM) for h in b_perm])
    q_scale = jnp.ones((off[-1],), F32)
    for s0, s1, extra in ((off[0], off[1], 1.0), (off[3], off[4], 1.0), (off[6], off[7], LOG2E)):
        q_scale = q_scale.at[s0:s1].set(HEAD_DIM ** -0.5 * extra)

    slopes_a = _alibi_slopes(N_HEADS_A)
    slopes_b = _alibi_slopes(N_HEADS_B)[jnp.array(b_perm)]
    slopes_c = _alibi_slopes(N_HEADS_C) * LOG2E

    mod_all = ada_modulation(c, w_ada, b_ada)
    x2d = x.reshape(t, d)
    nb = (t * TOP_K) // BLK_FFN + N_EXPERTS
    n_rows = nb * BLK_FFN

    for layer in range(depth):
        mod = mod_all[layer].reshape(bsz, N_MOD, 1, d)
        sh1, sc1, g1, sh2, sc2, g2 = [mod[:, i] for i in range(N_MOD)]

        w_l = w_in[layer] * q_scale[None, :]
        w_l = jnp.concatenate([w_l[:, :off[3]], w_l[:, off[3]:off[4]][:, col_perm], w_l[:, off[4]:]], axis=1)
        proj = in_projection(x2d, sc1, sh1, norm1[layer].reshape(1, d), w_l.astype(BF16), seq)
        proj3d = proj.reshape(bsz, seq, off[-1])

        n_blk_a = wa_w // LANES
        o_a = window_attention(proj3d, blk[0], blk[1], blk[2], 1, n_blk_a, slopes_a,
                               jnp.full((n_blk_a, 2), -jnp.inf, F32), DILATED_PAIRS).reshape(t, wa_w)

        sink_l = sinks[layer].astype(F32)[jnp.array(b_perm)].reshape(group, N_KV_B)
        o_b = window_attention(proj3d, blk[3], blk[4], blk[5], 0, group, slopes_b, sink_l,
                               ((2 * SWA_HALF, 1),)).reshape(t, wb_w)

        lam_init = jnp.full((1,), 0.8 - 0.6 * math.exp(-0.3 * layer), F32)
        heads_c = lambda i: proj3d[:, :, off[i]:off[i + 1]].reshape(bsz, seq, N_HEADS_C, FLASH_DV)
        qt, k_aug, vt_ones = flash_operands(heads_c(6), heads_c(7), heads_c(8), slopes_c)
        norms = tile_norms(proj3d[:, :, off[6]:off[7]], proj3d[:, :, off[7]:off[8]], min(T_FLASH, seq))
        o_c = diff_attention(qt, k_aug, vt_ones, norms, slopes_c, lam_init,
                             lambda_q1[layer].reshape(1, -1), lambda_k1[layer].reshape(1, -1),
                             lambda_q2[layer].reshape(1, -1), lambda_k2[layer].reshape(1, -1),
                             subln[layer].reshape(-1, 1)).reshape(t, wc_w)

        wo = w_out[layer]
        wo_b = wo[wa_w:wa_w + wb_w].reshape(N_HEADS_B, HEAD_DIM, d)[jnp.array(b_perm)].reshape(wb_w, d)
        x1, h_packed, logits_t = out_projection(
            x2d, o_a, o_b, o_c, wo[:wa_w].astype(BF16), wo_b.astype(BF16), wo[wa_w + wb_w:].astype(BF16),
            g1, sc2, sh2, norm2[layer].reshape(1, d), w_router[layer].T, b_router[layer].reshape(-1, 1), seq)

        eidx, gates, rank, counts = route(logits_t)
        dest, blk_expert, blk_valid, n_used = _routing_tables(counts, eidx, rank, nb)
        xs = dispatch_rows(h_packed, dest, n_rows)
        ys = expert_ffn(xs, blk_expert, blk_valid, n_used, layer,
                        w_gate, b_gate[:, :, None, :], w_up, b_up[:, :, None, :], w_down, b_down[:, :, None, :])
        x2d = combine(x1, ys, dest, gates.T, g2, norm_f.reshape(1, d), seq, layer == depth - 1)

    return x2d.reshape(bsz, seq, d)
```

```python
import functools
import math

import jax
import jax.numpy as jnp
from jax import lax
from jax.experimental import pallas as pl
from jax.experimental.pallas import tpu as pltpu

F32 = jnp.float32
BF16 = jnp.bfloat16
I32 = jnp.int32
U32 = jnp.uint32

HEAD_DIM = 64
DILATED_PAIRS = ((128, 1), (512, 4), (2048, 16))
N_HEADS_A = 4
N_HEADS_B = 4
N_KV_B = 2
SWA_HALF = 128
N_HEADS_C = 4
N_EXPERTS = 32
TOP_K = 4
SWIGLU_LIMIT = 7.0
SWIGLU_ALPHA = 1.702
RMS_EPS = 1e-6
N_MOD = 6

LANES = 128
VMEM_LIMIT_BYTES = 56 * 1024 * 1024

TM_PROJ = 512
TQ_BAND = 256
T_FLASH = 1024
FLASH_PANEL = 512
FLASH_CHUNK = 256
FLASH_SUB = 128
FLASH_DV = 2 * HEAD_DIM
FLASH_ONES = 16
FLASH_AUG = 3
FLASH_LOOKAHEAD = 2
FLASH_SAFE_LOG2 = 60.0
LOG2E = math.log2(math.e)
TM_ROUTE = 512
TM_DISPATCH = 512
BLK_FFN = 512
TM_COMBINE = 256
DMA_ISSUE_UNROLL = 8

HIGHEST = lax.Precision.HIGHEST
_NT = (((1,), (1,)), ((), ()))


def _cparams(sem):
    return pltpu.CompilerParams(dimension_semantics=sem, vmem_limit_bytes=VMEM_LIMIT_BYTES)


def _pack_halves(a, b):
    ua = lax.bitcast_convert_type(a.astype(BF16).astype(F32), U32)
    ub = lax.bitcast_convert_type(b.astype(BF16).astype(F32), U32)
    return ua | (ub >> 16)


def _unpack_halves(w):
    a = lax.bitcast_convert_type(w & jnp.uint32(0xFFFF0000), F32)
    b = lax.bitcast_convert_type(w << 16, F32)
    return a, b


def _rms_scale(x):
    return lax.rsqrt(jnp.mean(x * x, axis=-1, keepdims=True) + RMS_EPS)


def _ada_body(c_ref, w_ref, b_ref, o_ref):
    c = c_ref[...]
    cond = c / (1.0 + jnp.exp(-c))
    o_ref[0] = jnp.dot(cond, w_ref[0], preferred_element_type=F32, precision=HIGHEST) + b_ref[0]


def ada_modulation(c, w_ada, b_ada):
    depth, d, n = w_ada.shape
    bsz = c.shape[0]
    tn = 1536
    assert n % tn == 0
    return pl.pallas_call(
        _ada_body,
        grid=(depth, n // tn),
        in_specs=[
            pl.BlockSpec((bsz, d), lambda l, j: (0, 0)),
            pl.BlockSpec((1, d, tn), lambda l, j: (l, 0, j)),
            pl.BlockSpec((1, 1, tn), lambda l, j: (l, 0, j)),
        ],
        out_specs=pl.BlockSpec((1, bsz, tn), lambda l, j: (l, 0, j)),
        out_shape=jax.ShapeDtypeStruct((depth, bsz, n), F32),
        compiler_params=_cparams(("parallel", "parallel")),
        name="ada_modulation",
    )(c, w_ada, b_ada.reshape(depth, 1, n))


def _inproj_body(x_ref, sc_ref, sh_ref, g_ref, w_ref, o_ref):
    x = x_ref[...]
    h = x * _rms_scale(x) * g_ref[...]
    h = h * (1.0 + sc_ref[0]) + sh_ref[0]
    o_ref[...] = jnp.dot(h.astype(BF16), w_ref[...], preferred_element_type=F32).astype(o_ref.dtype)


def in_projection(x2d, scale, shift, gain, w_bf16, seq):
    t, d = x2d.shape
    n = w_bf16.shape[1]
    tm = min(TM_PROJ, seq)
    assert seq % tm == 0 and t % tm == 0
    mod_spec = pl.BlockSpec((1, 1, d), lambda i: ((i * tm) // seq, 0, 0))
    return pl.pallas_call(
        _inproj_body,
        grid=(t // tm,),
        in_specs=[
            pl.BlockSpec((tm, d), lambda i: (i, 0)),
            mod_spec, mod_spec,
            pl.BlockSpec((1, d), lambda i: (0, 0)),
            pl.BlockSpec((d, n), lambda i: (0, 0)),
        ],
        out_specs=pl.BlockSpec((tm, n), lambda i: (i, 0)),
        out_shape=jax.ShapeDtypeStruct((t, n), BF16),
        compiler_params=_cparams(("parallel",)),
        name="in_projection",
    )(x2d, scale, shift, gain, w_bf16)


def _offset_bias_body(c_ref, slope_ref, o_ref, *, pairs):
    vi = pl.program_id(0)
    hd = pl.program_id(1)
    tq, win = o_ref.shape[2], o_ref.shape[3]
    off = (lax.broadcasted_iota(I32, (tq, win), 1) - lax.broadcasted_iota(I32, (tq, win), 0)) + c_ref[vi]
    dist = jnp.abs(off)
    mult = jnp.zeros((tq, win), F32)
    for window, dil in pairs:
        hit = jnp.logical_and(dist <= window // 2, (dist & (dil - 1)) == 0)
        mult = mult + jnp.where(hit, 1.0, 0.0)
    o_ref[0, 0] = jnp.log(mult) - slope_ref[hd] * dist.astype(F32)


def _window_variants(seq, tq, reach):
    win = tq + 2 * reach
    nq, edge = seq // tq, -(-reach // tq)
    assert seq % tq == 0 and reach % 16 == 0 and win <= seq and nq >= 2 * edge + 1
    rel = [min(max(i * tq - reach, 0), seq - win) - i * tq for i in range(nq)]
    assert all(r == -reach for r in rel[edge:nq - edge])
    starts = rel[:edge] + [-reach] + rel[nq - edge:]

    def variant(i):
        return jnp.where(i < edge, i, jnp.where(i >= nq - edge, i - (nq - 2 * edge - 1), edge))

    return win, starts, variant


def _window_body(c_ref, sink_ref, q_ref, k_ref, v_ref, tab_ref, o_ref, *, tq, win, variant):
    j = pl.program_id(1)
    i = pl.program_id(2)
    ks = pl.multiple_of(i * tq + c_ref[variant(i)], 16)
    q = q_ref[0]
    k = k_ref[0, pl.ds(ks, win), :]
    v = v_ref[0, pl.ds(ks, win), :]
    lane_lo = lax.broadcasted_iota(I32, (1, LANES), 1) < HEAD_DIM
    outs = []
    for hf in range(2):
        qm = jnp.where(lane_lo if hf == 0 else jnp.logical_not(lane_lo), q, jnp.zeros_like(q))
        s = lax.dot_general(qm, k, _NT, preferred_element_type=F32) + tab_ref[0, hf]
        sink = sink_ref[j, hf]
        m = jnp.maximum(jnp.max(s, axis=-1, keepdims=True), sink)
        p = jnp.exp(s - m)
        l = jnp.sum(p, axis=-1, keepdims=True) + jnp.exp(sink - m)
        outs.append(jnp.dot(p.astype(BF16), v, preferred_element_type=F32) / l)
    o_ref[0] = jnp.where(lane_lo, outs[0], outs[1]).astype(o_ref.dtype)


def window_attention(proj3d, q_blk0, k_blk0, v_blk0, kv_step, n_blk, slopes, sinks, pairs):
    bsz, seq, _ = proj3d.shape
    assert all(d & (d - 1) == 0 for _, d in pairs)
    reach = max(w // 2 for w, _ in pairs)
    tq = TQ_BAND
    win, starts, variant = _window_variants(seq, tq, reach)
    starts = jnp.array(starts, I32)
    smem = pl.BlockSpec(memory_space=pltpu.SMEM)
    tables = pl.pallas_call(
        functools.partial(_offset_bias_body, pairs=pairs),
        grid=(starts.shape[0], 2 * n_blk),
        in_specs=[smem, smem],
        out_specs=pl.BlockSpec((1, 1, tq, win), lambda vi, hd: (vi, hd, 0, 0)),
        out_shape=jax.ShapeDtypeStruct((starts.shape[0], 2 * n_blk, tq, win), F32),
        compiler_params=_cparams(("parallel", "parallel")),
        name="offset_bias_tables",
    )(starts, slopes)
    return pl.pallas_call(
        functools.partial(_window_body, tq=tq, win=win, variant=variant),
        grid=(bsz, n_blk, seq // tq),
        in_specs=[
            smem, smem,
            pl.BlockSpec((1, tq, LANES), lambda b, j, i: (b, i, q_blk0 + j)),
            pl.BlockSpec((1, seq, LANES), lambda b, j, i: (b, 0, k_blk0 + kv_step * j)),
            pl.BlockSpec((1, seq, LANES), lambda b, j, i: (b, 0, v_blk0 + kv_step * j)),
            pl.BlockSpec((1, 2, tq, win), lambda b, j, i: (variant(i), j, 0, 0)),
        ],
        out_specs=pl.BlockSpec((1, tq, LANES), lambda b, j, i: (b, i, j)),
        out_shape=jax.ShapeDtypeStruct((bsz, seq, n_blk * LANES), BF16),
        compiler_params=_cparams(("parallel", "parallel", "parallel")),
        name="window_attention",
    )(starts, sinks, proj3d, proj3d, proj3d, tables)


def _flash_body(slope_ref, lami_ref, qn_ref, kn_ref, qt_ref, k_ref, kaug_ref, vt_ref, diag_ref, lq1_ref, lk1_ref,
                lq2_ref, lk2_ref, sub_ref, o_ref, m_s, acc_s, s_scr, mmin_s, *, t):
    b = pl.program_id(0)
    h = pl.program_id(1)
    qi = pl.program_id(2)
    ki = pl.program_id(3)
    nt = pl.num_programs(3)
    kt = lax.rem(qi + ki, nt)

    @pl.when(ki == 0)
    def _():
        m_s[...] = jnp.full(m_s.shape, -jnp.inf, F32)
        acc_s[...] = jnp.zeros(acc_s.shape, F32)
        mmin_s[0] = jnp.float32(0.0)

    slope = slope_ref[h]
    n_chunk = t // FLASH_CHUNK
    n_panel = t // FLASH_PANEL
    sub_per_panel = FLASH_PANEL // FLASH_CHUNK
    row = lax.broadcasted_iota(I32, (FLASH_DV, 1), 0)
    row_lo = row < HEAD_DIM
    row_aug = lax.broadcasted_iota(I32, (FLASH_ONES, 1), 0) < FLASH_AUG
    lane_pos = lax.broadcasted_iota(I32, (1, FLASH_CHUNK), 1).astype(F32)
    ones_rows = jnp.ones((FLASH_ONES, FLASH_CHUNK), BF16)

    def with_sign(qtm, sign):
        aug = jnp.where(row_aug, sign, 0.0).astype(BF16)
        pad = jnp.zeros((FLASH_DV - FLASH_ONES, sign.shape[1]), BF16)
        return jnp.concatenate([qtm, aug, pad], axis=0)

    def keys(kc):
        rows = slice(kc * FLASH_CHUNK, (kc + 1) * FLASH_CHUNK)
        return jnp.concatenate([k_ref[0, rows, :], kaug_ref[0, rows, :]], axis=1)

    def values(kc):
        return jnp.concatenate([vt_ref[0, 0, :, kc * FLASH_CHUNK:(kc + 1) * FLASH_CHUNK], ones_rows], axis=0)
    blocks = [(qp, mp) for qp in range(n_panel) for mp in range(2)]

    def fold8(x, op):
        return functools.reduce(op, [x[r:r + 8] for r in range(0, x.shape[0], 8)])

    def pair_terms(qp, kc):
        signs, shifts = [], []
        for j in range(sub_per_panel):
            gap = (qi * n_chunk + qp * sub_per_panel + j) - (kt * n_chunk + kc)
            sign = jnp.sign(gap).astype(F32)
            signs.append(jnp.full((1, FLASH_CHUNK), sign, F32))
            shifts.append(-slope * ((jnp.abs(gap) * FLASH_CHUNK).astype(F32) + sign * lane_pos))
        return jnp.concatenate(signs, axis=1), jnp.concatenate(shifts, axis=1)

    def masked_queries(bi):
        qp, mp = blocks[bi]
        qt = qt_ref[0, 0, :, qp * FLASH_PANEL:(qp + 1) * FLASH_PANEL]
        return jnp.where(row_lo if mp == 0 else jnp.logical_not(row_lo), qt, jnp.zeros_like(qt))

    def score_chunk(bi, kc, qtm, mx):
        qp, _ = blocks[bi]
        k0 = kc * FLASH_CHUNK
        sign, lane_shift = pair_terms(qp, kc)
        s = jnp.dot(keys(kc), with_sign(qtm, sign), preferred_element_type=F32)
        j = kc - qp * sub_per_panel
        if 0 <= j < sub_per_panel:
            cols = [s[:, i * FLASH_CHUNK:(i + 1) * FLASH_CHUNK] for i in range(sub_per_panel)]
            cols[j] = cols[j] + diag_ref[0, 0]
            s = jnp.concatenate(cols, axis=1)
        s_scr[bi % 2, k0:k0 + FLASH_CHUNK, :] = s
        cm = fold8(s, jnp.maximum) + lane_shift
        return cm if mx is None else jnp.maximum(mx, cm)

    def prob_chunk(bi, kc, m_new, acc):
        qp, _ = blocks[bi]
        k0 = kc * FLASH_CHUNK
        shift = m_new - pair_terms(qp, kc)[1]
        parts = []
        for r0 in range(k0, k0 + FLASH_CHUNK, FLASH_SUB):
            parts.append(jnp.exp2((s_scr[bi % 2, r0:r0 + FLASH_SUB, :] - shift).astype(BF16)))
        pv = jnp.dot(values(kc), jnp.concatenate(parts, axis=0), preferred_element_type=F32)
        return pv if acc is None else acc + pv

    def two_pass_tile():
        qtm_next = masked_queries(0)
        mx_next = None
        for kc in range(n_chunk):
            mx_next = score_chunk(0, kc, qtm_next, mx_next)
        for bi, (qp, mp) in enumerate(blocks):
            qs = slice(qp * FLASH_PANEL, (qp + 1) * FLASH_PANEL)
            m_old = m_s[mp, :, qs]
            m_new = jnp.maximum(m_old, jnp.max(mx_next, axis=0, keepdims=True))
            alpha = jnp.exp2(m_old - m_new)
            acc = None
            has_next = bi + 1 < len(blocks)
            if has_next:
                qtm_next = masked_queries(bi + 1)
            mx_next = None
            for kc in range(n_chunk):
                if has_next:
                    mx_next = score_chunk(bi + 1, kc, qtm_next, mx_next)
                acc = prob_chunk(bi, kc, m_new, acc)
            acc_s[mp, :, qs] = alpha * acc_s[mp, :, qs] + acc
            m_s[mp, :, qs] = m_new
        mmin_s[0] = jnp.min(m_s[...])

    def one_pass_tile():
        items = [(bi, kc) for bi in range(len(blocks)) for kc in range(n_chunk)]

        def scores(bi, kc):
            qp, mp = blocks[bi]
            sign, lane_shift = pair_terms(qp, kc)
            s = jnp.dot(keys(kc), with_sign(masked_queries(bi), sign), preferred_element_type=F32)
            return s, m_s[mp, :, qp * FLASH_PANEL:(qp + 1) * FLASH_PANEL] - lane_shift

        ahead = [scores(*items[i]) for i in range(FLASH_LOOKAHEAD)]
        acc = None
        for n, (bi, kc) in enumerate(items):
            s, shift = ahead.pop(0)
            if n + FLASH_LOOKAHEAD < len(items):
                ahead.append(scores(*items[n + FLASH_LOOKAHEAD]))
            p = jnp.exp2(s - shift).astype(BF16)
            pv = jnp.dot(values(kc), p, preferred_element_type=F32)
            acc = pv if kc == 0 else acc + pv
            if kc == n_chunk - 1:
                qp, mp = blocks[bi]
                qs = slice(qp * FLASH_PANEL, (qp + 1) * FLASH_PANEL)
                acc_s[mp, :, qs] = acc_s[mp, :, qs] + acc

    bound = qn_ref[(b * pl.num_programs(1) + h) * nt + qi] * kn_ref[(b * pl.num_programs(1) + h) * nt + kt]
    safe = jnp.logical_and(ki > 0, bound - mmin_s[0] <= FLASH_SAFE_LOG2)
    pl.when(safe)(one_pass_tile)
    pl.when(jnp.logical_not(safe))(two_pass_tile)

    @pl.when(ki == pl.num_programs(3) - 1)
    def _():
        lam_init = lami_ref[0]
        lam = (jnp.exp(jnp.sum(lq1_ref[...] * lk1_ref[...], axis=-1, keepdims=True))
               - jnp.exp(jnp.sum(lq2_ref[...] * lk2_ref[...], axis=-1, keepdims=True)) + lam_init)
        o1 = acc_s[0, :FLASH_DV, :] / acc_s[0, FLASH_DV:FLASH_DV + 1, :]
        o2 = acc_s[1, :FLASH_DV, :] / acc_s[1, FLASH_DV:FLASH_DV + 1, :]
        o = o1 - lam * o2
        r = lax.rsqrt(jnp.mean(o * o, axis=0, keepdims=True) + RMS_EPS)
        y = o * r * sub_ref[...] * (1.0 - lam_init)
        o_ref[0] = y.T.astype(o_ref.dtype)


def _split_bf16(x, n):
    parts = []
    for _ in range(n):
        p = x.astype(BF16)
        parts.append(p)
        x = x - p.astype(F32)
    return parts


def alibi_key_columns(slopes, t):
    pos = (jnp.arange(t) % FLASH_CHUNK).astype(F32)
    aug = jnp.stack(_split_bf16(slopes[:, None] * pos[None, :], FLASH_AUG), axis=-1)
    return jnp.concatenate([aug, jnp.zeros((slopes.shape[0], t, FLASH_DV - FLASH_AUG), BF16)], axis=-1)


def _tile_norms_body(q_ref, k_ref, o_ref, *, nh):
    lane = lax.broadcasted_iota(I32, (1, 2 * nh), 1)
    out = jnp.zeros((1, 2 * nh), F32)
    for j, ref in enumerate((q_ref, k_ref)):
        for hh in range(nh):
            x = ref[0, :, hh * FLASH_DV:(hh + 1) * FLASH_DV].astype(F32)
            n2 = jnp.max(jnp.sum(x * x, axis=1, keepdims=True), axis=0, keepdims=True)
            out = jnp.where(lane == j * nh + hh, jnp.sqrt(n2), out)
    o_ref[0, 0] = out


def tile_norms(q3, k3, t):
    bsz, seq, w = q3.shape
    nh = w // FLASH_DV
    spec = pl.BlockSpec((1, t, w), lambda b, i: (b, i, 0))
    return pl.pallas_call(
        functools.partial(_tile_norms_body, nh=nh),
        grid=(bsz, seq // t),
        in_specs=[spec, spec],
        out_specs=pl.BlockSpec((1, 1, 1, 2 * nh), lambda b, i: (b, i, 0, 0)),
        out_shape=jax.ShapeDtypeStruct((bsz, seq // t, 1, 2 * nh), F32),
        compiler_params=_cparams(("parallel", "parallel")),
        name="tile_norms",
    )(q3, k3)


def diff_attention(qt, proj3d, k_blk0, vt, norms, slopes, lam_init, lq1, lk1, lq2, lk2, subln):
    bsz, nh, _, seq = qt.shape
    t = min(T_FLASH, seq)
    assert seq % t == 0 and t % FLASH_PANEL == 0 and FLASH_PANEL % FLASH_CHUNK == 0
    nt = seq // t
    norms = norms.reshape(bsz, nt, 2, nh).transpose(2, 0, 3, 1).reshape(2, bsz * nh * nt)
    k_cols = alibi_key_columns(slopes, t)
    dv = FLASH_DV
    dva = FLASH_DV + FLASH_ONES
    r = jnp.arange(FLASH_CHUNK, dtype=F32)
    diag = -slopes[:, None, None] * jnp.abs(r[None, :] - r[:, None])[None]
    diag = jnp.stack([jnp.zeros_like(diag), diag], axis=1)
    smem = pl.BlockSpec(memory_space=pltpu.SMEM)
    vec = lambda n: pl.BlockSpec((1, n), lambda b, h, qi, ki: (0, 0))
    return pl.pallas_call(
        functools.partial(_flash_body, t=t),
        grid=(bsz, nh, nt, nt),
        in_specs=[
            smem, smem, smem, smem,
            pl.BlockSpec((1, 1, dv, t), lambda b, h, qi, ki: (b, h, 0, qi)),
            pl.BlockSpec((1, t, dv), lambda b, h, qi, ki: (b, (qi + ki) % nt, k_blk0 + h)),
            pl.BlockSpec((1, t, dv), lambda b, h, qi, ki: (h, 0, 0)),
            pl.BlockSpec((1, 1, dv, t), lambda b, h, qi, ki: (b, h, 0, (qi + ki) % nt)),
            pl.BlockSpec((1, 1, FLASH_CHUNK, FLASH_CHUNK),
                         lambda b, h, qi, ki: (h, (ki == 0).astype(I32), 0, 0)),
            vec(HEAD_DIM), vec(HEAD_DIM), vec(HEAD_DIM), vec(HEAD_DIM),
            pl.BlockSpec((dv, 1), lambda b, h, qi, ki: (0, 0)),
        ],
        out_specs=pl.BlockSpec((1, t, LANES), lambda b, h, qi, ki: (b, qi, h)),
        out_shape=jax.ShapeDtypeStruct((bsz, seq, nh * dv), BF16),
        scratch_shapes=[
            pltpu.VMEM((2, 1, t), F32),
            pltpu.VMEM((2, dva, t), F32),
            pltpu.VMEM((2, t, FLASH_PANEL), F32),
            pltpu.SMEM((1,), F32),
        ],
        compiler_params=_cparams(("parallel", "parallel", "parallel", "arbitrary")),
        name="diff_attention",
    )(slopes, lam_init, norms[0], norms[1], qt, proj3d, k_cols, vt, diag, lq1, lk1, lq2, lk2, subln)


def _outproj_body(x_ref, oa_ref, ob_ref, oc_ref, wa_ref, wb_ref, wc_ref, g1_ref, sc_ref, sh_ref,
                  n2_ref, wr_ref, br_ref, x1_ref, hp_ref, lg_ref):
    mix = (jnp.dot(oa_ref[...], wa_ref[...], preferred_element_type=F32)
           + jnp.dot(ob_ref[...], wb_ref[...], preferred_element_type=F32)
           + jnp.dot(oc_ref[...], wc_ref[...], preferred_element_type=F32))
    x1 = x_ref[...] + g1_ref[0] * mix
    x1_ref[...] = x1
    h = x1 * _rms_scale(x1) * n2_ref[...]
    h = h * (1.0 + sc_ref[0]) + sh_ref[0]
    half = h.shape[1] // 2
    hp_ref[...] = _pack_halves(h[:, :half], h[:, half:])
    lg_ref[...] = lax.dot_general(wr_ref[...], h, _NT, preferred_element_type=F32,
                                  precision=HIGHEST) + br_ref[...]


def out_projection(x2d, oa, ob, oc, wa, wb, wc, gate1, scale2, shift2, gain2, w_router_t, b_router, seq):
    t, d = x2d.shape
    e = w_router_t.shape[0]
    tm = min(TM_PROJ, seq)
    mod_spec = pl.BlockSpec((1, 1, d), lambda i: ((i * tm) // seq, 0, 0))
    row = lambda w: pl.BlockSpec((tm, w), lambda i: (i, 0))
    full = lambda a: pl.BlockSpec(a.shape, lambda i: (0, 0))
    return pl.pallas_call(
        _outproj_body,
        grid=(t // tm,),
        in_specs=[row(d), row(oa.shape[1]), row(ob.shape[1]), row(oc.shape[1]),
                  full(wa), full(wb), full(wc), mod_spec, mod_spec, mod_spec,
                  pl.BlockSpec((1, d), lambda i: (0, 0)), full(w_router_t),
                  pl.BlockSpec((e, 1), lambda i: (0, 0))],
        out_specs=[row(d), row(d // 2), pl.BlockSpec((e, tm), lambda i: (0, i))],
        out_shape=[jax.ShapeDtypeStruct((t, d), F32),
                   jax.ShapeDtypeStruct((t, d // 2), U32),
                   jax.ShapeDtypeStruct((e, t), F32)],
        compiler_params=_cparams(("parallel",)),
        name="out_projection",
    )(x2d, oa, ob, oc, wa, wb, wc, gate1, scale2, shift2, gain2, w_router_t, b_router)


def _router_body(lg_ref, tri_ref, eidx_ref, gate_ref, rank_ref, cnt_ref, base_s):
    step = pl.program_id(0)

    @pl.when(step == 0)
    def _():
        base_s[...] = jnp.zeros(base_s.shape, F32)

    work = lg_ref[...]
    n_e, tm = work.shape
    eio = lax.broadcasted_iota(I32, (n_e, tm), 0).astype(F32)
    vals, hots = [], []
    for k in range(TOP_K):
        mx = jnp.max(work, axis=0, keepdims=True)
        idx = jnp.min(jnp.where(work == mx, eio, float(n_e)), axis=0, keepdims=True)
        hot = eio == idx
        vals.append(mx)
        hots.append(hot)
        eidx_ref[k:k + 1, :] = idx.astype(I32)
        work = jnp.where(hot, -jnp.inf, work)
    exps = [jnp.exp(v - vals[0]) for v in vals]
    den = functools.reduce(lambda a, b: a + b, exps)
    for k in range(TOP_K):
        gate_ref[k:k + 1, :] = exps[k] / den
    any_hot = functools.reduce(lambda a, b: a + b, [h.astype(F32) for h in hots])
    prefix = jnp.dot(any_hot.astype(BF16), tri_ref[...], preferred_element_type=F32)
    pos = base_s[...] + prefix
    for k in range(TOP_K):
        rank_ref[k:k + 1, :] = jnp.sum(jnp.where(hots[k], pos, 0.0), axis=0, keepdims=True).astype(I32)
    base_s[...] = base_s[...] + jnp.sum(any_hot, axis=1, keepdims=True)
    cnt_ref[...] = base_s[...].astype(I32)


def route(logits_t):
    n_e, t = logits_t.shape
    tm = min(TM_ROUTE, t)
    assert t % tm == 0
    r = jnp.arange(tm)
    tri = (r[:, None] < r[None, :]).astype(BF16)
    out4 = lambda dt: jax.ShapeDtypeStruct((TOP_K, t), dt)
    spec4 = pl.BlockSpec((TOP_K, tm), lambda i: (0, i))
    return pl.pallas_call(
        _router_body,
        grid=(t // tm,),
        in_specs=[pl.BlockSpec((n_e, tm), lambda i: (0, i)),
                  pl.BlockSpec((tm, tm), lambda i: (0, 0))],
        out_specs=[spec4, spec4, spec4, pl.BlockSpec((n_e, 1), lambda i: (0, 0))],
        out_shape=[out4(I32), out4(F32), out4(I32), jax.ShapeDtypeStruct((n_e, 1), I32)],
        scratch_shapes=[pltpu.VMEM((n_e, 1), F32)],
        compiler_params=_cparams(("arbitrary",)),
        name="route",
    )(logits_t, tri)


def _dispatch_body(dest_ref, h_ref, init_ref, xs_ref, sem, *, tm):
    del init_ref

    def issue(tok, carry):
        for k in range(TOP_K):
            pltpu.make_async_copy(h_ref.at[pl.ds(tok, 1)], xs_ref.at[pl.ds(dest_ref[k, tok], 1)], sem).start()
        return carry

    lax.fori_loop(0, tm, issue, 0, unroll=DMA_ISSUE_UNROLL)
    for k in range(TOP_K):
        pltpu.make_async_copy(h_ref, xs_ref.at[pl.ds(0, tm)], sem).wait()


def dispatch_rows(h_packed, dest, n_rows):
    t, w = h_packed.shape
    tm = min(TM_DISPATCH, t)
    return pl.pallas_call(
        functools.partial(_dispatch_body, tm=tm),
        grid=(t // tm,),
        in_specs=[pl.BlockSpec((TOP_K, tm), lambda i: (0, i), memory_space=pltpu.SMEM),
                  pl.BlockSpec((tm, w), lambda i: (i, 0)),
                  pl.BlockSpec(memory_space=pl.ANY)],
        out_specs=pl.BlockSpec(memory_space=pl.ANY),
        out_shape=jax.ShapeDtypeStruct((n_rows, w), U32),
        input_output_aliases={2: 0},
        scratch_shapes=[pltpu.SemaphoreType.DMA],
        compiler_params=_cparams(("arbitrary",)),
        name="dispatch_rows",
    )(dest, h_packed, jnp.zeros((n_rows, w), U32))


def _ffn_body(be_ref, bv_ref, nu_ref, xs_ref, wg_ref, bg_ref, wu_ref, bu_ref, wd_ref, bd_ref, o_ref,
              wg_s, wu_s, wd_s):
    i = pl.program_id(0)

    @pl.when(i < nu_ref[0])
    def _():
        @pl.when(jnp.logical_or(i == 0, be_ref[i] != be_ref[jnp.maximum(i - 1, 0)]))
        def _():
            wg_s[...] = wg_ref[0, 0].astype(BF16)
            wu_s[...] = wu_ref[0, 0].astype(BF16)
            wd_s[...] = wd_ref[0, 0].astype(BF16)

        xa, xb = _unpack_halves(xs_ref[...])
        blk, half = xa.shape
        valid = lax.broadcasted_iota(I32, (blk, 1), 0) < bv_ref[i]
        xa = jnp.where(valid, xa, 0.0).astype(BF16)
        xb = jnp.where(valid, xb, 0.0).astype(BF16)

        def proj(w_s, b_ref):
            return (jnp.dot(xa, w_s[:half, :], preferred_element_type=F32)
                    + jnp.dot(xb, w_s[half:, :], preferred_element_type=F32) + b_ref[0, 0])

        a = jnp.minimum(proj(wg_s, bg_ref), SWIGLU_LIMIT)
        u = jnp.clip(proj(wu_s, bu_ref), -SWIGLU_LIMIT, SWIGLU_LIMIT)
        hid = a * (1.0 / (1.0 + jnp.exp(-SWIGLU_ALPHA * a))) * (u + 1.0)
        out = jnp.dot(hid.astype(BF16), wd_s[...], preferred_element_type=F32) + bd_ref[0, 0]
        d = out.shape[1] // 2
        o_ref[...] = _pack_halves(out[:, :d], out[:, d:])

    @pl.when(i >= nu_ref[0])
    def _():
        o_ref[...] = jnp.zeros(o_ref.shape, o_ref.dtype)


def expert_ffn(xs, blk_expert, blk_valid, n_used, layer, wg, bg, wu, bu, wd, bd):
    n_rows, w = xs.shape
    _, n_e, d, f = wg.shape
    nb = n_rows // BLK_FFN
    row_map = lambda i, be, bv, nu: (jnp.minimum(i, nu[0] - 1), 0)
    exp_map = lambda i, be, bv, nu: (layer, be[i], 0, 0)
    grid_spec = pltpu.PrefetchScalarGridSpec(
        num_scalar_prefetch=3,
        grid=(nb,),
        in_specs=[pl.BlockSpec((BLK_FFN, w), row_map),
                  pl.BlockSpec((1, 1, d, f), exp_map), pl.BlockSpec((1, 1, 1, f), exp_map),
                  pl.BlockSpec((1, 1, d, f), exp_map), pl.BlockSpec((1, 1, 1, f), exp_map),
                  pl.BlockSpec((1, 1, f, d), exp_map), pl.BlockSpec((1, 1, 1, d), exp_map)],
        out_specs=pl.BlockSpec((BLK_FFN, d // 2), lambda i, be, bv, nu: (i, 0)),
        scratch_shapes=[pltpu.VMEM((d, f), BF16), pltpu.VMEM((d, f), BF16), pltpu.VMEM((f, d), BF16)],
    )
    return pl.pallas_call(
        _ffn_body,
        grid_spec=grid_spec,
        out_shape=jax.ShapeDtypeStruct((n_rows, d // 2), U32),
        compiler_params=_cparams(("arbitrary",)),
        name="expert_ffn",
    )(blk_expert, blk_valid, n_used, xs, wg, bg, wu, bu, wd, bd)


def _combine_body(dest_ref, x_ref, gt_ref, g2_ref, nf_ref, ys_ref, o_ref, buf, sem, *, tm, final):
    def issue(tok, carry):
        for k in range(TOP_K):
            pltpu.make_async_copy(ys_ref.at[pl.ds(dest_ref[k, tok], 1)], buf.at[k, pl.ds(tok, 1)], sem).start()
        return carry

    lax.fori_loop(0, tm, issue, 0, unroll=DMA_ISSUE_UNROLL)
    for k in range(TOP_K):
        pltpu.make_async_copy(ys_ref.at[pl.ds(0, tm)], buf.at[k], sem).wait()
    gt = gt_ref[...]
    lo = hi = None
    for k in range(TOP_K):
        a, b = _unpack_halves(buf[k])
        g = gt[:, k:k + 1]
        lo = g * a if lo is None else lo + g * a
        hi = g * b if hi is None else hi + g * b
    half = lo.shape[1]
    g2 = g2_ref[0]
    x_lo = x_ref[:, :half] + g2[:, :half] * lo
    x_hi = x_ref[:, half:] + g2[:, half:] * hi
    if final:
        ms = (jnp.sum(x_lo * x_lo, axis=-1, keepdims=True)
              + jnp.sum(x_hi * x_hi, axis=-1, keepdims=True)) / (2 * half)
        r = lax.rsqrt(ms + RMS_EPS)
        x_lo = x_lo * r * nf_ref[:, :half]
        x_hi = x_hi * r * nf_ref[:, half:]
    o_ref[:, :half] = x_lo
    o_ref[:, half:] = x_hi


def combine(x1, ys, dest, gates_t, gate2, norm_f, seq, final):
    t, d = x1.shape
    tm = min(TM_COMBINE, seq)
    return pl.pallas_call(
        functools.partial(_combine_body, tm=tm, final=final),
        grid=(t // tm,),
        in_specs=[pl.BlockSpec((TOP_K, tm), lambda i: (0, i), memory_space=pltpu.SMEM),
                  pl.BlockSpec((tm, d), lambda i: (i, 0)),
                  pl.BlockSpec((tm, TOP_K), lambda i: (i, 0)),
                  pl.BlockSpec((1, 1, d), lambda i: ((i * tm) // seq, 0, 0)),
                  pl.BlockSpec((1, d), lambda i: (0, 0)),
                  pl.BlockSpec(memory_space=pl.ANY)],
        out_specs=pl.BlockSpec((tm, d), lambda i: (i, 0)),
        out_shape=jax.ShapeDtypeStruct((t, d), F32),
        scratch_shapes=[pltpu.VMEM((TOP_K, tm, d // 2), U32), pltpu.SemaphoreType.DMA],
        compiler_params=_cparams(("arbitrary",)),
        name="combine",
    )(dest, x1, gates_t, gate2, norm_f, ys)


def _alibi_slopes(n):
    return jnp.exp2(-8.0 * jnp.arange(1, n + 1, dtype=F32) / n)


def _routing_tables(counts, eidx, rank, nb):
    counts = counts.reshape(-1)
    padded = (counts + BLK_FFN - 1) // BLK_FFN * BLK_FFN
    pad_end = jnp.cumsum(padded)
    pad_start = pad_end - padded
    e_ids = jnp.arange(N_EXPERTS, dtype=I32)[:, None, None]
    dest = rank + jnp.sum(jnp.where(eidx[None] == e_ids, pad_start[:, None, None], 0), axis=0)
    blk_row0 = jnp.arange(nb, dtype=I32) * BLK_FFN
    blk_expert = jnp.minimum(jnp.sum(pad_end[None, :] <= blk_row0[:, None], axis=1), N_EXPERTS - 1).astype(I32)
    blk_valid = jnp.clip(counts[blk_expert] - (blk_row0 - pad_start[blk_expert]), 0, BLK_FFN).astype(I32)
    n_used = (pad_end[-1:] // BLK_FFN).astype(I32)
    return dest.astype(I32), blk_expert, blk_valid, n_used


def kernel(x, c, w_ada, b_ada, norm1, w_in, sinks, lambda_q1, lambda_k1, lambda_q2, lambda_k2, subln,
           w_out, norm2, w_router, b_router, w_gate, b_gate, w_up, b_up, w_down, b_down, norm_f):
    bsz, seq, d = x.shape
    depth = w_ada.shape[0]
    t = bsz * seq
    wa_w, wb_w = N_HEADS_A * HEAD_DIM, N_HEADS_B * HEAD_DIM
    kvb_w = N_KV_B * HEAD_DIM
    wc_w = N_HEADS_C * 2 * HEAD_DIM
    off = [0]
    for wdt in (wa_w, wa_w, wa_w, wb_w, kvb_w, kvb_w, wc_w, wc_w, wc_w):
        off.append(off[-1] + wdt)
    blk = [o // LANES for o in off]

    group = N_HEADS_B // N_KV_B
    b_perm = [kv * group + g for g in range(group) for kv in range(N_KV_B)]
    col_perm = jnp.concatenate([jnp.arange(h * HEAD_DIM, (h + 1) * HEAD_DIM) for h in b_perm])
    q_scale = jnp.ones((off[-1],), F32)
    for s0, s1, extra in ((off[0], off[1], 1.0), (off[3], off[4], 1.0), (off[6], off[7], LOG2E)):
        q_scale = q_scale.at[s0:s1].set(HEAD_DIM ** -0.5 * extra)

    slopes_a = _alibi_slopes(N_HEADS_A)
    slopes_b = _alibi_slopes(N_HEADS_B)[jnp.array(b_perm)]
    slopes_c = _alibi_slopes(N_HEADS_C) * LOG2E

    mod_all = ada_modulation(c, w_ada, b_ada)
    x2d = x.reshape(t, d)
    nb = (t * TOP_K) // BLK_FFN + N_EXPERTS
    n_rows = nb * BLK_FFN

    for layer in range(depth):
        mod = mod_all[layer].reshape(bsz, N_MOD, 1, d)
        sh1, sc1, g1, sh2, sc2, g2 = [mod[:, i] for i in range(N_MOD)]

        w_l = w_in[layer] * q_scale[None, :]
        w_l = jnp.concatenate([w_l[:, :off[3]], w_l[:, off[3]:off[4]][:, col_perm], w_l[:, off[4]:]], axis=1)
        proj = in_projection(x2d, sc1, sh1, norm1[layer].reshape(1, d), w_l.astype(BF16), seq)
        proj3d = proj.reshape(bsz, seq, off[-1])

        n_blk_a = wa_w // LANES
        o_a = window_attention(proj3d, blk[0], blk[1], blk[2], 1, n_blk_a, slopes_a,
                               jnp.full((n_blk_a, 2), -jnp.inf, F32), DILATED_PAIRS).reshape(t, wa_w)

        sink_l = sinks[layer].astype(F32)[jnp.array(b_perm)].reshape(group, N_KV_B)
        o_b = window_attention(proj3d, blk[3], blk[4], blk[5], 0, group, slopes_b, sink_l,
                               ((2 * SWA_HALF, 1),)).reshape(t, wb_w)

        lam_init = jnp.full((1,), 0.8 - 0.6 * math.exp(-0.3 * layer), F32)
        heads_t = lambda i: (proj3d[:, :, off[i]:off[i + 1]].reshape(bsz, seq, N_HEADS_C, FLASH_DV)
                             .transpose(0, 2, 3, 1))
        norms = tile_norms(proj3d[:, :, off[6]:off[7]], proj3d[:, :, off[7]:off[8]], min(T_FLASH, seq))
        o_c = diff_attention(heads_t(6), proj3d, blk[7], heads_t(8), norms, slopes_c, lam_init,
                             lambda_q1[layer].reshape(1, -1), lambda_k1[layer].reshape(1, -1),
                             lambda_q2[layer].reshape(1, -1), lambda_k2[layer].reshape(1, -1),
                             subln[layer].reshape(-1, 1)).reshape(t, wc_w)

        wo = w_out[layer]
        wo_b = wo[wa_w:wa_w + wb_w].reshape(N_HEADS_B, HEAD_DIM, d)[jnp.array(b_perm)].reshape(wb_w, d)
        x1, h_packed, logits_t = out_projection(
            x2d, o_a, o_b, o_c, wo[:wa_w].astype(BF16), wo_b.astype(BF16), wo[wa_w + wb_w:].astype(BF16),
            g1, sc2, sh2, norm2[layer].reshape(1, d), w_router[layer].T, b_router[layer].reshape(-1, 1), seq)

        eidx, gates, rank, counts = route(logits_t)
        dest, blk_expert, blk_valid, n_used = _routing_tables(counts, eidx, rank, nb)
        xs = dispatch_rows(h_packed, dest, n_rows)
        ys = expert_ffn(xs, blk_expert, blk_valid, n_used, layer,
                        w_gate, b_gate[:, :, None, :], w_up, b_up[:, :, None, :], w_down, b_down[:, :, None, :])
        x2d = combine(x1, ys, dest, gates.T, g2, norm_f.reshape(1, d), seq, layer == depth - 1)

    return x2d.reshape(bsz, seq, d)
```

```python
import functools
import math

import jax
import jax.numpy as jnp
from jax import lax
from jax.experimental import pallas as pl
from jax.experimental.pallas import tpu as pltpu

F32 = jnp.float32
BF16 = jnp.bfloat16
I32 = jnp.int32
U32 = jnp.uint32

HEAD_DIM = 64
DILATED_PAIRS = ((128, 1), (512, 4), (2048, 16))
N_HEADS_A = 4
N_HEADS_B = 4
N_KV_B = 2
SWA_HALF = 128
N_HEADS_C = 4
N_EXPERTS = 32
TOP_K = 4
SWIGLU_LIMIT = 7.0
SWIGLU_ALPHA = 1.702
RMS_EPS = 1e-6
N_MOD = 6

LANES = 128
VMEM_LIMIT_BYTES = 56 * 1024 * 1024

TM_PROJ = 512
TQ_BAND = 256
T_FLASH = 1024
FLASH_PANEL = 512
FLASH_CHUNK = 256
FLASH_SUB = 128
FLASH_DV = 2 * HEAD_DIM
FLASH_ONES = 16
FLASH_AUG = 3
FLASH_ZERO_LOG2 = 150.0
FLASH_LOOKAHEAD = 2
FLASH_SAFE_LOG2 = 60.0
LOG2E = math.log2(math.e)
TM_ROUTE = 512
TM_DISPATCH = 512
BLK_FFN = 512
TM_COMBINE = 256
DMA_ISSUE_UNROLL = 8

HIGHEST = lax.Precision.HIGHEST
_NT = (((1,), (1,)), ((), ()))


def _cparams(sem):
    return pltpu.CompilerParams(dimension_semantics=sem, vmem_limit_bytes=VMEM_LIMIT_BYTES)


def _pack_halves(a, b):
    ua = lax.bitcast_convert_type(a.astype(BF16).astype(F32), U32)
    ub = lax.bitcast_convert_type(b.astype(BF16).astype(F32), U32)
    return ua | (ub >> 16)


def _unpack_halves(w):
    a = lax.bitcast_convert_type(w & jnp.uint32(0xFFFF0000), F32)
    b = lax.bitcast_convert_type(w << 16, F32)
    return a, b


def _rms_scale(x):
    return lax.rsqrt(jnp.mean(x * x, axis=-1, keepdims=True) + RMS_EPS)


def _ada_body(c_ref, w_ref, b_ref, o_ref):
    c = c_ref[...]
    cond = c / (1.0 + jnp.exp(-c))
    o_ref[0] = jnp.dot(cond, w_ref[0], preferred_element_type=F32, precision=HIGHEST) + b_ref[0]


def ada_modulation(c, w_ada, b_ada):
    depth, d, n = w_ada.shape
    bsz = c.shape[0]
    tn = 1536
    assert n % tn == 0
    return pl.pallas_call(
        _ada_body,
        grid=(depth, n // tn),
        in_specs=[
            pl.BlockSpec((bsz, d), lambda l, j: (0, 0)),
            pl.BlockSpec((1, d, tn), lambda l, j: (l, 0, j)),
            pl.BlockSpec((1, 1, tn), lambda l, j: (l, 0, j)),
        ],
        out_specs=pl.BlockSpec((1, bsz, tn), lambda l, j: (l, 0, j)),
        out_shape=jax.ShapeDtypeStruct((depth, bsz, n), F32),
        compiler_params=_cparams(("parallel", "parallel")),
        name="ada_modulation",
    )(c, w_ada, b_ada.reshape(depth, 1, n))


def _inproj_body(x_ref, sc_ref, sh_ref, g_ref, w_ref, o_ref):
    x = x_ref[...]
    h = x * _rms_scale(x) * g_ref[...]
    h = h * (1.0 + sc_ref[0]) + sh_ref[0]
    o_ref[...] = jnp.dot(h.astype(BF16), w_ref[...], preferred_element_type=F32).astype(o_ref.dtype)


def in_projection(x2d, scale, shift, gain, w_bf16, seq):
    t, d = x2d.shape
    n = w_bf16.shape[1]
    tm = min(TM_PROJ, seq)
    assert seq % tm == 0 and t % tm == 0
    mod_spec = pl.BlockSpec((1, 1, d), lambda i: ((i * tm) // seq, 0, 0))
    return pl.pallas_call(
        _inproj_body,
        grid=(t // tm,),
        in_specs=[
            pl.BlockSpec((tm, d), lambda i: (i, 0)),
            mod_spec, mod_spec,
            pl.BlockSpec((1, d), lambda i: (0, 0)),
            pl.BlockSpec((d, n), lambda i: (0, 0)),
        ],
        out_specs=pl.BlockSpec((tm, n), lambda i: (i, 0)),
        out_shape=jax.ShapeDtypeStruct((t, n), BF16),
        compiler_params=_cparams(("parallel",)),
        name="in_projection",
    )(x2d, scale, shift, gain, w_bf16)


def _offset_bias_body(c_ref, slope_ref, o_ref, *, pairs):
    vi = pl.program_id(0)
    hd = pl.program_id(1)
    tq, win = o_ref.shape[2], o_ref.shape[3]
    off = (lax.broadcasted_iota(I32, (tq, win), 1) - lax.broadcasted_iota(I32, (tq, win), 0)) + c_ref[vi]
    dist = jnp.abs(off)
    mult = jnp.zeros((tq, win), F32)
    for window, dil in pairs:
        hit = jnp.logical_and(dist <= window // 2, (dist & (dil - 1)) == 0)
        mult = mult + jnp.where(hit, 1.0, 0.0)
    o_ref[0, 0] = jnp.log(mult) - slope_ref[hd] * dist.astype(F32)


def _window_variants(seq, tq, reach):
    win = tq + 2 * reach
    nq, edge = seq // tq, -(-reach // tq)
    assert seq % tq == 0 and reach % 16 == 0 and win <= seq and nq >= 2 * edge + 1
    rel = [min(max(i * tq - reach, 0), seq - win) - i * tq for i in range(nq)]
    assert all(r == -reach for r in rel[edge:nq - edge])
    starts = rel[:edge] + [-reach] + rel[nq - edge:]

    def variant(i):
        return jnp.where(i < edge, i, jnp.where(i >= nq - edge, i - (nq - 2 * edge - 1), edge))

    return win, starts, variant


def _window_body(c_ref, sink_ref, q_ref, k_ref, v_ref, tab_ref, o_ref, *, tq, win, variant):
    j = pl.program_id(1)
    i = pl.program_id(2)
    ks = pl.multiple_of(i * tq + c_ref[variant(i)], 16)
    q = q_ref[0]
    k = k_ref[0, pl.ds(ks, win), :]
    v = v_ref[0, pl.ds(ks, win), :]
    lane_lo = lax.broadcasted_iota(I32, (1, LANES), 1) < HEAD_DIM
    outs = []
    for hf in range(2):
        qm = jnp.where(lane_lo if hf == 0 else jnp.logical_not(lane_lo), q, jnp.zeros_like(q))
        s = lax.dot_general(qm, k, _NT, preferred_element_type=F32) + tab_ref[0, hf]
        sink = sink_ref[j, hf]
        m = jnp.maximum(jnp.max(s, axis=-1, keepdims=True), sink)
        p = jnp.exp(s - m)
        l = jnp.sum(p, axis=-1, keepdims=True) + jnp.exp(sink - m)
        outs.append(jnp.dot(p.astype(BF16), v, preferred_element_type=F32) / l)
    o_ref[0] = jnp.where(lane_lo, outs[0], outs[1]).astype(o_ref.dtype)


def window_attention(proj3d, q_blk0, k_blk0, v_blk0, kv_step, n_blk, slopes, sinks, pairs):
    bsz, seq, _ = proj3d.shape
    assert all(d & (d - 1) == 0 for _, d in pairs)
    reach = max(w // 2 for w, _ in pairs)
    tq = TQ_BAND
    win, starts, variant = _window_variants(seq, tq, reach)
    starts = jnp.array(starts, I32)
    smem = pl.BlockSpec(memory_space=pltpu.SMEM)
    tables = pl.pallas_call(
        functools.partial(_offset_bias_body, pairs=pairs),
        grid=(starts.shape[0], 2 * n_blk),
        in_specs=[smem, smem],
        out_specs=pl.BlockSpec((1, 1, tq, win), lambda vi, hd: (vi, hd, 0, 0)),
        out_shape=jax.ShapeDtypeStruct((starts.shape[0], 2 * n_blk, tq, win), F32),
        compiler_params=_cparams(("parallel", "parallel")),
        name="offset_bias_tables",
    )(starts, slopes)
    return pl.pallas_call(
        functools.partial(_window_body, tq=tq, win=win, variant=variant),
        grid=(bsz, n_blk, seq // tq),
        in_specs=[
            smem, smem,
            pl.BlockSpec((1, tq, LANES), lambda b, j, i: (b, i, q_blk0 + j)),
            pl.BlockSpec((1, seq, LANES), lambda b, j, i: (b, 0, k_blk0 + kv_step * j)),
            pl.BlockSpec((1, seq, LANES), lambda b, j, i: (b, 0, v_blk0 + kv_step * j)),
            pl.BlockSpec((1, 2, tq, win), lambda b, j, i: (variant(i), j, 0, 0)),
        ],
        out_specs=pl.BlockSpec((1, tq, LANES), lambda b, j, i: (b, i, j)),
        out_shape=jax.ShapeDtypeStruct((bsz, seq, n_blk * LANES), BF16),
        compiler_params=_cparams(("parallel", "parallel", "parallel")),
        name="window_attention",
    )(starts, sinks, proj3d, proj3d, proj3d, tables)


def _flash_body(slope_ref, lami_ref, qn_ref, kn_ref, qt_ref, k_ref, kaug_ref, vt_ref, diag_ref, lq1_ref, lk1_ref,
                lq2_ref, lk2_ref, sub_ref, o_ref, m_s, acc_s, s_scr, mmin_s, *, t):
    b = pl.program_id(0)
    h = pl.program_id(1)
    qi = pl.program_id(2)
    ki = pl.program_id(3)
    nt = pl.num_programs(3)
    kt = lax.rem(qi + ki, nt)

    @pl.when(ki == 0)
    def _():
        m_s[...] = jnp.full(m_s.shape, -jnp.inf, F32)
        acc_s[...] = jnp.zeros(acc_s.shape, F32)
        mmin_s[0] = jnp.float32(0.0)

    slope = slope_ref[h]
    n_chunk = t // FLASH_CHUNK
    n_panel = t // FLASH_PANEL
    sub_per_panel = FLASH_PANEL // FLASH_CHUNK
    row = lax.broadcasted_iota(I32, (FLASH_DV, 1), 0)
    row_lo = row < HEAD_DIM
    row_aug = lax.broadcasted_iota(I32, (FLASH_ONES, 1), 0) < FLASH_AUG
    lane_pos = lax.broadcasted_iota(I32, (1, FLASH_CHUNK), 1).astype(F32)
    ones_rows = jnp.ones((FLASH_ONES, FLASH_CHUNK), BF16)

    def with_sign(qtm, sign):
        aug = jnp.where(row_aug, sign, 0.0).astype(BF16)
        pad = jnp.zeros((FLASH_DV - FLASH_ONES, sign.shape[1]), BF16)
        return jnp.concatenate([qtm, aug, pad], axis=0)

    def keys(kc):
        rows = slice(kc * FLASH_CHUNK, (kc + 1) * FLASH_CHUNK)
        return jnp.concatenate([k_ref[0, rows, :], kaug_ref[0, rows, :]], axis=1)

    def values(kc):
        return jnp.concatenate([vt_ref[0, 0, :, kc * FLASH_CHUNK:(kc + 1) * FLASH_CHUNK], ones_rows], axis=0)
    blocks = [(qp, mp) for qp in range(n_panel) for mp in range(2)]

    def fold8(x, op):
        return functools.reduce(op, [x[r:r + 8] for r in range(0, x.shape[0], 8)])

    def pair_terms(qp, kc):
        signs, shifts = [], []
        for j in range(sub_per_panel):
            gap = (qi * n_chunk + qp * sub_per_panel + j) - (kt * n_chunk + kc)
            sign = jnp.sign(gap).astype(F32)
            signs.append(jnp.full((1, FLASH_CHUNK), sign, F32))
            shifts.append(-slope * ((jnp.abs(gap) * FLASH_CHUNK).astype(F32) + sign * lane_pos))
        return jnp.concatenate(signs, axis=1), jnp.concatenate(shifts, axis=1)

    def masked_queries(bi):
        qp, mp = blocks[bi]
        qt = qt_ref[0, 0, :, qp * FLASH_PANEL:(qp + 1) * FLASH_PANEL]
        return jnp.where(row_lo if mp == 0 else jnp.logical_not(row_lo), qt, jnp.zeros_like(qt))

    def score_chunk(bi, kc, qtm, mx):
        qp, _ = blocks[bi]
        k0 = kc * FLASH_CHUNK
        sign, lane_shift = pair_terms(qp, kc)
        s = jnp.dot(keys(kc), with_sign(qtm, sign), preferred_element_type=F32)
        j = kc - qp * sub_per_panel
        if 0 <= j < sub_per_panel:
            cols = [s[:, i * FLASH_CHUNK:(i + 1) * FLASH_CHUNK] for i in range(sub_per_panel)]
            cols[j] = cols[j] + diag_ref[0, 0]
            s = jnp.concatenate(cols, axis=1)
        s_scr[bi % 2, k0:k0 + FLASH_CHUNK, :] = s
        cm = fold8(s, jnp.maximum) + lane_shift
        return cm if mx is None else jnp.maximum(mx, cm)

    def prob_chunk(bi, kc, m_new, acc):
        qp, _ = blocks[bi]
        k0 = kc * FLASH_CHUNK
        shift = m_new - pair_terms(qp, kc)[1]
        parts = []
        for r0 in range(k0, k0 + FLASH_CHUNK, FLASH_SUB):
            parts.append(jnp.exp2((s_scr[bi % 2, r0:r0 + FLASH_SUB, :] - shift).astype(BF16)))
        pv = jnp.dot(values(kc), jnp.concatenate(parts, axis=0), preferred_element_type=F32)
        return pv if acc is None else acc + pv

    def two_pass_tile():
        qtm_next = masked_queries(0)
        mx_next = None
        for kc in range(n_chunk):
            mx_next = score_chunk(0, kc, qtm_next, mx_next)
        for bi, (qp, mp) in enumerate(blocks):
            qs = slice(qp * FLASH_PANEL, (qp + 1) * FLASH_PANEL)
            m_old = m_s[mp, :, qs]
            m_new = jnp.maximum(m_old, jnp.max(mx_next, axis=0, keepdims=True))
            alpha = jnp.exp2(m_old - m_new)
            acc = None
            has_next = bi + 1 < len(blocks)
            if has_next:
                qtm_next = masked_queries(bi + 1)
            mx_next = None
            for kc in range(n_chunk):
                if has_next:
                    mx_next = score_chunk(bi + 1, kc, qtm_next, mx_next)
                acc = prob_chunk(bi, kc, m_new, acc)
            acc_s[mp, :, qs] = alpha * acc_s[mp, :, qs] + acc
            m_s[mp, :, qs] = m_new
        mmin_s[0] = jnp.min(m_s[...])

    def one_pass_tile():
        items = [(bi, kc) for bi in range(len(blocks)) for kc in range(n_chunk)]

        def scores(bi, kc):
            qp, mp = blocks[bi]
            sign, lane_shift = pair_terms(qp, kc)
            s = jnp.dot(keys(kc), with_sign(masked_queries(bi), sign), preferred_element_type=F32)
            return s, m_s[mp, :, qp * FLASH_PANEL:(qp + 1) * FLASH_PANEL] - lane_shift

        ahead = [scores(*items[i]) for i in range(FLASH_LOOKAHEAD)]
        acc = None
        for n, (bi, kc) in enumerate(items):
            s, shift = ahead.pop(0)
            if n + FLASH_LOOKAHEAD < len(items):
                ahead.append(scores(*items[n + FLASH_LOOKAHEAD]))
            p = jnp.exp2(s - shift).astype(BF16)
            pv = jnp.dot(values(kc), p, preferred_element_type=F32)
            acc = pv if kc == 0 else acc + pv
            if kc == n_chunk - 1:
                qp, mp = blocks[bi]
                qs = slice(qp * FLASH_PANEL, (qp + 1) * FLASH_PANEL)
                acc_s[mp, :, qs] = acc_s[mp, :, qs] + acc

    bound = qn_ref[(b * pl.num_programs(1) + h) * nt + qi] * kn_ref[(b * pl.num_programs(1) + h) * nt + kt]
    min_dist = jnp.maximum((jnp.abs(qi - kt) - 1) * t + 1, 0).astype(F32)
    headroom = bound - mmin_s[0]
    all_zero = jnp.logical_and(ki > 0, headroom - slope * min_dist <= -FLASH_ZERO_LOG2)
    safe = jnp.logical_and(ki > 0, headroom <= FLASH_SAFE_LOG2)
    pl.when(jnp.logical_and(safe, jnp.logical_not(all_zero)))(one_pass_tile)
    pl.when(jnp.logical_not(jnp.logical_or(safe, all_zero)))(two_pass_tile)

    @pl.when(ki == pl.num_programs(3) - 1)
    def _():
        lam_init = lami_ref[0]
        lam = (jnp.exp(jnp.sum(lq1_ref[...] * lk1_ref[...], axis=-1, keepdims=True))
               - jnp.exp(jnp.sum(lq2_ref[...] * lk2_ref[...], axis=-1, keepdims=True)) + lam_init)
        o1 = acc_s[0, :FLASH_DV, :] / acc_s[0, FLASH_DV:FLASH_DV + 1, :]
        o2 = acc_s[1, :FLASH_DV, :] / acc_s[1, FLASH_DV:FLASH_DV + 1, :]
        o = o1 - lam * o2
        r = lax.rsqrt(jnp.mean(o * o, axis=0, keepdims=True) + RMS_EPS)
        y = o * r * sub_ref[...] * (1.0 - lam_init)
        o_ref[0] = y.T.astype(o_ref.dtype)


def _split_bf16(x, n):
    parts = []
    for _ in range(n):
        p = x.astype(BF16)
        parts.append(p)
        x = x - p.astype(F32)
    return parts


def alibi_key_columns(slopes, t):
    pos = (jnp.arange(t) % FLASH_CHUNK).astype(F32)
    aug = jnp.stack(_split_bf16(slopes[:, None] * pos[None, :], FLASH_AUG), axis=-1)
    return jnp.concatenate([aug, jnp.zeros((slopes.shape[0], t, FLASH_DV - FLASH_AUG), BF16)], axis=-1)


def _tile_norms_body(q_ref, k_ref, o_ref, *, nh):
    lane = lax.broadcasted_iota(I32, (1, 2 * nh), 1)
    out = jnp.zeros((1, 2 * nh), F32)
    for j, ref in enumerate((q_ref, k_ref)):
        for hh in range(nh):
            x = ref[0, :, hh * FLASH_DV:(hh + 1) * FLASH_DV].astype(F32)
            n2 = jnp.max(jnp.sum(x * x, axis=1, keepdims=True), axis=0, keepdims=True)
            out = jnp.where(lane == j * nh + hh, jnp.sqrt(n2), out)
    o_ref[0, 0] = out


def tile_norms(q3, k3, t):
    bsz, seq, w = q3.shape
    nh = w // FLASH_DV
    spec = pl.BlockSpec((1, t, w), lambda b, i: (b, i, 0))
    return pl.pallas_call(
        functools.partial(_tile_norms_body, nh=nh),
        grid=(bsz, seq // t),
        in_specs=[spec, spec],
        out_specs=pl.BlockSpec((1, 1, 1, 2 * nh), lambda b, i: (b, i, 0, 0)),
        out_shape=jax.ShapeDtypeStruct((bsz, seq // t, 1, 2 * nh), F32),
        compiler_params=_cparams(("parallel", "parallel")),
        name="tile_norms",
    )(q3, k3)


def diff_attention(qt, proj3d, k_blk0, vt, norms, slopes, lam_init, lq1, lk1, lq2, lk2, subln):
    bsz, nh, _, seq = qt.shape
    t = min(T_FLASH, seq)
    assert seq % t == 0 and t % FLASH_PANEL == 0 and FLASH_PANEL % FLASH_CHUNK == 0
    nt = seq // t
    norms = norms.reshape(bsz, nt, 2, nh).transpose(2, 0, 3, 1).reshape(2, bsz * nh * nt)
    k_cols = alibi_key_columns(slopes, t)
    dv = FLASH_DV
    dva = FLASH_DV + FLASH_ONES
    r = jnp.arange(FLASH_CHUNK, dtype=F32)
    diag = -slopes[:, None, None] * jnp.abs(r[None, :] - r[:, None])[None]
    diag = jnp.stack([jnp.zeros_like(diag), diag], axis=1)
    smem = pl.BlockSpec(memory_space=pltpu.SMEM)
    vec = lambda n: pl.BlockSpec((1, n), lambda b, h, qi, ki: (0, 0))
    return pl.pallas_call(
        functools.partial(_flash_body, t=t),
        grid=(bsz, nh, nt, nt),
        in_specs=[
            smem, smem, smem, smem,
            pl.BlockSpec((1, 1, dv, t), lambda b, h, qi, ki: (b, h, 0, qi)),
            pl.BlockSpec((1, t, dv), lambda b, h, qi, ki: (b, (qi + ki) % nt, k_blk0 + h)),
            pl.BlockSpec((1, t, dv), lambda b, h, qi, ki: (h, 0, 0)),
            pl.BlockSpec((1, 1, dv, t), lambda b, h, qi, ki: (b, h, 0, (qi + ki) % nt)),
            pl.BlockSpec((1, 1, FLASH_CHUNK, FLASH_CHUNK),
                         lambda b, h, qi, ki: (h, (ki == 0).astype(I32), 0, 0)),
            vec(HEAD_DIM), vec(HEAD_DIM), vec(HEAD_DIM), vec(HEAD_DIM),
            pl.BlockSpec((dv, 1), lambda b, h, qi, ki: (0, 0)),
        ],
        out_specs=pl.BlockSpec((1, t, LANES), lambda b, h, qi, ki: (b, qi, h)),
        out_shape=jax.ShapeDtypeStruct((bsz, seq, nh * dv), BF16),
        scratch_shapes=[
            pltpu.VMEM((2, 1, t), F32),
            pltpu.VMEM((2, dva, t), F32),
            pltpu.VMEM((2, t, FLASH_PANEL), F32),
            pltpu.SMEM((1,), F32),
        ],
        compiler_params=_cparams(("parallel", "parallel", "parallel", "arbitrary")),
        name="diff_attention",
    )(slopes, lam_init, norms[0], norms[1], qt, proj3d, k_cols, vt, diag, lq1, lk1, lq2, lk2, subln)


def _outproj_body(x_ref, oa_ref, ob_ref, oc_ref, wa_ref, wb_ref, wc_ref, g1_ref, sc_ref, sh_ref,
                  n2_ref, wr_ref, br_ref, x1_ref, hp_ref, lg_ref):
    mix = (jnp.dot(oa_ref[...], wa_ref[...], preferred_element_type=F32)
           + jnp.dot(ob_ref[...], wb_ref[...], preferred_element_type=F32)
           + jnp.dot(oc_ref[...], wc_ref[...], preferred_element_type=F32))
    x1 = x_ref[...] + g1_ref[0] * mix
    x1_ref[...] = x1
    h = x1 * _rms_scale(x1) * n2_ref[...]
    h = h * (1.0 + sc_ref[0]) + sh_ref[0]
    half = h.shape[1] // 2
    hp_ref[...] = _pack_halves(h[:, :half], h[:, half:])
    lg_ref[...] = lax.dot_general(wr_ref[...], h, _NT, preferred_element_type=F32,
                                  precision=HIGHEST) + br_ref[...]


def out_projection(x2d, oa, ob, oc, wa, wb, wc, gate1, scale2, shift2, gain2, w_router_t, b_router, seq):
    t, d = x2d.shape
    e = w_router_t.shape[0]
    tm = min(TM_PROJ, seq)
    mod_spec = pl.BlockSpec((1, 1, d), lambda i: ((i * tm) // seq, 0, 0))
    row = lambda w: pl.BlockSpec((tm, w), lambda i: (i, 0))
    full = lambda a: pl.BlockSpec(a.shape, lambda i: (0, 0))
    return pl.pallas_call(
        _outproj_body,
        grid=(t // tm,),
        in_specs=[row(d), row(oa.shape[1]), row(ob.shape[1]), row(oc.shape[1]),
                  full(wa), full(wb), full(wc), mod_spec, mod_spec, mod_spec,
                  pl.BlockSpec((1, d), lambda i: (0, 0)), full(w_router_t),
                  pl.BlockSpec((e, 1), lambda i: (0, 0))],
        out_specs=[row(d), row(d // 2), pl.BlockSpec((e, tm), lambda i: (0, i))],
        out_shape=[jax.ShapeDtypeStruct((t, d), F32),
                   jax.ShapeDtypeStruct((t, d // 2), U32),
                   jax.ShapeDtypeStruct((e, t), F32)],
        compiler_params=_cparams(("parallel",)),
        name="out_projection",
    )(x2d, oa, ob, oc, wa, wb, wc, gate1, scale2, shift2, gain2, w_router_t, b_router)


def _router_body(lg_ref, tri_ref, eidx_ref, gate_ref, rank_ref, cnt_ref, base_s):
    step = pl.program_id(0)

    @pl.when(step == 0)
    def _():
        base_s[...] = jnp.zeros(base_s.shape, F32)

    work = lg_ref[...]
    n_e, tm = work.shape
    eio = lax.broadcasted_iota(I32, (n_e, tm), 0).astype(F32)
    vals, hots = [], []
    for k in range(TOP_K):
        mx = jnp.max(work, axis=0, keepdims=True)
        idx = jnp.min(jnp.where(work == mx, eio, float(n_e)), axis=0, keepdims=True)
        hot = eio == idx
        vals.append(mx)
        hots.append(hot)
        eidx_ref[k:k + 1, :] = idx.astype(I32)
        work = jnp.where(hot, -jnp.inf, work)
    exps = [jnp.exp(v - vals[0]) for v in vals]
    den = functools.reduce(lambda a, b: a + b, exps)
    for k in range(TOP_K):
        gate_ref[k:k + 1, :] = exps[k] / den
    any_hot = functools.reduce(lambda a, b: a + b, [h.astype(F32) for h in hots])
    prefix = jnp.dot(any_hot.astype(BF16), tri_ref[...], preferred_element_type=F32)
    pos = base_s[...] + prefix
    for k in range(TOP_K):
        rank_ref[k:k + 1, :] = jnp.sum(jnp.where(hots[k], pos, 0.0), axis=0, keepdims=True).astype(I32)
    base_s[...] = base_s[...] + jnp.sum(any_hot, axis=1, keepdims=True)
    cnt_ref[...] = base_s[...].astype(I32)


def route(logits_t):
    n_e, t = logits_t.shape
    tm = min(TM_ROUTE, t)
    assert t % tm == 0
    r = jnp.arange(tm)
    tri = (r[:, None] < r[None, :]).astype(BF16)
    out4 = lambda dt: jax.ShapeDtypeStruct((TOP_K, t), dt)
    spec4 = pl.BlockSpec((TOP_K, tm), lambda i: (0, i))
    return pl.pallas_call(
        _router_body,
        grid=(t // tm,),
        in_specs=[pl.BlockSpec((n_e, tm), lambda i: (0, i)),
                  pl.BlockSpec((tm, tm), lambda i: (0, 0))],
        out_specs=[spec4, spec4, spec4, pl.BlockSpec((n_e, 1), lambda i: (0, 0))],
        out_shape=[out4(I32), out4(F32), out4(I32), jax.ShapeDtypeStruct((n_e, 1), I32)],
        scratch_shapes=[pltpu.VMEM((n_e, 1), F32)],
        compiler_params=_cparams(("arbitrary",)),
        name="route",
    )(logits_t, tri)


def _dispatch_body(dest_ref, h_ref, init_ref, xs_ref, sem, *, tm):
    del init_ref

    def issue(tok, carry):
        for k in range(TOP_K):
            pltpu.make_async_copy(h_ref.at[pl.ds(tok, 1)], xs_ref.at[pl.ds(dest_ref[k, tok], 1)], sem).start()
        return carry

    lax.fori_loop(0, tm, issue, 0, unroll=DMA_ISSUE_UNROLL)
    for k in range(TOP_K):
        pltpu.make_async_copy(h_ref, xs_ref.at[pl.ds(0, tm)], sem).wait()


def dispatch_rows(h_packed, dest, n_rows):
    t, w = h_packed.shape
    tm = min(TM_DISPATCH, t)
    return pl.pallas_call(
        functools.partial(_dispatch_body, tm=tm),
        grid=(t // tm,),
        in_specs=[pl.BlockSpec((TOP_K, tm), lambda i: (0, i), memory_space=pltpu.SMEM),
                  pl.BlockSpec((tm, w), lambda i: (i, 0)),
                  pl.BlockSpec(memory_space=pl.ANY)],
        out_specs=pl.BlockSpec(memory_space=pl.ANY),
        out_shape=jax.ShapeDtypeStruct((n_rows, w), U32),
        input_output_aliases={2: 0},
        scratch_shapes=[pltpu.SemaphoreType.DMA],
        compiler_params=_cparams(("arbitrary",)),
        name="dispatch_rows",
    )(dest, h_packed, jnp.zeros((n_rows, w), U32))


def _ffn_body(be_ref, bv_ref, nu_ref, xs_ref, wg_ref, bg_ref, wu_ref, bu_ref, wd_ref, bd_ref, o_ref,
              wg_s, wu_s, wd_s):
    i = pl.program_id(0)

    @pl.when(i < nu_ref[0])
    def _():
        @pl.when(jnp.logical_or(i == 0, be_ref[i] != be_ref[jnp.maximum(i - 1, 0)]))
        def _():
            wg_s[...] = wg_ref[0, 0].astype(BF16)
            wu_s[...] = wu_ref[0, 0].astype(BF16)
            wd_s[...] = wd_ref[0, 0].astype(BF16)

        xa, xb = _unpack_halves(xs_ref[...])
        blk, half = xa.shape
        valid = lax.broadcasted_iota(I32, (blk, 1), 0) < bv_ref[i]
        xa = jnp.where(valid, xa, 0.0).astype(BF16)
        xb = jnp.where(valid, xb, 0.0).astype(BF16)

        def proj(w_s, b_ref):
            return (jnp.dot(xa, w_s[:half, :], preferred_element_type=F32)
                    + jnp.dot(xb, w_s[half:, :], preferred_element_type=F32) + b_ref[0, 0])

        a = jnp.minimum(proj(wg_s, bg_ref), SWIGLU_LIMIT)
        u = jnp.clip(proj(wu_s, bu_ref), -SWIGLU_LIMIT, SWIGLU_LIMIT)
        hid = a * (1.0 / (1.0 + jnp.exp(-SWIGLU_ALPHA * a))) * (u + 1.0)
        out = jnp.dot(hid.astype(BF16), wd_s[...], preferred_element_type=F32) + bd_ref[0, 0]
        d = out.shape[1] // 2
        o_ref[...] = _pack_halves(out[:, :d], out[:, d:])

    @pl.when(i >= nu_ref[0])
    def _():
        o_ref[...] = jnp.zeros(o_ref.shape, o_ref.dtype)


def expert_ffn(xs, blk_expert, blk_valid, n_used, layer, wg, bg, wu, bu, wd, bd):
    n_rows, w = xs.shape
    _, n_e, d, f = wg.shape
    nb = n_rows // BLK_FFN
    row_map = lambda i, be, bv, nu: (jnp.minimum(i, nu[0] - 1), 0)
    exp_map = lambda i, be, bv, nu: (layer, be[i], 0, 0)
    grid_spec = pltpu.PrefetchScalarGridSpec(
        num_scalar_prefetch=3,
        grid=(nb,),
        in_specs=[pl.BlockSpec((BLK_FFN, w), row_map),
                  pl.BlockSpec((1, 1, d, f), exp_map), pl.BlockSpec((1, 1, 1, f), exp_map),
                  pl.BlockSpec((1, 1, d, f), exp_map), pl.BlockSpec((1, 1, 1, f), exp_map),
                  pl.BlockSpec((1, 1, f, d), exp_map), pl.BlockSpec((1, 1, 1, d), exp_map)],
        out_specs=pl.BlockSpec((BLK_FFN, d // 2), lambda i, be, bv, nu: (i, 0)),
        scratch_shapes=[pltpu.VMEM((d, f), BF16), pltpu.VMEM((d, f), BF16), pltpu.VMEM((f, d), BF16)],
    )
    return pl.pallas_call(
        _ffn_body,
        grid_spec=grid_spec,
        out_shape=jax.ShapeDtypeStruct((n_rows, d // 2), U32),
        compiler_params=_cparams(("arbitrary",)),
        name="expert_ffn",
    )(blk_expert, blk_valid, n_used, xs, wg, bg, wu, bu, wd, bd)


def _combine_body(dest_ref, x_ref, gt_ref, g2_ref, nf_ref, ys_ref, o_ref, buf, sem, *, tm, final):
    def issue(tok, carry):
        for k in range(TOP_K):
            pltpu.make_async_copy(ys_ref.at[pl.ds(dest_ref[k, tok], 1)], buf.at[k, pl.ds(tok, 1)], sem).start()
        return carry

    lax.fori_loop(0, tm, issue, 0, unroll=DMA_ISSUE_UNROLL)
    for k in range(TOP_K):
        pltpu.make_async_copy(ys_ref.at[pl.ds(0, tm)], buf.at[k], sem).wait()
    gt = gt_ref[...]
    lo = hi = None
    for k in range(TOP_K):
        a, b = _unpack_halves(buf[k])
        g = gt[:, k:k + 1]
        lo = g * a if lo is None else lo + g * a
        hi = g * b if hi is None else hi + g * b
    half = lo.shape[1]
    g2 = g2_ref[0]
    x_lo = x_ref[:, :half] + g2[:, :half] * lo
    x_hi = x_ref[:, half:] + g2[:, half:] * hi
    if final:
        ms = (jnp.sum(x_lo * x_lo, axis=-1, keepdims=True)
              + jnp.sum(x_hi * x_hi, axis=-1, keepdims=True)) / (2 * half)
        r = lax.rsqrt(ms + RMS_EPS)
        x_lo = x_lo * r * nf_ref[:, :half]
        x_hi = x_hi * r * nf_ref[:, half:]
    o_ref[:, :half] = x_lo
    o_ref[:, half:] = x_hi


def combine(x1, ys, dest, gates_t, gate2, norm_f, seq, final):
    t, d = x1.shape
    tm = min(TM_COMBINE, seq)
    return pl.pallas_call(
        functools.partial(_combine_body, tm=tm, final=final),
        grid=(t // tm,),
        in_specs=[pl.BlockSpec((TOP_K, tm), lambda i: (0, i), memory_space=pltpu.SMEM),
                  pl.BlockSpec((tm, d), lambda i: (i, 0)),
                  pl.BlockSpec((tm, TOP_K), lambda i: (i, 0)),
                  pl.BlockSpec((1, 1, d), lambda i: ((i * tm) // seq, 0, 0)),
                  pl.BlockSpec((1, d), lambda i: (0, 0)),
                  pl.BlockSpec(memory_space=pl.ANY)],
        out_specs=pl.BlockSpec((tm, d), lambda i: (i, 0)),
        out_shape=jax.ShapeDtypeStruct((t, d), F32),
        scratch_shapes=[pltpu.VMEM((TOP_K, tm, d // 2), U32), pltpu.SemaphoreType.DMA],
        compiler_params=_cparams(("arbitrary",)),
        name="combine",
    )(dest, x1, gates_t, gate2, norm_f, ys)


def _alibi_slopes(n):
    return jnp.exp2(-8.0 * jnp.arange(1, n + 1, dtype=F32) / n)


def _routing_tables(counts, eidx, rank, nb):
    counts = counts.reshape(-1)
    padded = (counts + BLK_FFN - 1) // BLK_FFN * BLK_FFN
    pad_end = jnp.cumsum(padded)
    pad_start = pad_end - padded
    e_ids = jnp.arange(N_EXPERTS, dtype=I32)[:, None, None]
    dest = rank + jnp.sum(jnp.where(eidx[None] == e_ids, pad_start[:, None, None], 0), axis=0)
    blk_row0 = jnp.arange(nb, dtype=I32) * BLK_FFN
    blk_expert = jnp.minimum(jnp.sum(pad_end[None, :] <= blk_row0[:, None], axis=1), N_EXPERTS - 1).astype(I32)
    blk_valid = jnp.clip(counts[blk_expert] - (blk_row0 - pad_start[blk_expert]), 0, BLK_FFN).astype(I32)
    n_used = (pad_end[-1:] // BLK_FFN).astype(I32)
    return dest.astype(I32), blk_expert, blk_valid, n_used


def kernel(x, c, w_ada, b_ada, norm1, w_in, sinks, lambda_q1, lambda_k1, lambda_q2, lambda_k2, subln,
           w_out, norm2, w_router, b_router, w_gate, b_gate, w_up, b_up, w_down, b_down, norm_f):
    bsz, seq, d = x.shape
    depth = w_ada.shape[0]
    t = bsz * seq
    wa_w, wb_w = N_HEADS_A * HEAD_DIM, N_HEADS_B * HEAD_DIM
    kvb_w = N_KV_B * HEAD_DIM
    wc_w = N_HEADS_C * 2 * HEAD_DIM
    off = [0]
    for wdt in (wa_w, wa_w, wa_w, wb_w, kvb_w, kvb_w, wc_w, wc_w, wc_w):
        off.append(off[-1] + wdt)
    blk = [o // LANES for o in off]

    group = N_HEADS_B // N_KV_B
    b_perm = [kv * group + g for g in range(group) for kv in range(N_KV_B)]
    col_perm = jnp.concatenate([jnp.arange(h * HEAD_DIM, (h + 1) * HEAD_DIM) for h in b_perm])
    q_scale = jnp.ones((off[-1],), F32)
    for s0, s1, extra in ((off[0], off[1], 1.0), (off[3], off[4], 1.0), (off[6], off[7], LOG2E)):
        q_scale = q_scale.at[s0:s1].set(HEAD_DIM ** -0.5 * extra)

    slopes_a = _alibi_slopes(N_HEADS_A)
    slopes_b = _alibi_slopes(N_HEADS_B)[jnp.array(b_perm)]
    slopes_c = _alibi_slopes(N_HEADS_C) * LOG2E

    mod_all = ada_modulation(c, w_ada, b_ada)
    x2d = x.reshape(t, d)
    nb = (t * TOP_K) // BLK_FFN + N_EXPERTS
    n_rows = nb * BLK_FFN

    for layer in range(depth):
        mod = mod_all[layer].reshape(bsz, N_MOD, 1, d)
        sh1, sc1, g1, sh2, sc2, g2 = [mod[:, i] for i in range(N_MOD)]

        w_l = w_in[layer] * q_scale[None, :]
        w_l = jnp.concatenate([w_l[:, :off[3]], w_l[:, off[3]:off[4]][:, col_perm], w_l[:, off[4]:]], axis=1)
        proj = in_projection(x2d, sc1, sh1, norm1[layer].reshape(1, d), w_l.astype(BF16), seq)
        proj3d = proj.reshape(bsz, seq, off[-1])

        n_blk_a = wa_w // LANES
        o_a = window_attention(proj3d, blk[0], blk[1], blk[2], 1, n_blk_a, slopes_a,
                               jnp.full((n_blk_a, 2), -jnp.inf, F32), DILATED_PAIRS).reshape(t, wa_w)

        sink_l = sinks[layer].astype(F32)[jnp.array(b_perm)].reshape(group, N_KV_B)
        o_b = window_attention(proj3d, blk[3], blk[4], blk[5], 0, group, slopes_b, sink_l,
                               ((2 * SWA_HALF, 1),)).reshape(t, wb_w)

        lam_init = jnp.full((1,), 0.8 - 0.6 * math.exp(-0.3 * layer), F32)
        heads_t = lambda i: (proj3d[:, :, off[i]:off[i + 1]].reshape(bsz, seq, N_HEADS_C, FLASH_DV)
                             .transpose(0, 2, 3, 1))
        norms = tile_norms(proj3d[:, :, off[6]:off[7]], proj3d[:, :, off[7]:off[8]], min(T_FLASH, seq))
        o_c = diff_attention(heads_t(6), proj3d, blk[7], heads_t(8), norms, slopes_c, lam_init,
                             lambda_q1[layer].reshape(1, -1), lambda_k1[layer].reshape(1, -1),
                             lambda_q2[layer].reshape(1, -1), lambda_k2[layer].reshape(1, -1),
                             subln[layer].reshape(-1, 1)).reshape(t, wc_w)

        wo = w_out[layer]
        wo_b = wo[wa_w:wa_w + wb_w].reshape(N_HEADS_B, HEAD_DIM, d)[jnp.array(b_perm)].reshape(wb_w, d)
        x1, h_packed, logits_t = out_projection(
            x2d, o_a, o_b, o_c, wo[:wa_w].astype(BF16), wo_b.astype(BF16), wo[wa_w + wb_w:].astype(BF16),
            g1, sc2, sh2, norm2[layer].reshape(1, d), w_router[layer].T, b_router[layer].reshape(-1, 1), seq)

        eidx, gates, rank, counts = route(logits_t)
        dest, blk_expert, blk_valid, n_used = _routing_tables(counts, eidx, rank, nb)
        xs = dispatch_rows(h_packed, dest, n_rows)
        ys = expert_ffn(xs, blk_expert, blk_valid, n_used, layer,
                        w_gate, b_gate[:, :, None, :], w_up, b_up[:, :, None, :], w_down, b_down[:, :, None, :])
        x2d = combine(x1, ys, dest, gates.T, g2, norm_f.reshape(1, d), seq, layer == depth - 1)

    return x2d.reshape(bsz, seq, d)
```

```python
import functools
import math

import jax
import jax.numpy as jnp
from jax import lax
from jax.experimental import pallas as pl
from jax.experimental.pallas import tpu as pltpu

F32 = jnp.float32
BF16 = jnp.bfloat16
I32 = jnp.int32
U32 = jnp.uint32

HEAD_DIM = 64
DILATED_PAIRS = ((128, 1), (512, 4), (2048, 16))
N_HEADS_A = 4
N_HEADS_B = 4
N_KV_B = 2
SWA_HALF = 128
N_HEADS_C = 4
N_EXPERTS = 32
TOP_K = 4
SWIGLU_LIMIT = 7.0
SWIGLU_ALPHA = 1.702
RMS_EPS = 1e-6
N_MOD = 6

LANES = 128
VMEM_LIMIT_BYTES = 56 * 1024 * 1024

TM_PROJ = 512
TQ_BAND = 256
T_FLASH = 1024
FLASH_PANEL = 512
FLASH_CHUNK = 256
FLASH_SUB = 128
FLASH_DV = 2 * HEAD_DIM
FLASH_ONES = 16
FLASH_AUG = 3
FLASH_ZERO_LOG2 = 150.0
FLASH_LOOKAHEAD = 2
FLASH_SAFE_LOG2 = 60.0
LOG2E = math.log2(math.e)
TM_ROUTE = 512
TM_DISPATCH = 512
BLK_FFN = 512
TM_COMBINE = 256
DMA_ISSUE_UNROLL = 8

HIGHEST = lax.Precision.HIGHEST
_NT = (((1,), (1,)), ((), ()))


def _cparams(sem):
    return pltpu.CompilerParams(dimension_semantics=sem, vmem_limit_bytes=VMEM_LIMIT_BYTES)


def _pack_halves(a, b):
    ua = lax.bitcast_convert_type(a.astype(BF16).astype(F32), U32)
    ub = lax.bitcast_convert_type(b.astype(BF16).astype(F32), U32)
    return ua | (ub >> 16)


def _unpack_halves(w):
    a = lax.bitcast_convert_type(w & jnp.uint32(0xFFFF0000), F32)
    b = lax.bitcast_convert_type(w << 16, F32)
    return a, b


def _rms_scale(x):
    return lax.rsqrt(jnp.mean(x * x, axis=-1, keepdims=True) + RMS_EPS)


def _ada_body(c_ref, w_ref, b_ref, o_ref):
    c = c_ref[...]
    cond = c / (1.0 + jnp.exp(-c))
    o_ref[0] = jnp.dot(cond, w_ref[0], preferred_element_type=F32, precision=HIGHEST) + b_ref[0]


def ada_modulation(c, w_ada, b_ada):
    depth, d, n = w_ada.shape
    bsz = c.shape[0]
    tn = 1536
    assert n % tn == 0
    return pl.pallas_call(
        _ada_body,
        grid=(depth, n // tn),
        in_specs=[
            pl.BlockSpec((bsz, d), lambda l, j: (0, 0)),
            pl.BlockSpec((1, d, tn), lambda l, j: (l, 0, j)),
            pl.BlockSpec((1, 1, tn), lambda l, j: (l, 0, j)),
        ],
        out_specs=pl.BlockSpec((1, bsz, tn), lambda l, j: (l, 0, j)),
        out_shape=jax.ShapeDtypeStruct((depth, bsz, n), F32),
        compiler_params=_cparams(("parallel", "parallel")),
        name="ada_modulation",
    )(c, w_ada, b_ada.reshape(depth, 1, n))


def _project_rows(x, sc, sh, g, w_ref, o_ref, n_ref, norm_cols):
    h = x * _rms_scale(x) * g
    h = h * (1.0 + sc) + sh
    out = jnp.dot(h.astype(BF16), w_ref[...], preferred_element_type=F32).astype(BF16)
    o_ref[...] = out
    lane = lax.broadcasted_iota(I32, (1, len(norm_cols)), 1)
    acc = jnp.zeros((1, len(norm_cols)), F32)
    for j, c0 in enumerate(norm_cols):
        v = out[:, c0:c0 + FLASH_DV].astype(F32)
        n2 = jnp.max(jnp.sum(v * v, axis=1, keepdims=True), axis=0, keepdims=True)
        acc = jnp.where(lane == j, jnp.sqrt(n2), acc)
    n_ref[0] = acc


def _inproj_body(x_ref, sc_ref, sh_ref, g_ref, w_ref, o_ref, n_ref, *, norm_cols):
    _project_rows(x_ref[...], sc_ref[0], sh_ref[0], g_ref[...], w_ref, o_ref, n_ref, norm_cols)


def in_projection(x2d, scale, shift, gain, w_bf16, seq, norm_cols):
    t, d = x2d.shape
    n = w_bf16.shape[1]
    tm = min(TM_PROJ, seq)
    assert seq % tm == 0 and t % tm == 0
    mod_spec = pl.BlockSpec((1, 1, d), lambda i: ((i * tm) // seq, 0, 0))
    return pl.pallas_call(
        functools.partial(_inproj_body, norm_cols=norm_cols),
        grid=(t // tm,),
        in_specs=[
            pl.BlockSpec((tm, d), lambda i: (i, 0)),
            mod_spec, mod_spec,
            pl.BlockSpec((1, d), lambda i: (0, 0)),
            pl.BlockSpec((d, n), lambda i: (0, 0)),
        ],
        out_specs=[pl.BlockSpec((tm, n), lambda i: (i, 0)),
                   pl.BlockSpec((1, 1, len(norm_cols)), lambda i: (i, 0, 0))],
        out_shape=[jax.ShapeDtypeStruct((t, n), BF16),
                   jax.ShapeDtypeStruct((t // tm, 1, len(norm_cols)), F32)],
        compiler_params=_cparams(("parallel",)),
        name="in_projection",
    )(x2d, scale, shift, gain, w_bf16)


def _offset_bias_body(c_ref, slope_ref, o_ref, *, pairs):
    vi = pl.program_id(0)
    hd = pl.program_id(1)
    tq, win = o_ref.shape[2], o_ref.shape[3]
    off = (lax.broadcasted_iota(I32, (tq, win), 1) - lax.broadcasted_iota(I32, (tq, win), 0)) + c_ref[vi]
    dist = jnp.abs(off)
    mult = jnp.zeros((tq, win), F32)
    for window, dil in pairs:
        hit = jnp.logical_and(dist <= window // 2, (dist & (dil - 1)) == 0)
        mult = mult + jnp.where(hit, 1.0, 0.0)
    o_ref[0, 0] = jnp.log(mult) - slope_ref[hd] * dist.astype(F32)


def _window_variants(seq, tq, reach):
    win = tq + 2 * reach
    nq, edge = seq // tq, -(-reach // tq)
    assert seq % tq == 0 and reach % 16 == 0 and win <= seq and nq >= 2 * edge + 1
    rel = [min(max(i * tq - reach, 0), seq - win) - i * tq for i in range(nq)]
    assert all(r == -reach for r in rel[edge:nq - edge])
    starts = rel[:edge] + [-reach] + rel[nq - edge:]

    def variant(i):
        return jnp.where(i < edge, i, jnp.where(i >= nq - edge, i - (nq - 2 * edge - 1), edge))

    return win, starts, variant


def _window_body(c_ref, sink_ref, q_ref, k_ref, v_ref, tab_ref, o_ref, *, tq, win, variant):
    j = pl.program_id(1)
    i = pl.program_id(2)
    ks = pl.multiple_of(i * tq + c_ref[variant(i)], 16)
    q = q_ref[0]
    k = k_ref[0, pl.ds(ks, win), :]
    v = v_ref[0, pl.ds(ks, win), :]
    lane_lo = lax.broadcasted_iota(I32, (1, LANES), 1) < HEAD_DIM
    outs = []
    for hf in range(2):
        qm = jnp.where(lane_lo if hf == 0 else jnp.logical_not(lane_lo), q, jnp.zeros_like(q))
        s = lax.dot_general(qm, k, _NT, preferred_element_type=F32) + tab_ref[0, hf]
        sink = sink_ref[j, hf]
        m = jnp.maximum(jnp.max(s, axis=-1, keepdims=True), sink)
        p = jnp.exp(s - m)
        l = jnp.sum(p, axis=-1, keepdims=True) + jnp.exp(sink - m)
        outs.append(jnp.dot(p.astype(BF16), v, preferred_element_type=F32) / l)
    o_ref[0] = jnp.where(lane_lo, outs[0], outs[1]).astype(o_ref.dtype)


def window_attention(proj3d, q_blk0, k_blk0, v_blk0, kv_step, n_blk, slopes, sinks, pairs):
    bsz, seq, _ = proj3d.shape
    assert all(d & (d - 1) == 0 for _, d in pairs)
    reach = max(w // 2 for w, _ in pairs)
    tq = TQ_BAND
    win, starts, variant = _window_variants(seq, tq, reach)
    starts = jnp.array(starts, I32)
    smem = pl.BlockSpec(memory_space=pltpu.SMEM)
    tables = pl.pallas_call(
        functools.partial(_offset_bias_body, pairs=pairs),
        grid=(starts.shape[0], 2 * n_blk),
        in_specs=[smem, smem],
        out_specs=pl.BlockSpec((1, 1, tq, win), lambda vi, hd: (vi, hd, 0, 0)),
        out_shape=jax.ShapeDtypeStruct((starts.shape[0], 2 * n_blk, tq, win), F32),
        compiler_params=_cparams(("parallel", "parallel")),
        name="offset_bias_tables",
    )(starts, slopes)
    return pl.pallas_call(
        functools.partial(_window_body, tq=tq, win=win, variant=variant),
        grid=(bsz, n_blk, seq // tq),
        in_specs=[
            smem, smem,
            pl.BlockSpec((1, tq, LANES), lambda b, j, i: (b, i, q_blk0 + j)),
            pl.BlockSpec((1, seq, LANES), lambda b, j, i: (b, 0, k_blk0 + kv_step * j)),
            pl.BlockSpec((1, seq, LANES), lambda b, j, i: (b, 0, v_blk0 + kv_step * j)),
            pl.BlockSpec((1, 2, tq, win), lambda b, j, i: (variant(i), j, 0, 0)),
        ],
        out_specs=pl.BlockSpec((1, tq, LANES), lambda b, j, i: (b, i, j)),
        out_shape=jax.ShapeDtypeStruct((bsz, seq, n_blk * LANES), BF16),
        compiler_params=_cparams(("parallel", "parallel", "parallel")),
        name="window_attention",
    )(starts, sinks, proj3d, proj3d, proj3d, tables)


def _flash_body(kte_ref, skip_ref, slope_ref, lami_ref, qn_ref, kn_ref, qt_ref, k_ref, kaug_ref, vt_ref, diag_ref,
                lq1_ref, lk1_ref, lq2_ref, lk2_ref, sub_ref, o_ref, m_s, acc_s, s_scr, mmin_s, *, t, step_id):
    b = pl.program_id(0)
    h = pl.program_id(1)
    qi = pl.program_id(2)
    ki = pl.program_id(3)
    nt = pl.num_programs(3)
    kt = lax.rem(qi + ki, nt)

    @pl.when(ki == 0)
    def _():
        m_s[...] = jnp.full(m_s.shape, -jnp.inf, F32)
        acc_s[...] = jnp.zeros(acc_s.shape, F32)
        mmin_s[0] = jnp.float32(0.0)

    slope = slope_ref[h]
    n_chunk = t // FLASH_CHUNK
    n_panel = t // FLASH_PANEL
    sub_per_panel = FLASH_PANEL // FLASH_CHUNK
    row = lax.broadcasted_iota(I32, (FLASH_DV, 1), 0)
    row_lo = row < HEAD_DIM
    row_aug = lax.broadcasted_iota(I32, (FLASH_ONES, 1), 0) < FLASH_AUG
    lane_pos = lax.broadcasted_iota(I32, (1, FLASH_CHUNK), 1).astype(F32)
    ones_rows = jnp.ones((FLASH_ONES, FLASH_CHUNK), BF16)

    def with_sign(qtm, sign):
        aug = jnp.where(row_aug, sign, 0.0).astype(BF16)
        pad = jnp.zeros((FLASH_DV - FLASH_ONES, sign.shape[1]), BF16)
        return jnp.concatenate([qtm, aug, pad], axis=0)

    def keys(kc):
        rows = slice(kc * FLASH_CHUNK, (kc + 1) * FLASH_CHUNK)
        return jnp.concatenate([k_ref[0, rows, :], kaug_ref[0, rows, :]], axis=1)

    def values(kc):
        return jnp.concatenate([vt_ref[0, 0, :, kc * FLASH_CHUNK:(kc + 1) * FLASH_CHUNK], ones_rows], axis=0)
    blocks = [(qp, mp) for qp in range(n_panel) for mp in range(2)]

    def fold8(x, op):
        return functools.reduce(op, [x[r:r + 8] for r in range(0, x.shape[0], 8)])

    def pair_terms(qp, kc):
        signs, shifts = [], []
        for j in range(sub_per_panel):
            gap = (qi * n_chunk + qp * sub_per_panel + j) - (kt * n_chunk + kc)
            sign = jnp.sign(gap).astype(F32)
            signs.append(jnp.full((1, FLASH_CHUNK), sign, F32))
            shifts.append(-slope * ((jnp.abs(gap) * FLASH_CHUNK).astype(F32) + sign * lane_pos))
        return jnp.concatenate(signs, axis=1), jnp.concatenate(shifts, axis=1)

    def masked_queries(bi):
        qp, mp = blocks[bi]
        qt = qt_ref[0, 0, :, qp * FLASH_PANEL:(qp + 1) * FLASH_PANEL]
        return jnp.where(row_lo if mp == 0 else jnp.logical_not(row_lo), qt, jnp.zeros_like(qt))

    def score_chunk(bi, kc, qtm, mx):
        qp, _ = blocks[bi]
        k0 = kc * FLASH_CHUNK
        sign, lane_shift = pair_terms(qp, kc)
        s = jnp.dot(keys(kc), with_sign(qtm, sign), preferred_element_type=F32)
        j = kc - qp * sub_per_panel
        if 0 <= j < sub_per_panel:
            cols = [s[:, i * FLASH_CHUNK:(i + 1) * FLASH_CHUNK] for i in range(sub_per_panel)]
            cols[j] = cols[j] + diag_ref[0, 0]
            s = jnp.concatenate(cols, axis=1)
        s_scr[bi % 2, k0:k0 + FLASH_CHUNK, :] = s
        cm = fold8(s, jnp.maximum) + lane_shift
        return cm if mx is None else jnp.maximum(mx, cm)

    def prob_chunk(bi, kc, m_new, acc):
        qp, _ = blocks[bi]
        k0 = kc * FLASH_CHUNK
        shift = m_new - pair_terms(qp, kc)[1]
        parts = []
        for r0 in range(k0, k0 + FLASH_CHUNK, FLASH_SUB):
            parts.append(jnp.exp2((s_scr[bi % 2, r0:r0 + FLASH_SUB, :] - shift).astype(BF16)))
        pv = jnp.dot(values(kc), jnp.concatenate(parts, axis=0), preferred_element_type=F32)
        return pv if acc is None else acc + pv

    def two_pass_tile():
        qtm_next = masked_queries(0)
        mx_next = None
        for kc in range(n_chunk):
            mx_next = score_chunk(0, kc, qtm_next, mx_next)
        for bi, (qp, mp) in enumerate(blocks):
            qs = slice(qp * FLASH_PANEL, (qp + 1) * FLASH_PANEL)
            m_old = m_s[mp, :, qs]
            m_new = jnp.maximum(m_old, jnp.max(mx_next, axis=0, keepdims=True))
            alpha = jnp.exp2(m_old - m_new)
            acc = None
            has_next = bi + 1 < len(blocks)
            if has_next:
                qtm_next = masked_queries(bi + 1)
            mx_next = None
            for kc in range(n_chunk):
                if has_next:
                    mx_next = score_chunk(bi + 1, kc, qtm_next, mx_next)
                acc = prob_chunk(bi, kc, m_new, acc)
            acc_s[mp, :, qs] = alpha * acc_s[mp, :, qs] + acc
            m_s[mp, :, qs] = m_new
        mmin_s[0] = jnp.min(m_s[...])

    def one_pass_tile():
        items = [(bi, kc) for bi in range(len(blocks)) for kc in range(n_chunk)]

        def scores(bi, kc):
            qp, mp = blocks[bi]
            sign, lane_shift = pair_terms(qp, kc)
            s = jnp.dot(keys(kc), with_sign(masked_queries(bi), sign), preferred_element_type=F32)
            return s, m_s[mp, :, qp * FLASH_PANEL:(qp + 1) * FLASH_PANEL] - lane_shift

        ahead = [scores(*items[i]) for i in range(FLASH_LOOKAHEAD)]
        acc = None
        for n, (bi, kc) in enumerate(items):
            s, shift = ahead.pop(0)
            if n + FLASH_LOOKAHEAD < len(items):
                ahead.append(scores(*items[n + FLASH_LOOKAHEAD]))
            p = jnp.exp2(s - shift).astype(BF16)
            pv = jnp.dot(values(kc), p, preferred_element_type=F32)
            acc = pv if kc == 0 else acc + pv
            if kc == n_chunk - 1:
                qp, mp = blocks[bi]
                qs = slice(qp * FLASH_PANEL, (qp + 1) * FLASH_PANEL)
                acc_s[mp, :, qs] = acc_s[mp, :, qs] + acc

    bound = qn_ref[(b * pl.num_programs(1) + h) * nt + qi] * kn_ref[(b * pl.num_programs(1) + h) * nt + kt]
    min_dist = jnp.maximum((jnp.abs(qi - kt) - 1) * t + 1, 0).astype(F32)
    headroom = bound - mmin_s[0]
    all_zero = jnp.logical_and(ki > 0, headroom - slope * min_dist <= -FLASH_ZERO_LOG2)
    all_zero = jnp.logical_or(all_zero, skip_ref[step_id(b, h, qi, ki)] == 1)
    safe = jnp.logical_and(ki > 0, headroom <= FLASH_SAFE_LOG2)
    pl.when(jnp.logical_and(safe, jnp.logical_not(all_zero)))(one_pass_tile)
    pl.when(jnp.logical_not(jnp.logical_or(safe, all_zero)))(two_pass_tile)

    @pl.when(ki == pl.num_programs(3) - 1)
    def _():
        lam_init = lami_ref[0]
        lam = (jnp.exp(jnp.sum(lq1_ref[...] * lk1_ref[...], axis=-1, keepdims=True))
               - jnp.exp(jnp.sum(lq2_ref[...] * lk2_ref[...], axis=-1, keepdims=True)) + lam_init)
        o1 = acc_s[0, :FLASH_DV, :] / acc_s[0, FLASH_DV:FLASH_DV + 1, :]
        o2 = acc_s[1, :FLASH_DV, :] / acc_s[1, FLASH_DV:FLASH_DV + 1, :]
        o = o1 - lam * o2
        r = lax.rsqrt(jnp.mean(o * o, axis=0, keepdims=True) + RMS_EPS)
        y = o * r * sub_ref[...] * (1.0 - lam_init)
        o_ref[0] = y.T.astype(o_ref.dtype)


def _split_bf16(x, n):
    parts = []
    for _ in range(n):
        p = x.astype(BF16)
        parts.append(p)
        x = x - p.astype(F32)
    return parts


def alibi_key_columns(slopes, t):
    pos = (jnp.arange(t) % FLASH_CHUNK).astype(F32)
    aug = jnp.stack(_split_bf16(slopes[:, None] * pos[None, :], FLASH_AUG), axis=-1)
    return jnp.concatenate([aug, jnp.zeros((slopes.shape[0], t, FLASH_DV - FLASH_AUG), BF16)], axis=-1)


def diff_attention(qt, proj3d, k_blk0, vt, norms, slopes, lam_init, lq1, lk1, lq2, lk2, subln):
    bsz, nh, _, seq = qt.shape
    t = min(T_FLASH, seq)
    assert seq % t == 0 and t % FLASH_PANEL == 0 and FLASH_PANEL % FLASH_CHUNK == 0
    nt = seq // t
    norms = norms.reshape(bsz, nt, 2, nh).transpose(2, 0, 3, 1)
    qn, kn = norms[0], norms[1]
    tile = jnp.arange(nt)
    kt = (tile[:, None] + tile[None, :]) % nt
    min_dist = jnp.maximum((jnp.abs(tile[:, None] - kt) - 1) * t + 1, 0).astype(F32)
    upper = qn[..., :, None] * jnp.take(kn, kt, axis=-1) - slopes[None, :, None, None] * min_dist
    lower_ref = -(qn * kn)[..., :, None]
    skip = jnp.logical_and(upper - lower_ref <= -FLASH_ZERO_LOG2, kt != tile[:, None])
    kt_cols = [jnp.broadcast_to(kt[:, 0], skip.shape[:-1])]
    for ki in range(1, nt):
        kt_cols.append(jnp.where(skip[..., ki], kt_cols[-1], kt[:, ki]))
    kt_eff = jnp.stack(kt_cols, axis=-1).astype(I32).reshape(-1)
    skip = skip.astype(I32).reshape(-1)

    def step_id(b, h, qi, ki):
        return ((b * nh + h) * nt + qi) * nt + ki

    k_cols = alibi_key_columns(slopes, t)
    dv = FLASH_DV
    dva = FLASH_DV + FLASH_ONES
    r = jnp.arange(FLASH_CHUNK, dtype=F32)
    diag = -slopes[:, None, None] * jnp.abs(r[None, :] - r[:, None])[None]
    diag = jnp.stack([jnp.zeros_like(diag), diag], axis=1)
    smem = pl.BlockSpec(memory_space=pltpu.SMEM)
    vec = lambda n: pl.BlockSpec((1, n), lambda b, h, qi, ki, kte, sk: (0, 0))
    grid_spec = pltpu.PrefetchScalarGridSpec(
        num_scalar_prefetch=2,
        grid=(bsz, nh, nt, nt),
        in_specs=[
            smem, smem, smem, smem,
            pl.BlockSpec((1, 1, dv, t), lambda b, h, qi, ki, kte, sk: (b, h, 0, qi)),
            pl.BlockSpec((1, t, dv), lambda b, h, qi, ki, kte, sk: (b, kte[step_id(b, h, qi, ki)], k_blk0 + h)),
            pl.BlockSpec((1, t, dv), lambda b, h, qi, ki, kte, sk: (h, 0, 0)),
            pl.BlockSpec((1, 1, dv, t), lambda b, h, qi, ki, kte, sk: (b, h, 0, kte[step_id(b, h, qi, ki)])),
            pl.BlockSpec((1, 1, FLASH_CHUNK, FLASH_CHUNK),
                         lambda b, h, qi, ki, kte, sk: (h, (ki == 0).astype(I32), 0, 0)),
            vec(HEAD_DIM), vec(HEAD_DIM), vec(HEAD_DIM), vec(HEAD_DIM),
            pl.BlockSpec((dv, 1), lambda b, h, qi, ki, kte, sk: (0, 0)),
        ],
        out_specs=pl.BlockSpec((1, t, LANES), lambda b, h, qi, ki, kte, sk: (b, qi, h)),
        scratch_shapes=[
            pltpu.VMEM((2, 1, t), F32),
            pltpu.VMEM((2, dva, t), F32),
            pltpu.VMEM((2, t, FLASH_PANEL), F32),
            pltpu.SMEM((1,), F32),
        ],
    )
    return pl.pallas_call(
        functools.partial(_flash_body, t=t, step_id=step_id),
        grid_spec=grid_spec,
        out_shape=jax.ShapeDtypeStruct((bsz, seq, nh * dv), BF16),
        compiler_params=_cparams(("parallel", "parallel", "parallel", "arbitrary")),
        name="diff_attention",
    )(kt_eff, skip, slopes, lam_init, qn.reshape(-1), kn.reshape(-1), qt, proj3d, k_cols, vt, diag,
      lq1, lk1, lq2, lk2, subln)


def _outproj_body(x_ref, oa_ref, ob_ref, oc_ref, wa_ref, wb_ref, wc_ref, g1_ref, sc_ref, sh_ref,
                  n2_ref, wr_ref, br_ref, x1_ref, hp_ref, lg_ref):
    mix = (jnp.dot(oa_ref[...], wa_ref[...], preferred_element_type=F32)
           + jnp.dot(ob_ref[...], wb_ref[...], preferred_element_type=F32)
           + jnp.dot(oc_ref[...], wc_ref[...], preferred_element_type=F32))
    x1 = x_ref[...] + g1_ref[0] * mix
    x1_ref[...] = x1
    h = x1 * _rms_scale(x1) * n2_ref[...]
    h = h * (1.0 + sc_ref[0]) + sh_ref[0]
    half = h.shape[1] // 2
    hp_ref[...] = _pack_halves(h[:, :half], h[:, half:])
    lg_ref[...] = lax.dot_general(wr_ref[...], h, _NT, preferred_element_type=F32,
                                  precision=HIGHEST) + br_ref[...]


def out_projection(x2d, oa, ob, oc, wa, wb, wc, gate1, scale2, shift2, gain2, w_router_t, b_router, seq):
    t, d = x2d.shape
    e = w_router_t.shape[0]
    tm = min(TM_PROJ, seq)
    mod_spec = pl.BlockSpec((1, 1, d), lambda i: ((i * tm) // seq, 0, 0))
    row = lambda w: pl.BlockSpec((tm, w), lambda i: (i, 0))
    full = lambda a: pl.BlockSpec(a.shape, lambda i: (0, 0))
    return pl.pallas_call(
        _outproj_body,
        grid=(t // tm,),
        in_specs=[row(d), row(oa.shape[1]), row(ob.shape[1]), row(oc.shape[1]),
                  full(wa), full(wb), full(wc), mod_spec, mod_spec, mod_spec,
                  pl.BlockSpec((1, d), lambda i: (0, 0)), full(w_router_t),
                  pl.BlockSpec((e, 1), lambda i: (0, 0))],
        out_specs=[row(d), row(d // 2), pl.BlockSpec((e, tm), lambda i: (0, i))],
        out_shape=[jax.ShapeDtypeStruct((t, d), F32),
                   jax.ShapeDtypeStruct((t, d // 2), U32),
                   jax.ShapeDtypeStruct((e, t), F32)],
        compiler_params=_cparams(("parallel",)),
        name="out_projection",
    )(x2d, oa, ob, oc, wa, wb, wc, gate1, scale2, shift2, gain2, w_router_t, b_router)


def _router_body(lg_ref, tri_ref, eidx_ref, gate_ref, rank_ref, cnt_ref, base_s):
    step = pl.program_id(0)

    @pl.when(step == 0)
    def _():
        base_s[...] = jnp.zeros(base_s.shape, F32)

    work = lg_ref[...]
    n_e, tm = work.shape
    eio = lax.broadcasted_iota(I32, (n_e, tm), 0).astype(F32)
    vals, hots = [], []
    for k in range(TOP_K):
        mx = jnp.max(work, axis=0, keepdims=True)
        idx = jnp.min(jnp.where(work == mx, eio, float(n_e)), axis=0, keepdims=True)
        hot = eio == idx
        vals.append(mx)
        hots.append(hot)
        eidx_ref[k:k + 1, :] = idx.astype(I32)
        work = jnp.where(hot, -jnp.inf, work)
    exps = [jnp.exp(v - vals[0]) for v in vals]
    den = functools.reduce(lambda a, b: a + b, exps)
    for k in range(TOP_K):
        gate_ref[k:k + 1, :] = exps[k] / den
    any_hot = functools.reduce(lambda a, b: a + b, [h.astype(F32) for h in hots])
    prefix = jnp.dot(any_hot.astype(BF16), tri_ref[...], preferred_element_type=F32)
    pos = base_s[...] + prefix
    for k in range(TOP_K):
        rank_ref[k:k + 1, :] = jnp.sum(jnp.where(hots[k], pos, 0.0), axis=0, keepdims=True).astype(I32)
    base_s[...] = base_s[...] + jnp.sum(any_hot, axis=1, keepdims=True)
    cnt_ref[...] = base_s[...].astype(I32)


def route(logits_t):
    n_e, t = logits_t.shape
    tm = min(TM_ROUTE, t)
    assert t % tm == 0
    r = jnp.arange(tm)
    tri = (r[:, None] < r[None, :]).astype(BF16)
    out4 = lambda dt: jax.ShapeDtypeStruct((TOP_K, t), dt)
    spec4 = pl.BlockSpec((TOP_K, tm), lambda i: (0, i))
    return pl.pallas_call(
        _router_body,
        grid=(t // tm,),
        in_specs=[pl.BlockSpec((n_e, tm), lambda i: (0, i)),
                  pl.BlockSpec((tm, tm), lambda i: (0, 0))],
        out_specs=[spec4, spec4, spec4, pl.BlockSpec((n_e, 1), lambda i: (0, 0))],
        out_shape=[out4(I32), out4(F32), out4(I32), jax.ShapeDtypeStruct((n_e, 1), I32)],
        scratch_shapes=[pltpu.VMEM((n_e, 1), F32)],
        compiler_params=_cparams(("arbitrary",)),
        name="route",
    )(logits_t, tri)


def _dispatch_body(dest_ref, h_ref, init_ref, xs_ref, sem, *, tm):
    del init_ref

    def issue(tok, carry):
        for k in range(TOP_K):
            pltpu.make_async_copy(h_ref.at[pl.ds(tok, 1)], xs_ref.at[pl.ds(dest_ref[k, tok], 1)], sem).start()
        return carry

    lax.fori_loop(0, tm, issue, 0, unroll=DMA_ISSUE_UNROLL)
    for k in range(TOP_K):
        pltpu.make_async_copy(h_ref, xs_ref.at[pl.ds(0, tm)], sem).wait()


def dispatch_rows(h_packed, dest, n_rows):
    t, w = h_packed.shape
    tm = min(TM_DISPATCH, t)
    return pl.pallas_call(
        functools.partial(_dispatch_body, tm=tm),
        grid=(t // tm,),
        in_specs=[pl.BlockSpec((TOP_K, tm), lambda i: (0, i), memory_space=pltpu.SMEM),
                  pl.BlockSpec((tm, w), lambda i: (i, 0)),
                  pl.BlockSpec(memory_space=pl.ANY)],
        out_specs=pl.BlockSpec(memory_space=pl.ANY),
        out_shape=jax.ShapeDtypeStruct((n_rows, w), U32),
        input_output_aliases={2: 0},
        scratch_shapes=[pltpu.SemaphoreType.DMA],
        compiler_params=_cparams(("arbitrary",)),
        name="dispatch_rows",
    )(dest, h_packed, jnp.zeros((n_rows, w), U32))


def _ffn_body(be_ref, bv_ref, nu_ref, xs_ref, wg_ref, bg_ref, wu_ref, bu_ref, wd_ref, bd_ref, o_ref,
              wg_s, wu_s, wd_s):
    i = pl.program_id(0)

    @pl.when(i < nu_ref[0])
    def _():
        @pl.when(jnp.logical_or(i == 0, be_ref[i] != be_ref[jnp.maximum(i - 1, 0)]))
        def _():
            wg_s[...] = wg_ref[0, 0].astype(BF16)
            wu_s[...] = wu_ref[0, 0].astype(BF16)
            wd_s[...] = wd_ref[0, 0].astype(BF16)

        xa, xb = _unpack_halves(xs_ref[...])
        blk, half = xa.shape
        valid = lax.broadcasted_iota(I32, (blk, 1), 0) < bv_ref[i]
        xa = jnp.where(valid, xa, 0.0).astype(BF16)
        xb = jnp.where(valid, xb, 0.0).astype(BF16)

        def proj(w_s, b_ref):
            return (jnp.dot(xa, w_s[:half, :], preferred_element_type=F32)
                    + jnp.dot(xb, w_s[half:, :], preferred_element_type=F32) + b_ref[0, 0])

        a = jnp.minimum(proj(wg_s, bg_ref), SWIGLU_LIMIT)
        u = jnp.clip(proj(wu_s, bu_ref), -SWIGLU_LIMIT, SWIGLU_LIMIT)
        hid = a * (1.0 / (1.0 + jnp.exp(-SWIGLU_ALPHA * a))) * (u + 1.0)
        out = jnp.dot(hid.astype(BF16), wd_s[...], preferred_element_type=F32) + bd_ref[0, 0]
        d = out.shape[1] // 2
        o_ref[...] = _pack_halves(out[:, :d], out[:, d:])

    @pl.when(i >= nu_ref[0])
    def _():
        o_ref[...] = jnp.zeros(o_ref.shape, o_ref.dtype)


def expert_ffn(xs, blk_expert, blk_valid, n_used, layer, wg, bg, wu, bu, wd, bd):
    n_rows, w = xs.shape
    _, n_e, d, f = wg.shape
    nb = n_rows // BLK_FFN
    row_map = lambda i, be, bv, nu: (jnp.minimum(i, nu[0] - 1), 0)
    exp_map = lambda i, be, bv, nu: (layer, be[i], 0, 0)
    grid_spec = pltpu.PrefetchScalarGridSpec(
        num_scalar_prefetch=3,
        grid=(nb,),
        in_specs=[pl.BlockSpec((BLK_FFN, w), row_map),
                  pl.BlockSpec((1, 1, d, f), exp_map), pl.BlockSpec((1, 1, 1, f), exp_map),
                  pl.BlockSpec((1, 1, d, f), exp_map), pl.BlockSpec((1, 1, 1, f), exp_map),
                  pl.BlockSpec((1, 1, f, d), exp_map), pl.BlockSpec((1, 1, 1, d), exp_map)],
        out_specs=pl.BlockSpec((BLK_FFN, d // 2), lambda i, be, bv, nu: (i, 0)),
        scratch_shapes=[pltpu.VMEM((d, f), BF16), pltpu.VMEM((d, f), BF16), pltpu.VMEM((f, d), BF16)],
    )
    return pl.pallas_call(
        _ffn_body,
        grid_spec=grid_spec,
        out_shape=jax.ShapeDtypeStruct((n_rows, d // 2), U32),
        compiler_params=_cparams(("arbitrary",)),
        name="expert_ffn",
    )(blk_expert, blk_valid, n_used, xs, wg, bg, wu, bu, wd, bd)


def _combine_body(dest_ref, x_ref, gt_ref, g2_ref, nf_ref, ys_ref, o_ref, buf, sem, *, tm, final):
    def issue(tok, carry):
        for k in range(TOP_K):
            pltpu.make_async_copy(ys_ref.at[pl.ds(dest_ref[k, tok], 1)], buf.at[k, pl.ds(tok, 1)], sem).start()
        return carry

    lax.fori_loop(0, tm, issue, 0, unroll=DMA_ISSUE_UNROLL)
    for k in range(TOP_K):
        pltpu.make_async_copy(ys_ref.at[pl.ds(0, tm)], buf.at[k], sem).wait()
    gt = gt_ref[...]
    lo = hi = None
    for k in range(TOP_K):
        a, b = _unpack_halves(buf[k])
        g = gt[:, k:k + 1]
        lo = g * a if lo is None else lo + g * a
        hi = g * b if hi is None else hi + g * b
    half = lo.shape[1]
    g2 = g2_ref[0]
    x_lo = x_ref[:, :half] + g2[:, :half] * lo
    x_hi = x_ref[:, half:] + g2[:, half:] * hi
    if final:
        ms = (jnp.sum(x_lo * x_lo, axis=-1, keepdims=True)
              + jnp.sum(x_hi * x_hi, axis=-1, keepdims=True)) / (2 * half)
        r = lax.rsqrt(ms + RMS_EPS)
        x_lo = x_lo * r * nf_ref[:, :half]
        x_hi = x_hi * r * nf_ref[:, half:]
    o_ref[:, :half] = x_lo
    o_ref[:, half:] = x_hi


def combine(x1, ys, dest, gates_t, gate2, norm_f, seq, final):
    t, d = x1.shape
    tm = min(TM_COMBINE, seq)
    return pl.pallas_call(
        functools.partial(_combine_body, tm=tm, final=final),
        grid=(t // tm,),
        in_specs=[pl.BlockSpec((TOP_K, tm), lambda i: (0, i), memory_space=pltpu.SMEM),
                  pl.BlockSpec((tm, d), lambda i: (i, 0)),
                  pl.BlockSpec((tm, TOP_K), lambda i: (i, 0)),
                  pl.BlockSpec((1, 1, d), lambda i: ((i * tm) // seq, 0, 0)),
                  pl.BlockSpec((1, d), lambda i: (0, 0)),
                  pl.BlockSpec(memory_space=pl.ANY)],
        out_specs=pl.BlockSpec((tm, d), lambda i: (i, 0)),
        out_shape=jax.ShapeDtypeStruct((t, d), F32),
        scratch_shapes=[pltpu.VMEM((TOP_K, tm, d // 2), U32), pltpu.SemaphoreType.DMA],
        compiler_params=_cparams(("arbitrary",)),
        name="combine",
    )(dest, x1, gates_t, gate2, norm_f, ys)


def _alibi_slopes(n):
    return jnp.exp2(-8.0 * jnp.arange(1, n + 1, dtype=F32) / n)


def _routing_tables(counts, eidx, rank, nb):
    counts = counts.reshape(-1)
    padded = (counts + BLK_FFN - 1) // BLK_FFN * BLK_FFN
    pad_end = jnp.cumsum(padded)
    pad_start = pad_end - padded
    e_ids = jnp.arange(N_EXPERTS, dtype=I32)[:, None, None]
    dest = rank + jnp.sum(jnp.where(eidx[None] == e_ids, pad_start[:, None, None], 0), axis=0)
    blk_row0 = jnp.arange(nb, dtype=I32) * BLK_FFN
    blk_expert = jnp.minimum(jnp.sum(pad_end[None, :] <= blk_row0[:, None], axis=1), N_EXPERTS - 1).astype(I32)
    blk_valid = jnp.clip(counts[blk_expert] - (blk_row0 - pad_start[blk_expert]), 0, BLK_FFN).astype(I32)
    n_used = (pad_end[-1:] // BLK_FFN).astype(I32)
    return dest.astype(I32), blk_expert, blk_valid, n_used


def kernel(x, c, w_ada, b_ada, norm1, w_in, sinks, lambda_q1, lambda_k1, lambda_q2, lambda_k2, subln,
           w_out, norm2, w_router, b_router, w_gate, b_gate, w_up, b_up, w_down, b_down, norm_f):
    bsz, seq, d = x.shape
    depth = w_ada.shape[0]
    t = bsz * seq
    wa_w, wb_w = N_HEADS_A * HEAD_DIM, N_HEADS_B * HEAD_DIM
    kvb_w = N_KV_B * HEAD_DIM
    wc_w = N_HEADS_C * 2 * HEAD_DIM
    off = [0]
    for wdt in (wa_w, wa_w, wa_w, wb_w, kvb_w, kvb_w, wc_w, wc_w, wc_w):
        off.append(off[-1] + wdt)
    blk = [o // LANES for o in off]
    norm_cols = tuple(off[i] + hh * FLASH_DV for i in (6, 7) for hh in range(N_HEADS_C))

    group = N_HEADS_B // N_KV_B
    b_perm = [kv * group + g for g in range(group) for kv in range(N_KV_B)]
    col_perm = jnp.concatenate([jnp.arange(h * HEAD_DIM, (h + 1) * HEAD_DIM) for h in b_perm])
    q_scale = jnp.ones((off[-1],), F32)
    for s0, s1, extra in ((off[0], off[1], 1.0), (off[3], off[4], 1.0), (off[6], off[7], LOG2E)):
        q_scale = q_scale.at[s0:s1].set(HEAD_DIM ** -0.5 * extra)

    slopes_a = _alibi_slopes(N_HEADS_A)
    slopes_b = _alibi_slopes(N_HEADS_B)[jnp.array(b_perm)]
    slopes_c = _alibi_slopes(N_HEADS_C) * LOG2E

    mod_all = ada_modulation(c, w_ada, b_ada)
    x2d = x.reshape(t, d)
    nb = (t * TOP_K) // BLK_FFN + N_EXPERTS
    n_rows = nb * BLK_FFN

    for layer in range(depth):
        mod = mod_all[layer].reshape(bsz, N_MOD, 1, d)
        sh1, sc1, g1, sh2, sc2, g2 = [mod[:, i] for i in range(N_MOD)]

        w_l = w_in[layer] * q_scale[None, :]
        w_l = jnp.concatenate([w_l[:, :off[3]], w_l[:, off[3]:off[4]][:, col_perm], w_l[:, off[4]:]], axis=1)
        proj, blk_norms = in_projection(x2d, sc1, sh1, norm1[layer].reshape(1, d), w_l.astype(BF16), seq,
                                        norm_cols)
        proj3d = proj.reshape(bsz, seq, off[-1])

        n_blk_a = wa_w // LANES
        o_a = window_attention(proj3d, blk[0], blk[1], blk[2], 1, n_blk_a, slopes_a,
                               jnp.full((n_blk_a, 2), -jnp.inf, F32), DILATED_PAIRS).reshape(t, wa_w)

        sink_l = sinks[layer].astype(F32)[jnp.array(b_perm)].reshape(group, N_KV_B)
        o_b = window_attention(proj3d, blk[3], blk[4], blk[5], 0, group, slopes_b, sink_l,
                               ((2 * SWA_HALF, 1),)).reshape(t, wb_w)

        lam_init = jnp.full((1,), 0.8 - 0.6 * math.exp(-0.3 * layer), F32)
        heads_t = lambda i: (proj3d[:, :, off[i]:off[i + 1]].reshape(bsz, seq, N_HEADS_C, FLASH_DV)
                             .transpose(0, 2, 3, 1))
        t_flash = min(T_FLASH, seq)
        norms = blk_norms.reshape(bsz, seq // t_flash, -1, 2 * N_HEADS_C).max(axis=2, keepdims=True)
        o_c = diff_attention(heads_t(6), proj3d, blk[7], heads_t(8), norms, slopes_c, lam_init,
                             lambda_q1[layer].reshape(1, -1), lambda_k1[layer].reshape(1, -1),
                             lambda_q2[layer].reshape(1, -1), lambda_k2[layer].reshape(1, -1),
                             subln[layer].reshape(-1, 1)).reshape(t, wc_w)

        wo = w_out[layer]
        wo_b = wo[wa_w:wa_w + wb_w].reshape(N_HEADS_B, HEAD_DIM, d)[jnp.array(b_perm)].reshape(wb_w, d)
        x1, h_packed, logits_t = out_projection(
            x2d, o_a, o_b, o_c, wo[:wa_w].astype(BF16), wo_b.astype(BF16), wo[wa_w + wb_w:].astype(BF16),
            g1, sc2, sh2, norm2[layer].reshape(1, d), w_router[layer].T, b_router[layer].reshape(-1, 1), seq)

        eidx, gates, rank, counts = route(logits_t)
        dest, blk_expert, blk_valid, n_used = _routing_tables(counts, eidx, rank, nb)
        xs = dispatch_rows(h_packed, dest, n_rows)
        ys = expert_ffn(xs, blk_expert, blk_valid, n_used, layer,
                        w_gate, b_gate[:, :, None, :], w_up, b_up[:, :, None, :], w_down, b_down[:, :, None, :])
        x2d = combine(x1, ys, dest, gates.T, g2, norm_f.reshape(1, d), seq, layer == depth - 1)

    return x2d.reshape(bsz, seq, d)
```

```python
import functools
import math

import jax
import jax.numpy as jnp
from jax import lax
from jax.experimental import pallas as pl
from jax.experimental.pallas import tpu as pltpu

F32 = jnp.float32
BF16 = jnp.bfloat16
I32 = jnp.int32
U32 = jnp.uint32

HEAD_DIM = 64
DILATED_PAIRS = ((128, 1), (512, 4), (2048, 16))
N_HEADS_A = 4
N_HEADS_B = 4
N_KV_B = 2
SWA_HALF = 128
N_HEADS_C = 4
N_EXPERTS = 32
TOP_K = 4
SWIGLU_LIMIT = 7.0
SWIGLU_ALPHA = 1.702
RMS_EPS = 1e-6
N_MOD = 6

LANES = 128
VMEM_LIMIT_BYTES = 56 * 1024 * 1024

TM_PROJ = 512
TQ_BAND = 256
T_FLASH = 1024
FLASH_PANEL = 512
FLASH_CHUNK = 256
FLASH_SUB = 128
FLASH_DV = 2 * HEAD_DIM
FLASH_ONES = 16
FLASH_AUG = 3
FLASH_ZERO_LOG2 = 150.0
FLASH_LOOKAHEAD = 2
FLASH_SAFE_LOG2 = 60.0
LOG2E = math.log2(math.e)
TM_ROUTE = 512
TM_DISPATCH = 512
BLK_FFN = 512
TM_COMBINE = 256
DMA_ISSUE_UNROLL = 8

HIGHEST = lax.Precision.HIGHEST
_NT = (((1,), (1,)), ((), ()))


def _cparams(sem):
    return pltpu.CompilerParams(dimension_semantics=sem, vmem_limit_bytes=VMEM_LIMIT_BYTES)


def _pack_halves(a, b):
    ua = lax.bitcast_convert_type(a.astype(BF16).astype(F32), U32)
    ub = lax.bitcast_convert_type(b.astype(BF16).astype(F32), U32)
    return ua | (ub >> 16)


def _unpack_halves(w):
    a = lax.bitcast_convert_type(w & jnp.uint32(0xFFFF0000), F32)
    b = lax.bitcast_convert_type(w << 16, F32)
    return a, b


def _rms_scale(x):
    return lax.rsqrt(jnp.mean(x * x, axis=-1, keepdims=True) + RMS_EPS)


def _ada_body(c_ref, w_ref, b_ref, o_ref):
    c = c_ref[...]
    cond = c / (1.0 + jnp.exp(-c))
    o_ref[0] = jnp.dot(cond, w_ref[0], preferred_element_type=F32, precision=HIGHEST) + b_ref[0]


def ada_modulation(c, w_ada, b_ada):
    depth, d, n = w_ada.shape
    bsz = c.shape[0]
    tn = 1536
    assert n % tn == 0
    return pl.pallas_call(
        _ada_body,
        grid=(depth, n // tn),
        in_specs=[
            pl.BlockSpec((bsz, d), lambda l, j: (0, 0)),
            pl.BlockSpec((1, d, tn), lambda l, j: (l, 0, j)),
            pl.BlockSpec((1, 1, tn), lambda l, j: (l, 0, j)),
        ],
        out_specs=pl.BlockSpec((1, bsz, tn), lambda l, j: (l, 0, j)),
        out_shape=jax.ShapeDtypeStruct((depth, bsz, n), F32),
        compiler_params=_cparams(("parallel", "parallel")),
        name="ada_modulation",
    )(c, w_ada, b_ada.reshape(depth, 1, n))


def _project_rows(x, sc, sh, g, w_ref, o_ref, n_ref, norm_cols):
    h = x * _rms_scale(x) * g
    h = h * (1.0 + sc) + sh
    out = jnp.dot(h.astype(BF16), w_ref[...], preferred_element_type=F32).astype(BF16)
    o_ref[...] = out
    lane = lax.broadcasted_iota(I32, (1, len(norm_cols)), 1)
    acc = jnp.zeros((1, len(norm_cols)), F32)
    for j, c0 in enumerate(norm_cols):
        v = out[:, c0:c0 + FLASH_DV].astype(F32)
        n2 = jnp.max(jnp.sum(v * v, axis=1, keepdims=True), axis=0, keepdims=True)
        acc = jnp.where(lane == j, jnp.sqrt(n2), acc)
    n_ref[0] = acc


def _inproj_body(x_ref, sc_ref, sh_ref, g_ref, w_ref, o_ref, n_ref, *, norm_cols):
    _project_rows(x_ref[...], sc_ref[0], sh_ref[0], g_ref[...], w_ref, o_ref, n_ref, norm_cols)


def in_projection(x2d, scale, shift, gain, w_bf16, seq, norm_cols):
    t, d = x2d.shape
    n = w_bf16.shape[1]
    tm = min(TM_PROJ, seq)
    assert seq % tm == 0 and t % tm == 0
    mod_spec = pl.BlockSpec((1, 1, d), lambda i: ((i * tm) // seq, 0, 0))
    return pl.pallas_call(
        functools.partial(_inproj_body, norm_cols=norm_cols),
        grid=(t // tm,),
        in_specs=[
            pl.BlockSpec((tm, d), lambda i: (i, 0)),
            mod_spec, mod_spec,
            pl.BlockSpec((1, d), lambda i: (0, 0)),
            pl.BlockSpec((d, n), lambda i: (0, 0)),
        ],
        out_specs=[pl.BlockSpec((tm, n), lambda i: (i, 0)),
                   pl.BlockSpec((1, 1, len(norm_cols)), lambda i: (i, 0, 0))],
        out_shape=[jax.ShapeDtypeStruct((t, n), BF16),
                   jax.ShapeDtypeStruct((t // tm, 1, len(norm_cols)), F32)],
        compiler_params=_cparams(("parallel",)),
        name="in_projection",
    )(x2d, scale, shift, gain, w_bf16)


def _offset_bias_body(c_ref, slope_ref, o_ref, *, pairs):
    vi = pl.program_id(0)
    hd = pl.program_id(1)
    tq, win = o_ref.shape[2], o_ref.shape[3]
    off = (lax.broadcasted_iota(I32, (tq, win), 1) - lax.broadcasted_iota(I32, (tq, win), 0)) + c_ref[vi]
    dist = jnp.abs(off)
    mult = jnp.zeros((tq, win), F32)
    for window, dil in pairs:
        hit = jnp.logical_and(dist <= window // 2, (dist & (dil - 1)) == 0)
        mult = mult + jnp.where(hit, 1.0, 0.0)
    o_ref[0, 0] = jnp.log(mult) - slope_ref[hd] * dist.astype(F32)


def _window_variants(seq, tq, reach):
    win = tq + 2 * reach
    nq, edge = seq // tq, -(-reach // tq)
    assert seq % tq == 0 and reach % 16 == 0 and win <= seq and nq >= 2 * edge + 1
    rel = [min(max(i * tq - reach, 0), seq - win) - i * tq for i in range(nq)]
    assert all(r == -reach for r in rel[edge:nq - edge])
    starts = rel[:edge] + [-reach] + rel[nq - edge:]

    def variant(i):
        return jnp.where(i < edge, i, jnp.where(i >= nq - edge, i - (nq - 2 * edge - 1), edge))

    return win, starts, variant


def _window_body(c_ref, sink_ref, q_ref, k_ref, v_ref, tab_ref, o_ref, *, tq, win, variant):
    j = pl.program_id(1)
    i = pl.program_id(2)
    ks = pl.multiple_of(i * tq + c_ref[variant(i)], 16)
    q = q_ref[0]
    k = k_ref[0, pl.ds(ks, win), :]
    v = v_ref[0, pl.ds(ks, win), :]
    lane_lo = lax.broadcasted_iota(I32, (1, LANES), 1) < HEAD_DIM
    outs = []
    for hf in range(2):
        qm = jnp.where(lane_lo if hf == 0 else jnp.logical_not(lane_lo), q, jnp.zeros_like(q))
        s = lax.dot_general(qm, k, _NT, preferred_element_type=F32) + tab_ref[0, hf]
        sink = sink_ref[j, hf]
        m = jnp.maximum(jnp.max(s, axis=-1, keepdims=True), sink)
        p = jnp.exp(s - m)
        l = jnp.sum(p, axis=-1, keepdims=True) + jnp.exp(sink - m)
        outs.append(jnp.dot(p.astype(BF16), v, preferred_element_type=F32) / l)
    o_ref[0] = jnp.where(lane_lo, outs[0], outs[1]).astype(o_ref.dtype)


def window_attention(proj3d, q_blk0, k_blk0, v_blk0, kv_step, n_blk, slopes, sinks, pairs):
    bsz, seq, _ = proj3d.shape
    assert all(d & (d - 1) == 0 for _, d in pairs)
    reach = max(w // 2 for w, _ in pairs)
    tq = TQ_BAND
    win, starts, variant = _window_variants(seq, tq, reach)
    starts = jnp.array(starts, I32)
    smem = pl.BlockSpec(memory_space=pltpu.SMEM)
    tables = pl.pallas_call(
        functools.partial(_offset_bias_body, pairs=pairs),
        grid=(starts.shape[0], 2 * n_blk),
        in_specs=[smem, smem],
        out_specs=pl.BlockSpec((1, 1, tq, win), lambda vi, hd: (vi, hd, 0, 0)),
        out_shape=jax.ShapeDtypeStruct((starts.shape[0], 2 * n_blk, tq, win), F32),
        compiler_params=_cparams(("parallel", "parallel")),
        name="offset_bias_tables",
    )(starts, slopes)
    return pl.pallas_call(
        functools.partial(_window_body, tq=tq, win=win, variant=variant),
        grid=(bsz, n_blk, seq // tq),
        in_specs=[
            smem, smem,
            pl.BlockSpec((1, tq, LANES), lambda b, j, i: (b, i, q_blk0 + j)),
            pl.BlockSpec((1, seq, LANES), lambda b, j, i: (b, 0, k_blk0 + kv_step * j)),
            pl.BlockSpec((1, seq, LANES), lambda b, j, i: (b, 0, v_blk0 + kv_step * j)),
            pl.BlockSpec((1, 2, tq, win), lambda b, j, i: (variant(i), j, 0, 0)),
        ],
        out_specs=pl.BlockSpec((1, tq, LANES), lambda b, j, i: (b, i, j)),
        out_shape=jax.ShapeDtypeStruct((bsz, seq, n_blk * LANES), BF16),
        compiler_params=_cparams(("parallel", "parallel", "parallel")),
        name="window_attention",
    )(starts, sinks, proj3d, proj3d, proj3d, tables)


def _flash_body(kte_ref, skip_ref, slope_ref, lami_ref, qn_ref, kn_ref, qt_ref, k_ref, kaug_ref, vt_ref, diag_ref,
                lq1_ref, lk1_ref, lq2_ref, lk2_ref, sub_ref, o_ref, m_s, acc_s, s_scr, mmin_s, *, t, step_id):
    b = pl.program_id(0)
    h = pl.program_id(1)
    qi = pl.program_id(2)
    ki = pl.program_id(3)
    nt = pl.num_programs(3)
    kt = lax.rem(qi + ki, nt)

    @pl.when(ki == 0)
    def _():
        m_s[...] = jnp.full(m_s.shape, -jnp.inf, F32)
        acc_s[...] = jnp.zeros(acc_s.shape, F32)
        mmin_s[0] = jnp.float32(0.0)

    slope = slope_ref[h]
    n_chunk = t // FLASH_CHUNK
    n_panel = t // FLASH_PANEL
    sub_per_panel = FLASH_PANEL // FLASH_CHUNK
    row = lax.broadcasted_iota(I32, (FLASH_DV, 1), 0)
    row_lo = row < HEAD_DIM
    row_aug = lax.broadcasted_iota(I32, (FLASH_ONES, 1), 0) < FLASH_AUG
    lane_pos = lax.broadcasted_iota(I32, (1, FLASH_CHUNK), 1).astype(F32)
    ones_rows = jnp.ones((FLASH_ONES, FLASH_CHUNK), BF16)

    def with_sign(qtm, sign):
        aug = jnp.where(row_aug, sign, 0.0).astype(BF16)
        pad = jnp.zeros((FLASH_DV - FLASH_ONES, sign.shape[1]), BF16)
        return jnp.concatenate([qtm, aug, pad], axis=0)

    def keys(kc):
        rows = slice(kc * FLASH_CHUNK, (kc + 1) * FLASH_CHUNK)
        return jnp.concatenate([k_ref[0, rows, :], kaug_ref[0, rows, :]], axis=1)

    def values(kc):
        return jnp.concatenate([vt_ref[0, 0, :, kc * FLASH_CHUNK:(kc + 1) * FLASH_CHUNK], ones_rows], axis=0)
    blocks = [(qp, mp) for qp in range(n_panel) for mp in range(2)]

    def fold8(x, op):
        return functools.reduce(op, [x[r:r + 8] for r in range(0, x.shape[0], 8)])

    def pair_terms(qp, kc):
        signs, shifts = [], []
        for j in range(sub_per_panel):
            gap = (qi * n_chunk + qp * sub_per_panel + j) - (kt * n_chunk + kc)
            sign = jnp.sign(gap).astype(F32)
            signs.append(jnp.full((1, FLASH_CHUNK), sign, F32))
            shifts.append(-slope * ((jnp.abs(gap) * FLASH_CHUNK).astype(F32) + sign * lane_pos))
        return jnp.concatenate(signs, axis=1), jnp.concatenate(shifts, axis=1)

    def masked_queries(bi):
        qp, mp = blocks[bi]
        qt = qt_ref[0, 0, :, qp * FLASH_PANEL:(qp + 1) * FLASH_PANEL]
        return jnp.where(row_lo if mp == 0 else jnp.logical_not(row_lo), qt, jnp.zeros_like(qt))

    def score_chunk(bi, kc, qtm, mx):
        qp, _ = blocks[bi]
        k0 = kc * FLASH_CHUNK
        sign, lane_shift = pair_terms(qp, kc)
        s = jnp.dot(keys(kc), with_sign(qtm, sign), preferred_element_type=F32)
        j = kc - qp * sub_per_panel
        if 0 <= j < sub_per_panel:
            cols = [s[:, i * FLASH_CHUNK:(i + 1) * FLASH_CHUNK] for i in range(sub_per_panel)]
            cols[j] = cols[j] + diag_ref[0, 0]
            s = jnp.concatenate(cols, axis=1)
        s_scr[bi % 2, k0:k0 + FLASH_CHUNK, :] = s
        cm = fold8(s, jnp.maximum) + lane_shift
        return cm if mx is None else jnp.maximum(mx, cm)

    def prob_chunk(bi, kc, m_new, acc):
        qp, _ = blocks[bi]
        k0 = kc * FLASH_CHUNK
        shift = m_new - pair_terms(qp, kc)[1]
        parts = []
        for r0 in range(k0, k0 + FLASH_CHUNK, FLASH_SUB):
            parts.append(jnp.exp2((s_scr[bi % 2, r0:r0 + FLASH_SUB, :] - shift).astype(BF16)))
        pv = jnp.dot(values(kc), jnp.concatenate(parts, axis=0), preferred_element_type=F32)
        return pv if acc is None else acc + pv

    def two_pass_tile():
        qtm_next = masked_queries(0)
        mx_next = None
        for kc in range(n_chunk):
            mx_next = score_chunk(0, kc, qtm_next, mx_next)
        for bi, (qp, mp) in enumerate(blocks):
            qs = slice(qp * FLASH_PANEL, (qp + 1) * FLASH_PANEL)
            m_old = m_s[mp, :, qs]
            m_new = jnp.maximum(m_old, jnp.max(mx_next, axis=0, keepdims=True))
            alpha = jnp.exp2(m_old - m_new)
            acc = None
            has_next = bi + 1 < len(blocks)
            if has_next:
                qtm_next = masked_queries(bi + 1)
            mx_next = None
            for kc in range(n_chunk):
                if has_next:
                    mx_next = score_chunk(bi + 1, kc, qtm_next, mx_next)
                acc = prob_chunk(bi, kc, m_new, acc)
            acc_s[mp, :, qs] = alpha * acc_s[mp, :, qs] + acc
            m_s[mp, :, qs] = m_new
        mmin_s[0] = jnp.min(m_s[...])

    def one_pass_tile():
        items = [(bi, kc) for bi in range(len(blocks)) for kc in range(n_chunk)]

        def scores(bi, kc):
            qp, mp = blocks[bi]
            sign, lane_shift = pair_terms(qp, kc)
            s = jnp.dot(keys(kc), with_sign(masked_queries(bi), sign), preferred_element_type=F32)
            return s, m_s[mp, :, qp * FLASH_PANEL:(qp + 1) * FLASH_PANEL] - lane_shift

        ahead = [scores(*items[i]) for i in range(FLASH_LOOKAHEAD)]
        acc = None
        for n, (bi, kc) in enumerate(items):
            s, shift = ahead.pop(0)
            if n + FLASH_LOOKAHEAD < len(items):
                ahead.append(scores(*items[n + FLASH_LOOKAHEAD]))
            p = jnp.exp2(s - shift).astype(BF16)
            pv = jnp.dot(values(kc), p, preferred_element_type=F32)
            acc = pv if kc == 0 else acc + pv
            if kc == n_chunk - 1:
                qp, mp = blocks[bi]
                qs = slice(qp * FLASH_PANEL, (qp + 1) * FLASH_PANEL)
                acc_s[mp, :, qs] = acc_s[mp, :, qs] + acc

    bound = qn_ref[(b * pl.num_programs(1) + h) * nt + qi] * kn_ref[(b * pl.num_programs(1) + h) * nt + kt]
    min_dist = jnp.maximum((jnp.abs(qi - kt) - 1) * t + 1, 0).astype(F32)
    headroom = bound - mmin_s[0]
    all_zero = jnp.logical_and(ki > 0, headroom - slope * min_dist <= -FLASH_ZERO_LOG2)
    all_zero = jnp.logical_or(all_zero, skip_ref[step_id(b, h, qi, ki)] == 1)
    safe = jnp.logical_and(ki > 0, headroom <= FLASH_SAFE_LOG2)
    pl.when(jnp.logical_and(safe, jnp.logical_not(all_zero)))(one_pass_tile)
    pl.when(jnp.logical_not(jnp.logical_or(safe, all_zero)))(two_pass_tile)

    @pl.when(ki == pl.num_programs(3) - 1)
    def _():
        lam_init = lami_ref[0]
        lam = (jnp.exp(jnp.sum(lq1_ref[...] * lk1_ref[...], axis=-1, keepdims=True))
               - jnp.exp(jnp.sum(lq2_ref[...] * lk2_ref[...], axis=-1, keepdims=True)) + lam_init)
        o1 = acc_s[0, :FLASH_DV, :] / acc_s[0, FLASH_DV:FLASH_DV + 1, :]
        o2 = acc_s[1, :FLASH_DV, :] / acc_s[1, FLASH_DV:FLASH_DV + 1, :]
        o = o1 - lam * o2
        r = lax.rsqrt(jnp.mean(o * o, axis=0, keepdims=True) + RMS_EPS)
        y = o * r * sub_ref[...] * (1.0 - lam_init)
        o_ref[0] = y.T.astype(o_ref.dtype)


def _split_bf16(x, n):
    parts = []
    for _ in range(n):
        p = x.astype(BF16)
        parts.append(p)
        x = x - p.astype(F32)
    return parts


def alibi_key_columns(slopes, t):
    pos = (jnp.arange(t) % FLASH_CHUNK).astype(F32)
    aug = jnp.stack(_split_bf16(slopes[:, None] * pos[None, :], FLASH_AUG), axis=-1)
    return jnp.concatenate([aug, jnp.zeros((slopes.shape[0], t, FLASH_DV - FLASH_AUG), BF16)], axis=-1)


def diff_attention(qt, proj3d, k_blk0, vt, norms, slopes, lam_init, lq1, lk1, lq2, lk2, subln):
    bsz, nh, _, seq = qt.shape
    t = min(T_FLASH, seq)
    assert seq % t == 0 and t % FLASH_PANEL == 0 and FLASH_PANEL % FLASH_CHUNK == 0
    nt = seq // t
    norms = norms.reshape(bsz, nt, 2, nh).transpose(2, 0, 3, 1)
    qn, kn = norms[0], norms[1]
    tile = jnp.arange(nt)
    kt = (tile[:, None] + tile[None, :]) % nt
    min_dist = jnp.maximum((jnp.abs(tile[:, None] - kt) - 1) * t + 1, 0).astype(F32)
    upper = qn[..., :, None] * jnp.take(kn, kt, axis=-1) - slopes[None, :, None, None] * min_dist
    lower_ref = -(qn * kn)[..., :, None]
    skip = jnp.logical_and(upper - lower_ref <= -FLASH_ZERO_LOG2, kt != tile[:, None])
    kt_cols = [jnp.broadcast_to(kt[:, 0], skip.shape[:-1])]
    for ki in range(1, nt):
        kt_cols.append(jnp.where(skip[..., ki], kt_cols[-1], kt[:, ki]))
    kt_eff = jnp.stack(kt_cols, axis=-1).astype(I32).reshape(-1)
    skip = skip.astype(I32).reshape(-1)

    def step_id(b, h, qi, ki):
        return ((b * nh + h) * nt + qi) * nt + ki

    k_cols = alibi_key_columns(slopes, t)
    dv = FLASH_DV
    dva = FLASH_DV + FLASH_ONES
    r = jnp.arange(FLASH_CHUNK, dtype=F32)
    diag = -slopes[:, None, None] * jnp.abs(r[None, :] - r[:, None])[None]
    diag = jnp.stack([jnp.zeros_like(diag), diag], axis=1)
    smem = pl.BlockSpec(memory_space=pltpu.SMEM)
    vec = lambda n: pl.BlockSpec((1, n), lambda b, h, qi, ki, kte, sk: (0, 0))
    grid_spec = pltpu.PrefetchScalarGridSpec(
        num_scalar_prefetch=2,
        grid=(bsz, nh, nt, nt),
        in_specs=[
            smem, smem, smem, smem,
            pl.BlockSpec((1, 1, dv, t), lambda b, h, qi, ki, kte, sk: (b, h, 0, qi)),
            pl.BlockSpec((1, t, dv), lambda b, h, qi, ki, kte, sk: (b, kte[step_id(b, h, qi, ki)], k_blk0 + h)),
            pl.BlockSpec((1, t, dv), lambda b, h, qi, ki, kte, sk: (h, 0, 0)),
            pl.BlockSpec((1, 1, dv, t), lambda b, h, qi, ki, kte, sk: (b, h, 0, kte[step_id(b, h, qi, ki)])),
            pl.BlockSpec((1, 1, FLASH_CHUNK, FLASH_CHUNK),
                         lambda b, h, qi, ki, kte, sk: (h, (ki == 0).astype(I32), 0, 0)),
            vec(HEAD_DIM), vec(HEAD_DIM), vec(HEAD_DIM), vec(HEAD_DIM),
            pl.BlockSpec((dv, 1), lambda b, h, qi, ki, kte, sk: (0, 0)),
        ],
        out_specs=pl.BlockSpec((1, t, LANES), lambda b, h, qi, ki, kte, sk: (b, qi, h)),
        scratch_shapes=[
            pltpu.VMEM((2, 1, t), F32),
            pltpu.VMEM((2, dva, t), F32),
            pltpu.VMEM((2, t, FLASH_PANEL), F32),
            pltpu.SMEM((1,), F32),
        ],
    )
    return pl.pallas_call(
        functools.partial(_flash_body, t=t, step_id=step_id),
        grid_spec=grid_spec,
        out_shape=jax.ShapeDtypeStruct((bsz, seq, nh * dv), BF16),
        compiler_params=_cparams(("parallel", "parallel", "parallel", "arbitrary")),
        name="diff_attention",
    )(kt_eff, skip, slopes, lam_init, qn.reshape(-1), kn.reshape(-1), qt, proj3d, k_cols, vt, diag,
      lq1, lk1, lq2, lk2, subln)


def _outproj_body(x_ref, oa_ref, ob_ref, oc_ref, wa_ref, wb_ref, wc_ref, g1_ref, sc_ref, sh_ref,
                  n2_ref, wr_ref, br_ref, x1_ref, hp_ref, lg_ref):
    mix = (jnp.dot(oa_ref[...], wa_ref[...], preferred_element_type=F32)
           + jnp.dot(ob_ref[...], wb_ref[...], preferred_element_type=F32)
           + jnp.dot(oc_ref[...], wc_ref[...], preferred_element_type=F32))
    x1 = x_ref[...] + g1_ref[0] * mix
    x1_ref[...] = x1
    h = x1 * _rms_scale(x1) * n2_ref[...]
    h = h * (1.0 + sc_ref[0]) + sh_ref[0]
    half = h.shape[1] // 2
    hp_ref[...] = _pack_halves(h[:, :half], h[:, half:])
    lg_ref[...] = lax.dot_general(wr_ref[...], h, _NT, preferred_element_type=F32,
                                  precision=HIGHEST) + br_ref[...]


def out_projection(x2d, oa, ob, oc, wa, wb, wc, gate1, scale2, shift2, gain2, w_router_t, b_router, seq):
    t, d = x2d.shape
    e = w_router_t.shape[0]
    tm = min(TM_PROJ, seq)
    mod_spec = pl.BlockSpec((1, 1, d), lambda i: ((i * tm) // seq, 0, 0))
    row = lambda w: pl.BlockSpec((tm, w), lambda i: (i, 0))
    full = lambda a: pl.BlockSpec(a.shape, lambda i: (0, 0))
    return pl.pallas_call(
        _outproj_body,
        grid=(t // tm,),
        in_specs=[row(d), row(oa.shape[1]), row(ob.shape[1]), row(oc.shape[1]),
                  full(wa), full(wb), full(wc), mod_spec, mod_spec, mod_spec,
                  pl.BlockSpec((1, d), lambda i: (0, 0)), full(w_router_t),
                  pl.BlockSpec((e, 1), lambda i: (0, 0))],
        out_specs=[row(d), row(d // 2), pl.BlockSpec((e, tm), lambda i: (0, i))],
        out_shape=[jax.ShapeDtypeStruct((t, d), F32),
                   jax.ShapeDtypeStruct((t, d // 2), U32),
                   jax.ShapeDtypeStruct((e, t), F32)],
        compiler_params=_cparams(("parallel",)),
        name="out_projection",
    )(x2d, oa, ob, oc, wa, wb, wc, gate1, scale2, shift2, gain2, w_router_t, b_router)


def _router_body(lg_ref, tri_ref, eidx_ref, gate_ref, rank_ref, cnt_ref, base_s):
    step = pl.program_id(0)

    @pl.when(step == 0)
    def _():
        base_s[...] = jnp.zeros(base_s.shape, F32)

    work = lg_ref[...]
    n_e, tm = work.shape
    eio = lax.broadcasted_iota(I32, (n_e, tm), 0).astype(F32)
    vals, hots = [], []
    for k in range(TOP_K):
        mx = jnp.max(work, axis=0, keepdims=True)
        idx = jnp.min(jnp.where(work == mx, eio, float(n_e)), axis=0, keepdims=True)
        hot = eio == idx
        vals.append(mx)
        hots.append(hot)
        eidx_ref[k:k + 1, :] = idx.astype(I32)
        work = jnp.where(hot, -jnp.inf, work)
    exps = [jnp.exp(v - vals[0]) for v in vals]
    den = functools.reduce(lambda a, b: a + b, exps)
    for k in range(TOP_K):
        gate_ref[k:k + 1, :] = exps[k] / den
    any_hot = functools.reduce(lambda a, b: a + b, [h.astype(F32) for h in hots])
    prefix = jnp.dot(any_hot.astype(BF16), tri_ref[...], preferred_element_type=F32)
    pos = base_s[...] + prefix
    for k in range(TOP_K):
        rank_ref[k:k + 1, :] = jnp.sum(jnp.where(hots[k], pos, 0.0), axis=0, keepdims=True).astype(I32)
    base_s[...] = base_s[...] + jnp.sum(any_hot, axis=1, keepdims=True)
    cnt_ref[...] = base_s[...].astype(I32)


def route(logits_t):
    n_e, t = logits_t.shape
    tm = min(TM_ROUTE, t)
    assert t % tm == 0
    r = jnp.arange(tm)
    tri = (r[:, None] < r[None, :]).astype(BF16)
    out4 = lambda dt: jax.ShapeDtypeStruct((TOP_K, t), dt)
    spec4 = pl.BlockSpec((TOP_K, tm), lambda i: (0, i))
    return pl.pallas_call(
        _router_body,
        grid=(t // tm,),
        in_specs=[pl.BlockSpec((n_e, tm), lambda i: (0, i)),
                  pl.BlockSpec((tm, tm), lambda i: (0, 0))],
        out_specs=[spec4, spec4, spec4, pl.BlockSpec((n_e, 1), lambda i: (0, 0))],
        out_shape=[out4(I32), out4(F32), out4(I32), jax.ShapeDtypeStruct((n_e, 1), I32)],
        scratch_shapes=[pltpu.VMEM((n_e, 1), F32)],
        compiler_params=_cparams(("arbitrary",)),
        name="route",
    )(logits_t, tri)


def _dispatch_body(dest_ref, h_ref, init_ref, xs_ref, sem, *, tm):
    del init_ref

    def issue(tok, carry):
        for k in range(TOP_K):
            pltpu.make_async_copy(h_ref.at[pl.ds(tok, 1)], xs_ref.at[pl.ds(dest_ref[k, tok], 1)], sem).start()
        return carry

    lax.fori_loop(0, tm, issue, 0, unroll=DMA_ISSUE_UNROLL)
    for k in range(TOP_K):
        pltpu.make_async_copy(h_ref, xs_ref.at[pl.ds(0, tm)], sem).wait()


def dispatch_rows(h_packed, dest, n_rows):
    t, w = h_packed.shape
    tm = min(TM_DISPATCH, t)
    return pl.pallas_call(
        functools.partial(_dispatch_body, tm=tm),
        grid=(t // tm,),
        in_specs=[pl.BlockSpec((TOP_K, tm), lambda i: (0, i), memory_space=pltpu.SMEM),
                  pl.BlockSpec((tm, w), lambda i: (i, 0)),
                  pl.BlockSpec(memory_space=pl.ANY)],
        out_specs=pl.BlockSpec(memory_space=pl.ANY),
        out_shape=jax.ShapeDtypeStruct((n_rows, w), U32),
        input_output_aliases={2: 0},
        scratch_shapes=[pltpu.SemaphoreType.DMA],
        compiler_params=_cparams(("arbitrary",)),
        name="dispatch_rows",
    )(dest, h_packed, jnp.zeros((n_rows, w), U32))


def _ffn_body(be_ref, bv_ref, nu_ref, xs_ref, wg_ref, bg_ref, wu_ref, bu_ref, wd_ref, bd_ref, o_ref,
              wg_s, wu_s, wd_s):
    i = pl.program_id(0)

    @pl.when(i < nu_ref[0])
    def _():
        @pl.when(jnp.logical_or(i == 0, be_ref[i] != be_ref[jnp.maximum(i - 1, 0)]))
        def _():
            wg_s[...] = wg_ref[0, 0].astype(BF16)
            wu_s[...] = wu_ref[0, 0].astype(BF16)
            wd_s[...] = wd_ref[0, 0].astype(BF16)

        xa, xb = _unpack_halves(xs_ref[...])
        blk, half = xa.shape
        valid = lax.broadcasted_iota(I32, (blk, 1), 0) < bv_ref[i]
        xa = jnp.where(valid, xa, 0.0).astype(BF16)
        xb = jnp.where(valid, xb, 0.0).astype(BF16)

        def proj(w_s, b_ref):
            return (jnp.dot(xa, w_s[:half, :], preferred_element_type=F32)
                    + jnp.dot(xb, w_s[half:, :], preferred_element_type=F32) + b_ref[0, 0])

        a = jnp.minimum(proj(wg_s, bg_ref), SWIGLU_LIMIT)
        u = jnp.clip(proj(wu_s, bu_ref), -SWIGLU_LIMIT, SWIGLU_LIMIT)
        hid = a * (1.0 / (1.0 + jnp.exp(-SWIGLU_ALPHA * a))) * (u + 1.0)
        out = jnp.dot(hid.astype(BF16), wd_s[...], preferred_element_type=F32) + bd_ref[0, 0]
        d = out.shape[1] // 2
        o_ref[...] = _pack_halves(out[:, :d], out[:, d:])

    @pl.when(i >= nu_ref[0])
    def _():
        o_ref[...] = jnp.zeros(o_ref.shape, o_ref.dtype)


def expert_ffn(xs, blk_expert, blk_valid, n_used, layer, wg, bg, wu, bu, wd, bd):
    n_rows, w = xs.shape
    _, n_e, d, f = wg.shape
    nb = n_rows // BLK_FFN
    row_map = lambda i, be, bv, nu: (jnp.minimum(i, nu[0] - 1), 0)
    exp_map = lambda i, be, bv, nu: (layer, be[i], 0, 0)
    grid_spec = pltpu.PrefetchScalarGridSpec(
        num_scalar_prefetch=3,
        grid=(nb,),
        in_specs=[pl.BlockSpec((BLK_FFN, w), row_map),
                  pl.BlockSpec((1, 1, d, f), exp_map), pl.BlockSpec((1, 1, 1, f), exp_map),
                  pl.BlockSpec((1, 1, d, f), exp_map), pl.BlockSpec((1, 1, 1, f), exp_map),
                  pl.BlockSpec((1, 1, f, d), exp_map), pl.BlockSpec((1, 1, 1, d), exp_map)],
        out_specs=pl.BlockSpec((BLK_FFN, d // 2), lambda i, be, bv, nu: (i, 0)),
        scratch_shapes=[pltpu.VMEM((d, f), BF16), pltpu.VMEM((d, f), BF16), pltpu.VMEM((f, d), BF16)],
    )
    return pl.pallas_call(
        _ffn_body,
        grid_spec=grid_spec,
        out_shape=jax.ShapeDtypeStruct((n_rows, d // 2), U32),
        compiler_params=_cparams(("arbitrary",)),
        name="expert_ffn",
    )(blk_expert, blk_valid, n_used, xs, wg, bg, wu, bu, wd, bd)


def _combine_body(dest_ref, x_ref, gt_ref, g2_ref, nf_ref, ys_ref, o_ref, buf, sem, *, tm, final):
    def issue(tok, carry):
        for k in range(TOP_K):
            pltpu.make_async_copy(ys_ref.at[pl.ds(dest_ref[k, tok], 1)], buf.at[k, pl.ds(tok, 1)], sem).start()
        return carry

    lax.fori_loop(0, tm, issue, 0, unroll=DMA_ISSUE_UNROLL)
    for k in range(TOP_K):
        pltpu.make_async_copy(ys_ref.at[pl.ds(0, tm)], buf.at[k], sem).wait()
    gt = gt_ref[...]
    lo = hi = None
    for k in range(TOP_K):
        a, b = _unpack_halves(buf[k])
        g = gt[:, k:k + 1]
        lo = g * a if lo is None else lo + g * a
        hi = g * b if hi is None else hi + g * b
    half = lo.shape[1]
    g2 = g2_ref[0]
    x_lo = x_ref[:, :half] + g2[:, :half] * lo
    x_hi = x_ref[:, half:] + g2[:, half:] * hi
    if final:
        ms = (jnp.sum(x_lo * x_lo, axis=-1, keepdims=True)
              + jnp.sum(x_hi * x_hi, axis=-1, keepdims=True)) / (2 * half)
        r = lax.rsqrt(ms + RMS_EPS)
        x_lo = x_lo * r * nf_ref[:, :half]
        x_hi = x_hi * r * nf_ref[:, half:]
    o_ref[:, :half] = x_lo
    o_ref[:, half:] = x_hi


def combine(x1, ys, dest, gates_t, gate2, norm_f, seq, final):
    t, d = x1.shape
    tm = min(TM_COMBINE, seq)
    return pl.pallas_call(
        functools.partial(_combine_body, tm=tm, final=final),
        grid=(t // tm,),
        in_specs=[pl.BlockSpec((TOP_K, tm), lambda i: (0, i), memory_space=pltpu.SMEM),
                  pl.BlockSpec((tm, d), lambda i: (i, 0)),
                  pl.BlockSpec((tm, TOP_K), lambda i: (i, 0)),
                  pl.BlockSpec((1, 1, d), lambda i: ((i * tm) // seq, 0, 0)),
                  pl.BlockSpec((1, d), lambda i: (0, 0)),
                  pl.BlockSpec(memory_space=pl.ANY)],
        out_specs=pl.BlockSpec((tm, d), lambda i: (i, 0)),
        out_shape=jax.ShapeDtypeStruct((t, d), F32),
        scratch_shapes=[pltpu.VMEM((TOP_K, tm, d // 2), U32), pltpu.SemaphoreType.DMA],
        compiler_params=_cparams(("arbitrary",)),
        name="combine",
    )(dest, x1, gates_t, gate2, norm_f, ys)


def _combine_project_body(dcur_ref, dnext_ref, x_ref, gt_ref, g2_ref, sc_ref, sh_ref, gn_ref, w_ref, ys_ref,
                          x2_ref, proj_ref, nrm_ref, buf0, buf1, sem, *, tm, norm_cols):
    i = pl.program_id(0)
    bufs = (buf0, buf1)

    def start(dref, col, tok, k, s):
        pltpu.make_async_copy(ys_ref.at[pl.ds(dref[k, col + tok], 1)], bufs[s].at[k, pl.ds(tok, 1)], sem.at[s]).start()

    def wait(s):
        for k in range(TOP_K):
            pltpu.make_async_copy(ys_ref.at[pl.ds(0, tm)], bufs[s].at[k], sem.at[s]).wait()

    def issue_inline(dref, col, s):
        for tok in range(tm):
            for k in range(TOP_K):
                start(dref, col, tok, k, s)

    def compute(s):
        rows = slice(s * tm, (s + 1) * tm)
        gt = gt_ref[rows, :]
        lo = hi = None
        for k in range(TOP_K):
            a, b = _unpack_halves(bufs[s][k])
            g = gt[:, k:k + 1]
            lo = g * a if lo is None else lo + g * a
            hi = g * b if hi is None else hi + g * b
        half = lo.shape[1]
        g2 = g2_ref[0]
        x2 = jnp.concatenate([x_ref[rows, :half] + g2[:, :half] * lo, x_ref[rows, half:] + g2[:, half:] * hi],
                             axis=1)
        x2_ref[rows, :] = x2
        _project_rows(x2, sc_ref[0], sh_ref[0], gn_ref[...], w_ref, proj_ref.at[rows, :], nrm_ref.at[s:s + 1],
                      norm_cols)

    @pl.when(i == 0)
    def _():
        def issue(tok, carry):
            for k in range(TOP_K):
                start(dcur_ref, 0, tok, k, 0)
            return carry

        lax.fori_loop(0, tm, issue, 0, unroll=DMA_ISSUE_UNROLL)

    wait(0)
    issue_inline(dcur_ref, tm, 1)
    compute(0)
    wait(1)
    issue_inline(dnext_ref, 0, 0)
    compute(1)

    @pl.when(i == pl.num_programs(0) - 1)
    def _():
        wait(0)


def combine_project(x1, ys, dest, gates_t, gate2, scale, shift, gain, w_bf16, seq, norm_cols):
    t, d = x1.shape
    n = w_bf16.shape[1]
    tm = min(TM_COMBINE, seq // 2)
    nblocks = t // tm
    nsteps = nblocks // 2
    assert t % (2 * tm) == 0 and seq % (2 * tm) == 0
    mod_spec = pl.BlockSpec((1, 1, d), lambda i: ((i * 2 * tm) // seq, 0, 0))
    return pl.pallas_call(
        functools.partial(_combine_project_body, tm=tm, norm_cols=norm_cols),
        grid=(nsteps,),
        in_specs=[pl.BlockSpec((TOP_K, 2 * tm), lambda i: (0, i), memory_space=pltpu.SMEM),
                  pl.BlockSpec((TOP_K, tm), lambda i: (0, jnp.minimum(2 * i + 2, nblocks - 1)),
                               memory_space=pltpu.SMEM),
                  pl.BlockSpec((2 * tm, d), lambda i: (i, 0)),
                  pl.BlockSpec((2 * tm, TOP_K), lambda i: (i, 0)),
                  mod_spec, mod_spec, mod_spec,
                  pl.BlockSpec((1, d), lambda i: (0, 0)),
                  pl.BlockSpec((d, n), lambda i: (0, 0)),
                  pl.BlockSpec(memory_space=pl.ANY)],
        out_specs=[pl.BlockSpec((2 * tm, d), lambda i: (i, 0)),
                   pl.BlockSpec((2 * tm, n), lambda i: (i, 0)),
                   pl.BlockSpec((2, 1, len(norm_cols)), lambda i: (i, 0, 0))],
        out_shape=[jax.ShapeDtypeStruct((t, d), F32),
                   jax.ShapeDtypeStruct((t, n), BF16),
                   jax.ShapeDtypeStruct((nblocks, 1, len(norm_cols)), F32)],
        scratch_shapes=[pltpu.VMEM((TOP_K, tm, d // 2), U32), pltpu.VMEM((TOP_K, tm, d // 2), U32),
                        pltpu.SemaphoreType.DMA((2,))],
        compiler_params=_cparams(("arbitrary",)),
        name="combine_project",
    )(dest, dest, x1, gates_t, gate2, scale, shift, gain, w_bf16, ys)


def _alibi_slopes(n):
    return jnp.exp2(-8.0 * jnp.arange(1, n + 1, dtype=F32) / n)


def _routing_tables(counts, eidx, rank, nb):
    counts = counts.reshape(-1)
    padded = (counts + BLK_FFN - 1) // BLK_FFN * BLK_FFN
    pad_end = jnp.cumsum(padded)
    pad_start = pad_end - padded
    e_ids = jnp.arange(N_EXPERTS, dtype=I32)[:, None, None]
    dest = rank + jnp.sum(jnp.where(eidx[None] == e_ids, pad_start[:, None, None], 0), axis=0)
    blk_row0 = jnp.arange(nb, dtype=I32) * BLK_FFN
    blk_expert = jnp.minimum(jnp.sum(pad_end[None, :] <= blk_row0[:, None], axis=1), N_EXPERTS - 1).astype(I32)
    blk_valid = jnp.clip(counts[blk_expert] - (blk_row0 - pad_start[blk_expert]), 0, BLK_FFN).astype(I32)
    n_used = (pad_end[-1:] // BLK_FFN).astype(I32)
    return dest.astype(I32), blk_expert, blk_valid, n_used


def kernel(x, c, w_ada, b_ada, norm1, w_in, sinks, lambda_q1, lambda_k1, lambda_q2, lambda_k2, subln,
           w_out, norm2, w_router, b_router, w_gate, b_gate, w_up, b_up, w_down, b_down, norm_f):
    bsz, seq, d = x.shape
    depth = w_ada.shape[0]
    t = bsz * seq
    wa_w, wb_w = N_HEADS_A * HEAD_DIM, N_HEADS_B * HEAD_DIM
    kvb_w = N_KV_B * HEAD_DIM
    wc_w = N_HEADS_C * 2 * HEAD_DIM
    off = [0]
    for wdt in (wa_w, wa_w, wa_w, wb_w, kvb_w, kvb_w, wc_w, wc_w, wc_w):
        off.append(off[-1] + wdt)
    blk = [o // LANES for o in off]
    norm_cols = tuple(off[i] + hh * FLASH_DV for i in (6, 7) for hh in range(N_HEADS_C))

    group = N_HEADS_B // N_KV_B
    b_perm = [kv * group + g for g in range(group) for kv in range(N_KV_B)]
    col_perm = jnp.concatenate([jnp.arange(h * HEAD_DIM, (h + 1) * HEAD_DIM) for h in b_perm])
    q_scale = jnp.ones((off[-1],), F32)
    for s0, s1, extra in ((off[0], off[1], 1.0), (off[3], off[4], 1.0), (off[6], off[7], LOG2E)):
        q_scale = q_scale.at[s0:s1].set(HEAD_DIM ** -0.5 * extra)

    slopes_a = _alibi_slopes(N_HEADS_A)
    slopes_b = _alibi_slopes(N_HEADS_B)[jnp.array(b_perm)]
    slopes_c = _alibi_slopes(N_HEADS_C) * LOG2E

    mod_all = ada_modulation(c, w_ada, b_ada)
    x2d = x.reshape(t, d)
    nb = (t * TOP_K) // BLK_FFN + N_EXPERTS
    n_rows = nb * BLK_FFN

    def layer_mod(layer):
        mod = mod_all[layer].reshape(bsz, N_MOD, 1, d)
        return [mod[:, i] for i in range(N_MOD)]

    def in_weights(layer):
        w_l = w_in[layer] * q_scale[None, :]
        w_l = jnp.concatenate([w_l[:, :off[3]], w_l[:, off[3]:off[4]][:, col_perm], w_l[:, off[4]:]], axis=1)
        return w_l.astype(BF16)

    proj = blk_norms = None
    for layer in range(depth):
        sh1, sc1, g1, sh2, sc2, g2 = layer_mod(layer)
        if layer == 0:
            proj, blk_norms = in_projection(x2d, sc1, sh1, norm1[layer].reshape(1, d), in_weights(layer), seq,
                                            norm_cols)
        proj3d = proj.reshape(bsz, seq, off[-1])

        n_blk_a = wa_w // LANES
        o_a = window_attention(proj3d, blk[0], blk[1], blk[2], 1, n_blk_a, slopes_a,
                               jnp.full((n_blk_a, 2), -jnp.inf, F32), DILATED_PAIRS).reshape(t, wa_w)

        sink_l = sinks[layer].astype(F32)[jnp.array(b_perm)].reshape(group, N_KV_B)
        o_b = window_attention(proj3d, blk[3], blk[4], blk[5], 0, group, slopes_b, sink_l,
                               ((2 * SWA_HALF, 1),)).reshape(t, wb_w)

        lam_init = jnp.full((1,), 0.8 - 0.6 * math.exp(-0.3 * layer), F32)
        heads_t = lambda i: (proj3d[:, :, off[i]:off[i + 1]].reshape(bsz, seq, N_HEADS_C, FLASH_DV)
                             .transpose(0, 2, 3, 1))
        t_flash = min(T_FLASH, seq)
        norms = blk_norms.reshape(bsz, seq // t_flash, -1, 2 * N_HEADS_C).max(axis=2, keepdims=True)
        o_c = diff_attention(heads_t(6), proj3d, blk[7], heads_t(8), norms, slopes_c, lam_init,
                             lambda_q1[layer].reshape(1, -1), lambda_k1[layer].reshape(1, -1),
                             lambda_q2[layer].reshape(1, -1), lambda_k2[layer].reshape(1, -1),
                             subln[layer].reshape(-1, 1)).reshape(t, wc_w)

        wo = w_out[layer]
        wo_b = wo[wa_w:wa_w + wb_w].reshape(N_HEADS_B, HEAD_DIM, d)[jnp.array(b_perm)].reshape(wb_w, d)
        x1, h_packed, logits_t = out_projection(
            x2d, o_a, o_b, o_c, wo[:wa_w].astype(BF16), wo_b.astype(BF16), wo[wa_w + wb_w:].astype(BF16),
            g1, sc2, sh2, norm2[layer].reshape(1, d), w_router[layer].T, b_router[layer].reshape(-1, 1), seq)

        eidx, gates, rank, counts = route(logits_t)
        dest, blk_expert, blk_valid, n_used = _routing_tables(counts, eidx, rank, nb)
        xs = dispatch_rows(h_packed, dest, n_rows)
        ys = expert_ffn(xs, blk_expert, blk_valid, n_used, layer,
                        w_gate, b_gate[:, :, None, :], w_up, b_up[:, :, None, :], w_down, b_down[:, :, None, :])
        if layer == depth - 1:
            x2d = combine(x1, ys, dest, gates.T, g2, norm_f.reshape(1, d), seq, True)
        else:
            nsh1, nsc1 = layer_mod(layer + 1)[:2]
            x2d, proj, blk_norms = combine_project(x1, ys, dest, gates.T, g2, nsc1, nsh1,
                                                   norm1[layer + 1].reshape(1, d), in_weights(layer + 1), seq,
                                                   norm_cols)

    return x2d.reshape(bsz, seq, d)
```

```python
import functools
import math

import jax
import jax.numpy as jnp
from jax import lax
from jax.experimental import pallas as pl
from jax.experimental.pallas import tpu as pltpu

F32 = jnp.float32
BF16 = jnp.bfloat16
I32 = jnp.int32
U32 = jnp.uint32

HEAD_DIM = 64
DILATED_PAIRS = ((128, 1), (512, 4), (2048, 16))
N_HEADS_A = 4
N_HEADS_B = 4
N_KV_B = 2
SWA_HALF = 128
N_HEADS_C = 4
N_EXPERTS = 32
TOP_K = 4
SWIGLU_LIMIT = 7.0
SWIGLU_ALPHA = 1.702
RMS_EPS = 1e-6
N_MOD = 6

LANES = 128
VMEM_LIMIT_BYTES = 56 * 1024 * 1024

TM_PROJ = 512
TQ_BAND = 256
T_FLASH = 1024
FLASH_PANEL = 512
FLASH_CHUNK = 256
FLASH_SUB = 128
FLASH_DV = 2 * HEAD_DIM
FLASH_ONES = 16
FLASH_AUG = 3
FLASH_ZERO_LOG2 = 150.0
FLASH_LOOKAHEAD = 2
FLASH_SAFE_LOG2 = 60.0
LOG2E = math.log2(math.e)
TM_ROUTE = 512
TM_DISPATCH = 512
BLK_FFN = 512
TM_COMBINE = 256
DMA_ISSUE_UNROLL = 8
N_DMA_THREADS = 2

HIGHEST = lax.Precision.HIGHEST
_NT = (((1,), (1,)), ((), ()))


def _cparams(sem):
    return pltpu.CompilerParams(dimension_semantics=sem, vmem_limit_bytes=VMEM_LIMIT_BYTES)


def _pack_halves(a, b):
    ua = lax.bitcast_convert_type(a.astype(BF16).astype(F32), U32)
    ub = lax.bitcast_convert_type(b.astype(BF16).astype(F32), U32)
    return ua | (ub >> 16)


def _unpack_halves(w):
    a = lax.bitcast_convert_type(w & jnp.uint32(0xFFFF0000), F32)
    b = lax.bitcast_convert_type(w << 16, F32)
    return a, b


def _rms_scale(x):
    return lax.rsqrt(jnp.mean(x * x, axis=-1, keepdims=True) + RMS_EPS)


def _ada_body(c_ref, w_ref, b_ref, o_ref):
    c = c_ref[...]
    cond = c / (1.0 + jnp.exp(-c))
    o_ref[0] = jnp.dot(cond, w_ref[0], preferred_element_type=F32, precision=HIGHEST) + b_ref[0]


def ada_modulation(c, w_ada, b_ada):
    depth, d, n = w_ada.shape
    bsz = c.shape[0]
    tn = 1536
    assert n % tn == 0
    return pl.pallas_call(
        _ada_body,
        grid=(depth, n // tn),
        in_specs=[
            pl.BlockSpec((bsz, d), lambda l, j: (0, 0)),
            pl.BlockSpec((1, d, tn), lambda l, j: (l, 0, j)),
            pl.BlockSpec((1, 1, tn), lambda l, j: (l, 0, j)),
        ],
        out_specs=pl.BlockSpec((1, bsz, tn), lambda l, j: (l, 0, j)),
        out_shape=jax.ShapeDtypeStruct((depth, bsz, n), F32),
        compiler_params=_cparams(("parallel", "parallel")),
        name="ada_modulation",
    )(c, w_ada, b_ada.reshape(depth, 1, n))


def _project_rows(x, sc, sh, g, w_ref, o_ref, n_ref, norm_cols):
    h = x * _rms_scale(x) * g
    h = h * (1.0 + sc) + sh
    out = jnp.dot(h.astype(BF16), w_ref[...], preferred_element_type=F32).astype(BF16)
    o_ref[...] = out
    lane = lax.broadcasted_iota(I32, (1, len(norm_cols)), 1)
    acc = jnp.zeros((1, len(norm_cols)), F32)
    for j, c0 in enumerate(norm_cols):
        v = out[:, c0:c0 + FLASH_DV].astype(F32)
        n2 = jnp.max(jnp.sum(v * v, axis=1, keepdims=True), axis=0, keepdims=True)
        acc = jnp.where(lane == j, jnp.sqrt(n2), acc)
    n_ref[0] = acc


def _inproj_body(x_ref, sc_ref, sh_ref, g_ref, w_ref, o_ref, n_ref, *, norm_cols):
    _project_rows(x_ref[...], sc_ref[0], sh_ref[0], g_ref[...], w_ref, o_ref, n_ref, norm_cols)


def in_projection(x2d, scale, shift, gain, w_bf16, seq, norm_cols):
    t, d = x2d.shape
    n = w_bf16.shape[1]
    tm = min(TM_PROJ, seq)
    assert seq % tm == 0 and t % tm == 0
    mod_spec = pl.BlockSpec((1, 1, d), lambda i: ((i * tm) // seq, 0, 0))
    return pl.pallas_call(
        functools.partial(_inproj_body, norm_cols=norm_cols),
        grid=(t // tm,),
        in_specs=[
            pl.BlockSpec((tm, d), lambda i: (i, 0)),
            mod_spec, mod_spec,
            pl.BlockSpec((1, d), lambda i: (0, 0)),
            pl.BlockSpec((d, n), lambda i: (0, 0)),
        ],
        out_specs=[pl.BlockSpec((tm, n), lambda i: (i, 0)),
                   pl.BlockSpec((1, 1, len(norm_cols)), lambda i: (i, 0, 0))],
        out_shape=[jax.ShapeDtypeStruct((t, n), BF16),
                   jax.ShapeDtypeStruct((t // tm, 1, len(norm_cols)), F32)],
        compiler_params=_cparams(("parallel",)),
        name="in_projection",
    )(x2d, scale, shift, gain, w_bf16)


def _offset_bias_body(c_ref, slope_ref, o_ref, *, pairs):
    vi = pl.program_id(0)
    hd = pl.program_id(1)
    tq, win = o_ref.shape[2], o_ref.shape[3]
    off = (lax.broadcasted_iota(I32, (tq, win), 1) - lax.broadcasted_iota(I32, (tq, win), 0)) + c_ref[vi]
    dist = jnp.abs(off)
    mult = jnp.zeros((tq, win), F32)
    for window, dil in pairs:
        hit = jnp.logical_and(dist <= window // 2, (dist & (dil - 1)) == 0)
        mult = mult + jnp.where(hit, 1.0, 0.0)
    o_ref[0, 0] = jnp.log(mult) - slope_ref[hd] * dist.astype(F32)


def _window_variants(seq, tq, reach):
    win = tq + 2 * reach
    nq, edge = seq // tq, -(-reach // tq)
    assert seq % tq == 0 and reach % 16 == 0 and win <= seq and nq >= 2 * edge + 1
    rel = [min(max(i * tq - reach, 0), seq - win) - i * tq for i in range(nq)]
    assert all(r == -reach for r in rel[edge:nq - edge])
    starts = rel[:edge] + [-reach] + rel[nq - edge:]

    def variant(i):
        return jnp.where(i < edge, i, jnp.where(i >= nq - edge, i - (nq - 2 * edge - 1), edge))

    return win, starts, variant


def _window_body(c_ref, sink_ref, q_ref, k_ref, v_ref, tab_ref, o_ref, *, tq, win, variant):
    j = pl.program_id(1)
    i = pl.program_id(2)
    ks = pl.multiple_of(i * tq + c_ref[variant(i)], 16)
    q = q_ref[0]
    k = k_ref[0, pl.ds(ks, win), :]
    v = v_ref[0, pl.ds(ks, win), :]
    lane_lo = lax.broadcasted_iota(I32, (1, LANES), 1) < HEAD_DIM
    outs = []
    for hf in range(2):
        qm = jnp.where(lane_lo if hf == 0 else jnp.logical_not(lane_lo), q, jnp.zeros_like(q))
        s = lax.dot_general(qm, k, _NT, preferred_element_type=F32) + tab_ref[0, hf]
        sink = sink_ref[j, hf]
        m = jnp.maximum(jnp.max(s, axis=-1, keepdims=True), sink)
        p = jnp.exp(s - m)
        l = jnp.sum(p, axis=-1, keepdims=True) + jnp.exp(sink - m)
        outs.append(jnp.dot(p.astype(BF16), v, preferred_element_type=F32) / l)
    o_ref[0] = jnp.where(lane_lo, outs[0], outs[1]).astype(o_ref.dtype)


def window_attention(proj3d, q_blk0, k_blk0, v_blk0, kv_step, n_blk, slopes, sinks, pairs):
    bsz, seq, _ = proj3d.shape
    assert all(d & (d - 1) == 0 for _, d in pairs)
    reach = max(w // 2 for w, _ in pairs)
    tq = TQ_BAND
    win, starts, variant = _window_variants(seq, tq, reach)
    starts = jnp.array(starts, I32)
    smem = pl.BlockSpec(memory_space=pltpu.SMEM)
    tables = pl.pallas_call(
        functools.partial(_offset_bias_body, pairs=pairs),
        grid=(starts.shape[0], 2 * n_blk),
        in_specs=[smem, smem],
        out_specs=pl.BlockSpec((1, 1, tq, win), lambda vi, hd: (vi, hd, 0, 0)),
        out_shape=jax.ShapeDtypeStruct((starts.shape[0], 2 * n_blk, tq, win), F32),
        compiler_params=_cparams(("parallel", "parallel")),
        name="offset_bias_tables",
    )(starts, slopes)
    return pl.pallas_call(
        functools.partial(_window_body, tq=tq, win=win, variant=variant),
        grid=(bsz, n_blk, seq // tq),
        in_specs=[
            smem, smem,
            pl.BlockSpec((1, tq, LANES), lambda b, j, i: (b, i, q_blk0 + j)),
            pl.BlockSpec((1, seq, LANES), lambda b, j, i: (b, 0, k_blk0 + kv_step * j)),
            pl.BlockSpec((1, seq, LANES), lambda b, j, i: (b, 0, v_blk0 + kv_step * j)),
            pl.BlockSpec((1, 2, tq, win), lambda b, j, i: (variant(i), j, 0, 0)),
        ],
        out_specs=pl.BlockSpec((1, tq, LANES), lambda b, j, i: (b, i, j)),
        out_shape=jax.ShapeDtypeStruct((bsz, seq, n_blk * LANES), BF16),
        compiler_params=_cparams(("parallel", "parallel", "parallel")),
        name="window_attention",
    )(starts, sinks, proj3d, proj3d, proj3d, tables)


def _flash_body(kte_ref, skip_ref, slope_ref, lami_ref, qn_ref, kn_ref, qt_ref, k_ref, kaug_ref, vt_ref, diag_ref,
                lq1_ref, lk1_ref, lq2_ref, lk2_ref, sub_ref, o_ref, m_s, acc_s, s_scr, mmin_s, *, t, step_id):
    b = pl.program_id(0)
    h = pl.program_id(1)
    qi = pl.program_id(2)
    ki = pl.program_id(3)
    nt = pl.num_programs(3)
    kt = lax.rem(qi + ki, nt)

    @pl.when(ki == 0)
    def _():
        m_s[...] = jnp.full(m_s.shape, -jnp.inf, F32)
        acc_s[...] = jnp.zeros(acc_s.shape, F32)
        mmin_s[0] = jnp.float32(0.0)

    slope = slope_ref[h]
    n_chunk = t // FLASH_CHUNK
    n_panel = t // FLASH_PANEL
    sub_per_panel = FLASH_PANEL // FLASH_CHUNK
    row = lax.broadcasted_iota(I32, (FLASH_DV, 1), 0)
    row_lo = row < HEAD_DIM
    row_aug = lax.broadcasted_iota(I32, (FLASH_ONES, 1), 0) < FLASH_AUG
    lane_pos = lax.broadcasted_iota(I32, (1, FLASH_CHUNK), 1).astype(F32)
    ones_rows = jnp.ones((FLASH_ONES, FLASH_CHUNK), BF16)

    def with_sign(qtm, sign):
        aug = jnp.where(row_aug, sign, 0.0).astype(BF16)
        pad = jnp.zeros((FLASH_DV - FLASH_ONES, sign.shape[1]), BF16)
        return jnp.concatenate([qtm, aug, pad], axis=0)

    def keys(kc):
        rows = slice(kc * FLASH_CHUNK, (kc + 1) * FLASH_CHUNK)
        return jnp.concatenate([k_ref[0, rows, :], kaug_ref[0, rows, :]], axis=1)

    def values(kc):
        return jnp.concatenate([vt_ref[0, 0, :, kc * FLASH_CHUNK:(kc + 1) * FLASH_CHUNK], ones_rows], axis=0)
    blocks = [(qp, mp) for qp in range(n_panel) for mp in range(2)]

    def fold8(x, op):
        return functools.reduce(op, [x[r:r + 8] for r in range(0, x.shape[0], 8)])

    def pair_terms(qp, kc):
        signs, shifts = [], []
        for j in range(sub_per_panel):
            gap = (qi * n_chunk + qp * sub_per_panel + j) - (kt * n_chunk + kc)
            sign = jnp.sign(gap).astype(F32)
            signs.append(jnp.full((1, FLASH_CHUNK), sign, F32))
            shifts.append(-slope * ((jnp.abs(gap) * FLASH_CHUNK).astype(F32) + sign * lane_pos))
        return jnp.concatenate(signs, axis=1), jnp.concatenate(shifts, axis=1)

    def masked_queries(bi):
        qp, mp = blocks[bi]
        qt = qt_ref[0, 0, :, qp * FLASH_PANEL:(qp + 1) * FLASH_PANEL]
        return jnp.where(row_lo if mp == 0 else jnp.logical_not(row_lo), qt, jnp.zeros_like(qt))

    def score_chunk(bi, kc, qtm, mx):
        qp, _ = blocks[bi]
        k0 = kc * FLASH_CHUNK
        sign, lane_shift = pair_terms(qp, kc)
        s = jnp.dot(keys(kc), with_sign(qtm, sign), preferred_element_type=F32)
        j = kc - qp * sub_per_panel
        if 0 <= j < sub_per_panel:
            cols = [s[:, i * FLASH_CHUNK:(i + 1) * FLASH_CHUNK] for i in range(sub_per_panel)]
            cols[j] = cols[j] + diag_ref[0, 0]
            s = jnp.concatenate(cols, axis=1)
        s_scr[bi % 2, k0:k0 + FLASH_CHUNK, :] = s
        cm = fold8(s, jnp.maximum) + lane_shift
        return cm if mx is None else jnp.maximum(mx, cm)

    def prob_chunk(bi, kc, m_new, acc):
        qp, _ = blocks[bi]
        k0 = kc * FLASH_CHUNK
        shift = m_new - pair_terms(qp, kc)[1]
        parts = []
        for r0 in range(k0, k0 + FLASH_CHUNK, FLASH_SUB):
            parts.append(jnp.exp2((s_scr[bi % 2, r0:r0 + FLASH_SUB, :] - shift).astype(BF16)))
        pv = jnp.dot(values(kc), jnp.concatenate(parts, axis=0), preferred_element_type=F32)
        return pv if acc is None else acc + pv

    def two_pass_tile():
        qtm_next = masked_queries(0)
        mx_next = None
        for kc in range(n_chunk):
            mx_next = score_chunk(0, kc, qtm_next, mx_next)
        for bi, (qp, mp) in enumerate(blocks):
            qs = slice(qp * FLASH_PANEL, (qp + 1) * FLASH_PANEL)
            m_old = m_s[mp, :, qs]
            m_new = jnp.maximum(m_old, jnp.max(mx_next, axis=0, keepdims=True))
            alpha = jnp.exp2(m_old - m_new)
            acc = None
            has_next = bi + 1 < len(blocks)
            if has_next:
                qtm_next = masked_queries(bi + 1)
            mx_next = None
            for kc in range(n_chunk):
                if has_next:
                    mx_next = score_chunk(bi + 1, kc, qtm_next, mx_next)
                acc = prob_chunk(bi, kc, m_new, acc)
            acc_s[mp, :, qs] = alpha * acc_s[mp, :, qs] + acc
            m_s[mp, :, qs] = m_new
        mmin_s[0] = jnp.min(m_s[...])

    def one_pass_tile():
        items = [(bi, kc) for bi in range(len(blocks)) for kc in range(n_chunk)]

        def scores(bi, kc):
            qp, mp = blocks[bi]
            sign, lane_shift = pair_terms(qp, kc)
            s = jnp.dot(keys(kc), with_sign(masked_queries(bi), sign), preferred_element_type=F32)
            return s, m_s[mp, :, qp * FLASH_PANEL:(qp + 1) * FLASH_PANEL] - lane_shift

        ahead = [scores(*items[i]) for i in range(FLASH_LOOKAHEAD)]
        acc = None
        for n, (bi, kc) in enumerate(items):
            s, shift = ahead.pop(0)
            if n + FLASH_LOOKAHEAD < len(items):
                ahead.append(scores(*items[n + FLASH_LOOKAHEAD]))
            p = jnp.exp2(s - shift).astype(BF16)
            pv = jnp.dot(values(kc), p, preferred_element_type=F32)
            acc = pv if kc == 0 else acc + pv
            if kc == n_chunk - 1:
                qp, mp = blocks[bi]
                qs = slice(qp * FLASH_PANEL, (qp + 1) * FLASH_PANEL)
                acc_s[mp, :, qs] = acc_s[mp, :, qs] + acc

    bound = qn_ref[(b * pl.num_programs(1) + h) * nt + qi] * kn_ref[(b * pl.num_programs(1) + h) * nt + kt]
    min_dist = jnp.maximum((jnp.abs(qi - kt) - 1) * t + 1, 0).astype(F32)
    headroom = bound - mmin_s[0]
    all_zero = jnp.logical_and(ki > 0, headroom - slope * min_dist <= -FLASH_ZERO_LOG2)
    all_zero = jnp.logical_or(all_zero, skip_ref[step_id(b, h, qi, ki)] == 1)
    safe = jnp.logical_and(ki > 0, headroom <= FLASH_SAFE_LOG2)
    pl.when(jnp.logical_and(safe, jnp.logical_not(all_zero)))(one_pass_tile)
    pl.when(jnp.logical_not(jnp.logical_or(safe, all_zero)))(two_pass_tile)

    @pl.when(ki == pl.num_programs(3) - 1)
    def _():
        lam_init = lami_ref[0]
        lam = (jnp.exp(jnp.sum(lq1_ref[...] * lk1_ref[...], axis=-1, keepdims=True))
               - jnp.exp(jnp.sum(lq2_ref[...] * lk2_ref[...], axis=-1, keepdims=True)) + lam_init)
        o1 = acc_s[0, :FLASH_DV, :] / acc_s[0, FLASH_DV:FLASH_DV + 1, :]
        o2 = acc_s[1, :FLASH_DV, :] / acc_s[1, FLASH_DV:FLASH_DV + 1, :]
        o = o1 - lam * o2
        r = lax.rsqrt(jnp.mean(o * o, axis=0, keepdims=True) + RMS_EPS)
        y = o * r * sub_ref[...] * (1.0 - lam_init)
        o_ref[0] = y.T.astype(o_ref.dtype)


def _split_bf16(x, n):
    parts = []
    for _ in range(n):
        p = x.astype(BF16)
        parts.append(p)
        x = x - p.astype(F32)
    return parts


def alibi_key_columns(slopes, t):
    pos = (jnp.arange(t) % FLASH_CHUNK).astype(F32)
    aug = jnp.stack(_split_bf16(slopes[:, None] * pos[None, :], FLASH_AUG), axis=-1)
    return jnp.concatenate([aug, jnp.zeros((slopes.shape[0], t, FLASH_DV - FLASH_AUG), BF16)], axis=-1)


def diff_attention(qt, proj3d, k_blk0, vt, norms, slopes, lam_init, lq1, lk1, lq2, lk2, subln):
    bsz, nh, _, seq = qt.shape
    t = min(T_FLASH, seq)
    assert seq % t == 0 and t % FLASH_PANEL == 0 and FLASH_PANEL % FLASH_CHUNK == 0
    nt = seq // t
    norms = norms.reshape(bsz, nt, 2, nh).transpose(2, 0, 3, 1)
    qn, kn = norms[0], norms[1]
    tile = jnp.arange(nt)
    kt = (tile[:, None] + tile[None, :]) % nt
    min_dist = jnp.maximum((jnp.abs(tile[:, None] - kt) - 1) * t + 1, 0).astype(F32)
    upper = qn[..., :, None] * jnp.take(kn, kt, axis=-1) - slopes[None, :, None, None] * min_dist
    lower_ref = -(qn * kn)[..., :, None]
    skip = jnp.logical_and(upper - lower_ref <= -FLASH_ZERO_LOG2, kt != tile[:, None])
    kt_cols = [jnp.broadcast_to(kt[:, 0], skip.shape[:-1])]
    for ki in range(1, nt):
        kt_cols.append(jnp.where(skip[..., ki], kt_cols[-1], kt[:, ki]))
    kt_eff = jnp.stack(kt_cols, axis=-1).astype(I32).reshape(-1)
    skip = skip.astype(I32).reshape(-1)

    def step_id(b, h, qi, ki):
        return ((b * nh + h) * nt + qi) * nt + ki

    k_cols = alibi_key_columns(slopes, t)
    dv = FLASH_DV
    dva = FLASH_DV + FLASH_ONES
    r = jnp.arange(FLASH_CHUNK, dtype=F32)
    diag = -slopes[:, None, None] * jnp.abs(r[None, :] - r[:, None])[None]
    diag = jnp.stack([jnp.zeros_like(diag), diag], axis=1)
    smem = pl.BlockSpec(memory_space=pltpu.SMEM)
    vec = lambda n: pl.BlockSpec((1, n), lambda b, h, qi, ki, kte, sk: (0, 0))
    grid_spec = pltpu.PrefetchScalarGridSpec(
        num_scalar_prefetch=2,
        grid=(bsz, nh, nt, nt),
        in_specs=[
            smem, smem, smem, smem,
            pl.BlockSpec((1, 1, dv, t), lambda b, h, qi, ki, kte, sk: (b, h, 0, qi)),
            pl.BlockSpec((1, t, dv), lambda b, h, qi, ki, kte, sk: (b, kte[step_id(b, h, qi, ki)], k_blk0 + h)),
            pl.BlockSpec((1, t, dv), lambda b, h, qi, ki, kte, sk: (h, 0, 0)),
            pl.BlockSpec((1, 1, dv, t), lambda b, h, qi, ki, kte, sk: (b, h, 0, kte[step_id(b, h, qi, ki)])),
            pl.BlockSpec((1, 1, FLASH_CHUNK, FLASH_CHUNK),
                         lambda b, h, qi, ki, kte, sk: (h, (ki == 0).astype(I32), 0, 0)),
            vec(HEAD_DIM), vec(HEAD_DIM), vec(HEAD_DIM), vec(HEAD_DIM),
            pl.BlockSpec((dv, 1), lambda b, h, qi, ki, kte, sk: (0, 0)),
        ],
        out_specs=pl.BlockSpec((1, t, LANES), lambda b, h, qi, ki, kte, sk: (b, qi, h)),
        scratch_shapes=[
            pltpu.VMEM((2, 1, t), F32),
            pltpu.VMEM((2, dva, t), F32),
            pltpu.VMEM((2, t, FLASH_PANEL), F32),
            pltpu.SMEM((1,), F32),
        ],
    )
    return pl.pallas_call(
        functools.partial(_flash_body, t=t, step_id=step_id),
        grid_spec=grid_spec,
        out_shape=jax.ShapeDtypeStruct((bsz, seq, nh * dv), BF16),
        compiler_params=_cparams(("parallel", "parallel", "parallel", "arbitrary")),
        name="diff_attention",
    )(kt_eff, skip, slopes, lam_init, qn.reshape(-1), kn.reshape(-1), qt, proj3d, k_cols, vt, diag,
      lq1, lk1, lq2, lk2, subln)


def _outproj_body(x_ref, oa_ref, ob_ref, oc_ref, wa_ref, wb_ref, wc_ref, g1_ref, sc_ref, sh_ref,
                  n2_ref, wr_ref, br_ref, x1_ref, hp_ref, lg_ref):
    mix = (jnp.dot(oa_ref[...], wa_ref[...], preferred_element_type=F32)
           + jnp.dot(ob_ref[...], wb_ref[...], preferred_element_type=F32)
           + jnp.dot(oc_ref[...], wc_ref[...], preferred_element_type=F32))
    x1 = x_ref[...] + g1_ref[0] * mix
    x1_ref[...] = x1
    h = x1 * _rms_scale(x1) * n2_ref[...]
    h = h * (1.0 + sc_ref[0]) + sh_ref[0]
    half = h.shape[1] // 2
    hp_ref[...] = _pack_halves(h[:, :half], h[:, half:])
    lg_ref[...] = lax.dot_general(wr_ref[...], h, _NT, preferred_element_type=F32,
                                  precision=HIGHEST) + br_ref[...]


def out_projection(x2d, oa, ob, oc, wa, wb, wc, gate1, scale2, shift2, gain2, w_router_t, b_router, seq):
    t, d = x2d.shape
    e = w_router_t.shape[0]
    tm = min(TM_PROJ, seq)
    mod_spec = pl.BlockSpec((1, 1, d), lambda i: ((i * tm) // seq, 0, 0))
    row = lambda w: pl.BlockSpec((tm, w), lambda i: (i, 0))
    full = lambda a: pl.BlockSpec(a.shape, lambda i: (0, 0))
    return pl.pallas_call(
        _outproj_body,
        grid=(t // tm,),
        in_specs=[row(d), row(oa.shape[1]), row(ob.shape[1]), row(oc.shape[1]),
                  full(wa), full(wb), full(wc), mod_spec, mod_spec, mod_spec,
                  pl.BlockSpec((1, d), lambda i: (0, 0)), full(w_router_t),
                  pl.BlockSpec((e, 1), lambda i: (0, 0))],
        out_specs=[row(d), row(d // 2), pl.BlockSpec((e, tm), lambda i: (0, i))],
        out_shape=[jax.ShapeDtypeStruct((t, d), F32),
                   jax.ShapeDtypeStruct((t, d // 2), U32),
                   jax.ShapeDtypeStruct((e, t), F32)],
        compiler_params=_cparams(("parallel",)),
        name="out_projection",
    )(x2d, oa, ob, oc, wa, wb, wc, gate1, scale2, shift2, gain2, w_router_t, b_router)


def _router_body(lg_ref, tri_ref, eidx_ref, gate_ref, rank_ref, cnt_ref, base_s):
    step = pl.program_id(0)

    @pl.when(step == 0)
    def _():
        base_s[...] = jnp.zeros(base_s.shape, F32)

    work = lg_ref[...]
    n_e, tm = work.shape
    eio = lax.broadcasted_iota(I32, (n_e, tm), 0).astype(F32)
    vals, hots = [], []
    for k in range(TOP_K):
        mx = jnp.max(work, axis=0, keepdims=True)
        idx = jnp.min(jnp.where(work == mx, eio, float(n_e)), axis=0, keepdims=True)
        hot = eio == idx
        vals.append(mx)
        hots.append(hot)
        eidx_ref[k:k + 1, :] = idx.astype(I32)
        work = jnp.where(hot, -jnp.inf, work)
    exps = [jnp.exp(v - vals[0]) for v in vals]
    den = functools.reduce(lambda a, b: a + b, exps)
    for k in range(TOP_K):
        gate_ref[k:k + 1, :] = exps[k] / den
    any_hot = functools.reduce(lambda a, b: a + b, [h.astype(F32) for h in hots])
    prefix = jnp.dot(any_hot.astype(BF16), tri_ref[...], preferred_element_type=F32)
    pos = base_s[...] + prefix
    for k in range(TOP_K):
        rank_ref[k:k + 1, :] = jnp.sum(jnp.where(hots[k], pos, 0.0), axis=0, keepdims=True).astype(I32)
    base_s[...] = base_s[...] + jnp.sum(any_hot, axis=1, keepdims=True)
    cnt_ref[...] = base_s[...].astype(I32)


def route(logits_t):
    n_e, t = logits_t.shape
    tm = min(TM_ROUTE, t)
    assert t % tm == 0
    r = jnp.arange(tm)
    tri = (r[:, None] < r[None, :]).astype(BF16)
    out4 = lambda dt: jax.ShapeDtypeStruct((TOP_K, t), dt)
    spec4 = pl.BlockSpec((TOP_K, tm), lambda i: (0, i))
    return pl.pallas_call(
        _router_body,
        grid=(t // tm,),
        in_specs=[pl.BlockSpec((n_e, tm), lambda i: (0, i)),
                  pl.BlockSpec((tm, tm), lambda i: (0, 0))],
        out_specs=[spec4, spec4, spec4, pl.BlockSpec((n_e, 1), lambda i: (0, 0))],
        out_shape=[out4(I32), out4(F32), out4(I32), jax.ShapeDtypeStruct((n_e, 1), I32)],
        scratch_shapes=[pltpu.VMEM((n_e, 1), F32)],
        compiler_params=_cparams(("arbitrary",)),
        name="route",
    )(logits_t, tri)


def _dispatch_body(dest_ref, h_ref, init_ref, xs_ref, sem, *, tm):
    del init_ref

    def issue(tok, carry):
        for k in range(TOP_K):
            pltpu.make_async_copy(h_ref.at[pl.ds(tok, 1)], xs_ref.at[pl.ds(dest_ref[k, tok], 1)], sem).start(priority=k % N_DMA_THREADS)
        return carry

    lax.fori_loop(0, tm, issue, 0, unroll=DMA_ISSUE_UNROLL)
    for k in range(TOP_K):
        pltpu.make_async_copy(h_ref, xs_ref.at[pl.ds(0, tm)], sem).wait()


def dispatch_rows(h_packed, dest, n_rows):
    t, w = h_packed.shape
    tm = min(TM_DISPATCH, t)
    return pl.pallas_call(
        functools.partial(_dispatch_body, tm=tm),
        grid=(t // tm,),
        in_specs=[pl.BlockSpec((TOP_K, tm), lambda i: (0, i), memory_space=pltpu.SMEM),
                  pl.BlockSpec((tm, w), lambda i: (i, 0)),
                  pl.BlockSpec(memory_space=pl.ANY)],
        out_specs=pl.BlockSpec(memory_space=pl.ANY),
        out_shape=jax.ShapeDtypeStruct((n_rows, w), U32),
        input_output_aliases={2: 0},
        scratch_shapes=[pltpu.SemaphoreType.DMA],
        compiler_params=_cparams(("arbitrary",)),
        name="dispatch_rows",
    )(dest, h_packed, jnp.zeros((n_rows, w), U32))


def _ffn_body(be_ref, bv_ref, nu_ref, xs_ref, wg_ref, bg_ref, wu_ref, bu_ref, wd_ref, bd_ref, o_ref,
              wg_s, wu_s, wd_s):
    i = pl.program_id(0)

    @pl.when(i < nu_ref[0])
    def _():
        @pl.when(jnp.logical_or(i == 0, be_ref[i] != be_ref[jnp.maximum(i - 1, 0)]))
        def _():
            wg_s[...] = wg_ref[0, 0].astype(BF16)
            wu_s[...] = wu_ref[0, 0].astype(BF16)
            wd_s[...] = wd_ref[0, 0].astype(BF16)

        xa, xb = _unpack_halves(xs_ref[...])
        blk, half = xa.shape
        valid = lax.broadcasted_iota(I32, (blk, 1), 0) < bv_ref[i]
        xa = jnp.where(valid, xa, 0.0).astype(BF16)
        xb = jnp.where(valid, xb, 0.0).astype(BF16)

        def proj(w_s, b_ref):
            return (jnp.dot(xa, w_s[:half, :], preferred_element_type=F32)
                    + jnp.dot(xb, w_s[half:, :], preferred_element_type=F32) + b_ref[0, 0])

        a = jnp.minimum(proj(wg_s, bg_ref), SWIGLU_LIMIT)
        u = jnp.clip(proj(wu_s, bu_ref), -SWIGLU_LIMIT, SWIGLU_LIMIT)
        hid = a * (1.0 / (1.0 + jnp.exp(-SWIGLU_ALPHA * a))) * (u + 1.0)
        out = jnp.dot(hid.astype(BF16), wd_s[...], preferred_element_type=F32) + bd_ref[0, 0]
        d = out.shape[1] // 2
        o_ref[...] = _pack_halves(out[:, :d], out[:, d:])

    @pl.when(i >= nu_ref[0])
    def _():
        o_ref[...] = jnp.zeros(o_ref.shape, o_ref.dtype)


def expert_ffn(xs, blk_expert, blk_valid, n_used, layer, wg, bg, wu, bu, wd, bd):
    n_rows, w = xs.shape
    _, n_e, d, f = wg.shape
    nb = n_rows // BLK_FFN
    row_map = lambda i, be, bv, nu: (jnp.minimum(i, nu[0] - 1), 0)
    exp_map = lambda i, be, bv, nu: (layer, be[i], 0, 0)
    grid_spec = pltpu.PrefetchScalarGridSpec(
        num_scalar_prefetch=3,
        grid=(nb,),
        in_specs=[pl.BlockSpec((BLK_FFN, w), row_map),
                  pl.BlockSpec((1, 1, d, f), exp_map), pl.BlockSpec((1, 1, 1, f), exp_map),
                  pl.BlockSpec((1, 1, d, f), exp_map), pl.BlockSpec((1, 1, 1, f), exp_map),
                  pl.BlockSpec((1, 1, f, d), exp_map), pl.BlockSpec((1, 1, 1, d), exp_map)],
        out_specs=pl.BlockSpec((BLK_FFN, d // 2), lambda i, be, bv, nu: (i, 0)),
        scratch_shapes=[pltpu.VMEM((d, f), BF16), pltpu.VMEM((d, f), BF16), pltpu.VMEM((f, d), BF16)],
    )
    return pl.pallas_call(
        _ffn_body,
        grid_spec=grid_spec,
        out_shape=jax.ShapeDtypeStruct((n_rows, d // 2), U32),
        compiler_params=_cparams(("arbitrary",)),
        name="expert_ffn",
    )(blk_expert, blk_valid, n_used, xs, wg, bg, wu, bu, wd, bd)


def _combine_body(dest_ref, x_ref, gt_ref, g2_ref, nf_ref, ys_ref, o_ref, buf, sem, *, tm, final):
    def issue(tok, carry):
        for k in range(TOP_K):
            pltpu.make_async_copy(ys_ref.at[pl.ds(dest_ref[k, tok], 1)], buf.at[k, pl.ds(tok, 1)], sem).start(priority=k % N_DMA_THREADS)
        return carry

    lax.fori_loop(0, tm, issue, 0, unroll=DMA_ISSUE_UNROLL)
    for k in range(TOP_K):
        pltpu.make_async_copy(ys_ref.at[pl.ds(0, tm)], buf.at[k], sem).wait()
    gt = gt_ref[...]
    lo = hi = None
    for k in range(TOP_K):
        a, b = _unpack_halves(buf[k])
        g = gt[:, k:k + 1]
        lo = g * a if lo is None else lo + g * a
        hi = g * b if hi is None else hi + g * b
    half = lo.shape[1]
    g2 = g2_ref[0]
    x_lo = x_ref[:, :half] + g2[:, :half] * lo
    x_hi = x_ref[:, half:] + g2[:, half:] * hi
    if final:
        ms = (jnp.sum(x_lo * x_lo, axis=-1, keepdims=True)
              + jnp.sum(x_hi * x_hi, axis=-1, keepdims=True)) / (2 * half)
        r = lax.rsqrt(ms + RMS_EPS)
        x_lo = x_lo * r * nf_ref[:, :half]
        x_hi = x_hi * r * nf_ref[:, half:]
    o_ref[:, :half] = x_lo
    o_ref[:, half:] = x_hi


def combine(x1, ys, dest, gates_t, gate2, norm_f, seq, final):
    t, d = x1.shape
    tm = min(TM_COMBINE, seq)
    return pl.pallas_call(
        functools.partial(_combine_body, tm=tm, final=final),
        grid=(t // tm,),
        in_specs=[pl.BlockSpec((TOP_K, tm), lambda i: (0, i), memory_space=pltpu.SMEM),
                  pl.BlockSpec((tm, d), lambda i: (i, 0)),
                  pl.BlockSpec((tm, TOP_K), lambda i: (i, 0)),
                  pl.BlockSpec((1, 1, d), lambda i: ((i * tm) // seq, 0, 0)),
                  pl.BlockSpec((1, d), lambda i: (0, 0)),
                  pl.BlockSpec(memory_space=pl.ANY)],
        out_specs=pl.BlockSpec((tm, d), lambda i: (i, 0)),
        out_shape=jax.ShapeDtypeStruct((t, d), F32),
        scratch_shapes=[pltpu.VMEM((TOP_K, tm, d // 2), U32), pltpu.SemaphoreType.DMA],
        compiler_params=_cparams(("arbitrary",)),
        name="combine",
    )(dest, x1, gates_t, gate2, norm_f, ys)


def _combine_project_body(dcur_ref, dnext_ref, x_ref, gt_ref, g2_ref, sc_ref, sh_ref, gn_ref, w_ref, ys_ref,
                          x2_ref, proj_ref, nrm_ref, buf0, buf1, sem, *, tm, norm_cols):
    i = pl.program_id(0)
    bufs = (buf0, buf1)

    def start(dref, col, tok, k, s):
        pltpu.make_async_copy(ys_ref.at[pl.ds(dref[k, col + tok], 1)], bufs[s].at[k, pl.ds(tok, 1)], sem.at[s]).start(priority=k % N_DMA_THREADS)

    def wait(s):
        for k in range(TOP_K):
            pltpu.make_async_copy(ys_ref.at[pl.ds(0, tm)], bufs[s].at[k], sem.at[s]).wait()

    def issue_inline(dref, col, s):
        for tok in range(tm):
            for k in range(TOP_K):
                start(dref, col, tok, k, s)

    def compute(s):
        rows = slice(s * tm, (s + 1) * tm)
        gt = gt_ref[rows, :]
        lo = hi = None
        for k in range(TOP_K):
            a, b = _unpack_halves(bufs[s][k])
            g = gt[:, k:k + 1]
            lo = g * a if lo is None else lo + g * a
            hi = g * b if hi is None else hi + g * b
        half = lo.shape[1]
        g2 = g2_ref[0]
        x2 = jnp.concatenate([x_ref[rows, :half] + g2[:, :half] * lo, x_ref[rows, half:] + g2[:, half:] * hi],
                             axis=1)
        x2_ref[rows, :] = x2
        _project_rows(x2, sc_ref[0], sh_ref[0], gn_ref[...], w_ref, proj_ref.at[rows, :], nrm_ref.at[s:s + 1],
                      norm_cols)

    @pl.when(i == 0)
    def _():
        def issue(tok, carry):
            for k in range(TOP_K):
                start(dcur_ref, 0, tok, k, 0)
            return carry

        lax.fori_loop(0, tm, issue, 0, unroll=DMA_ISSUE_UNROLL)

    wait(0)
    issue_inline(dcur_ref, tm, 1)
    compute(0)
    wait(1)
    issue_inline(dnext_ref, 0, 0)
    compute(1)

    @pl.when(i == pl.num_programs(0) - 1)
    def _():
        wait(0)


def combine_project(x1, ys, dest, gates_t, gate2, scale, shift, gain, w_bf16, seq, norm_cols):
    t, d = x1.shape
    n = w_bf16.shape[1]
    tm = min(TM_COMBINE, seq // 2)
    nblocks = t // tm
    nsteps = nblocks // 2
    assert t % (2 * tm) == 0 and seq % (2 * tm) == 0
    mod_spec = pl.BlockSpec((1, 1, d), lambda i: ((i * 2 * tm) // seq, 0, 0))
    return pl.pallas_call(
        functools.partial(_combine_project_body, tm=tm, norm_cols=norm_cols),
        grid=(nsteps,),
        in_specs=[pl.BlockSpec((TOP_K, 2 * tm), lambda i: (0, i), memory_space=pltpu.SMEM),
                  pl.BlockSpec((TOP_K, tm), lambda i: (0, jnp.minimum(2 * i + 2, nblocks - 1)),
                               memory_space=pltpu.SMEM),
                  pl.BlockSpec((2 * tm, d), lambda i: (i, 0)),
                  pl.BlockSpec((2 * tm, TOP_K), lambda i: (i, 0)),
                  mod_spec, mod_spec, mod_spec,
                  pl.BlockSpec((1, d), lambda i: (0, 0)),
                  pl.BlockSpec((d, n), lambda i: (0, 0)),
                  pl.BlockSpec(memory_space=pl.ANY)],
        out_specs=[pl.BlockSpec((2 * tm, d), lambda i: (i, 0)),
                   pl.BlockSpec((2 * tm, n), lambda i: (i, 0)),
                   pl.BlockSpec((2, 1, len(norm_cols)), lambda i: (i, 0, 0))],
        out_shape=[jax.ShapeDtypeStruct((t, d), F32),
                   jax.ShapeDtypeStruct((t, n), BF16),
                   jax.ShapeDtypeStruct((nblocks, 1, len(norm_cols)), F32)],
        scratch_shapes=[pltpu.VMEM((TOP_K, tm, d // 2), U32), pltpu.VMEM((TOP_K, tm, d // 2), U32),
                        pltpu.SemaphoreType.DMA((2,))],
        compiler_params=_cparams(("arbitrary",)),
        name="combine_project",
    )(dest, dest, x1, gates_t, gate2, scale, shift, gain, w_bf16, ys)


def _alibi_slopes(n):
    return jnp.exp2(-8.0 * jnp.arange(1, n + 1, dtype=F32) / n)


def _routing_tables(counts, eidx, rank, nb):
    counts = counts.reshape(-1)
    padded = (counts + BLK_FFN - 1) // BLK_FFN * BLK_FFN
    pad_end = jnp.cumsum(padded)
    pad_start = pad_end - padded
    e_ids = jnp.arange(N_EXPERTS, dtype=I32)[:, None, None]
    dest = rank + jnp.sum(jnp.where(eidx[None] == e_ids, pad_start[:, None, None], 0), axis=0)
    blk_row0 = jnp.arange(nb, dtype=I32) * BLK_FFN
    blk_expert = jnp.minimum(jnp.sum(pad_end[None, :] <= blk_row0[:, None], axis=1), N_EXPERTS - 1).astype(I32)
    blk_valid = jnp.clip(counts[blk_expert] - (blk_row0 - pad_start[blk_expert]), 0, BLK_FFN).astype(I32)
    n_used = (pad_end[-1:] // BLK_FFN).astype(I32)
    return dest.astype(I32), blk_expert, blk_valid, n_used


def kernel(x, c, w_ada, b_ada, norm1, w_in, sinks, lambda_q1, lambda_k1, lambda_q2, lambda_k2, subln,
           w_out, norm2, w_router, b_router, w_gate, b_gate, w_up, b_up, w_down, b_down, norm_f):
    bsz, seq, d = x.shape
    depth = w_ada.shape[0]
    t = bsz * seq
    wa_w, wb_w = N_HEADS_A * HEAD_DIM, N_HEADS_B * HEAD_DIM
    kvb_w = N_KV_B * HEAD_DIM
    wc_w = N_HEADS_C * 2 * HEAD_DIM
    off = [0]
    for wdt in (wa_w, wa_w, wa_w, wb_w, kvb_w, kvb_w, wc_w, wc_w, wc_w):
        off.append(off[-1] + wdt)
    blk = [o // LANES for o in off]
    norm_cols = tuple(off[i] + hh * FLASH_DV for i in (6, 7) for hh in range(N_HEADS_C))

    group = N_HEADS_B // N_KV_B
    b_perm = [kv * group + g for g in range(group) for kv in range(N_KV_B)]
    col_perm = jnp.concatenate([jnp.arange(h * HEAD_DIM, (h + 1) * HEAD_DIM) for h in b_perm])
    q_scale = jnp.ones((off[-1],), F32)
    for s0, s1, extra in ((off[0], off[1], 1.0), (off[3], off[4], 1.0), (off[6], off[7], LOG2E)):
        q_scale = q_scale.at[s0:s1].set(HEAD_DIM ** -0.5 * extra)

    slopes_a = _alibi_slopes(N_HEADS_A)
    slopes_b = _alibi_slopes(N_HEADS_B)[jnp.array(b_perm)]
    slopes_c = _alibi_slopes(N_HEADS_C) * LOG2E

    mod_all = ada_modulation(c, w_ada, b_ada)
    x2d = x.reshape(t, d)
    nb = (t * TOP_K) // BLK_FFN + N_EXPERTS
    n_rows = nb * BLK_FFN

    def layer_mod(layer):
        mod = mod_all[layer].reshape(bsz, N_MOD, 1, d)
        return [mod[:, i] for i in range(N_MOD)]

    def in_weights(layer):
        w_l = w_in[layer] * q_scale[None, :]
        w_l = jnp.concatenate([w_l[:, :off[3]], w_l[:, off[3]:off[4]][:, col_perm], w_l[:, off[4]:]], axis=1)
        return w_l.astype(BF16)

    proj = blk_norms = None
    for layer in range(depth):
        sh1, sc1, g1, sh2, sc2, g2 = layer_mod(layer)
        if layer == 0:
            proj, blk_norms = in_projection(x2d, sc1, sh1, norm1[layer].reshape(1, d), in_weights(layer), seq,
                                            norm_cols)
        proj3d = proj.reshape(bsz, seq, off[-1])

        n_blk_a = wa_w // LANES
        o_a = window_attention(proj3d, blk[0], blk[1], blk[2], 1, n_blk_a, slopes_a,
                               jnp.full((n_blk_a, 2), -jnp.inf, F32), DILATED_PAIRS).reshape(t, wa_w)

        sink_l = sinks[layer].astype(F32)[jnp.array(b_perm)].reshape(group, N_KV_B)
        o_b = window_attention(proj3d, blk[3], blk[4], blk[5], 0, group, slopes_b, sink_l,
                               ((2 * SWA_HALF, 1),)).reshape(t, wb_w)

        lam_init = jnp.full((1,), 0.8 - 0.6 * math.exp(-0.3 * layer), F32)
        heads_t = lambda i: (proj3d[:, :, off[i]:off[i + 1]].reshape(bsz, seq, N_HEADS_C, FLASH_DV)
                             .transpose(0, 2, 3, 1))
        t_flash = min(T_FLASH, seq)
        norms = blk_norms.reshape(bsz, seq // t_flash, -1, 2 * N_HEADS_C).max(axis=2, keepdims=True)
        o_c = diff_attention(heads_t(6), proj3d, blk[7], heads_t(8), norms, slopes_c, lam_init,
                             lambda_q1[layer].reshape(1, -1), lambda_k1[layer].reshape(1, -1),
                             lambda_q2[layer].reshape(1, -1), lambda_k2[layer].reshape(1, -1),
                             subln[layer].reshape(-1, 1)).reshape(t, wc_w)

        wo = w_out[layer]
        wo_b = wo[wa_w:wa_w + wb_w].reshape(N_HEADS_B, HEAD_DIM, d)[jnp.array(b_perm)].reshape(wb_w, d)
        x1, h_packed, logits_t = out_projection(
            x2d, o_a, o_b, o_c, wo[:wa_w].astype(BF16), wo_b.astype(BF16), wo[wa_w + wb_w:].astype(BF16),
            g1, sc2, sh2, norm2[layer].reshape(1, d), w_router[layer].T, b_router[layer].reshape(-1, 1), seq)

        eidx, gates, rank, counts = route(logits_t)
        dest, blk_expert, blk_valid, n_used = _routing_tables(counts, eidx, rank, nb)
        xs = dispatch_rows(h_packed, dest, n_rows)
        ys = expert_ffn(xs, blk_expert, blk_valid, n_used, layer,
                        w_gate, b_gate[:, :, None, :], w_up, b_up[:, :, None, :], w_down, b_down[:, :, None, :])
        if layer == depth - 1:
            x2d = combine(x1, ys, dest, gates.T, g2, norm_f.reshape(1, d), seq, True)
        else:
            nsh1, nsc1 = layer_mod(layer + 1)[:2]
            x2d, proj, blk_norms = combine_project(x1, ys, dest, gates.T, g2, nsc1, nsh1,
                                                   norm1[layer + 1].reshape(1, d), in_weights(layer + 1), seq,
                                                   norm_cols)

    return x2d.reshape(bsz, seq, d)
```

```python
import functools
import math

import jax
import jax.numpy as jnp
from jax import lax
from jax.experimental import pallas as pl
from jax.experimental.pallas import tpu as pltpu

F32 = jnp.float32
BF16 = jnp.bfloat16
I32 = jnp.int32
U32 = jnp.uint32

HEAD_DIM = 64
DILATED_PAIRS = ((128, 1), (512, 4), (2048, 16))
N_HEADS_A = 4
N_HEADS_B = 4
N_KV_B = 2
SWA_HALF = 128
N_HEADS_C = 4
N_EXPERTS = 32
TOP_K = 4
SWIGLU_LIMIT = 7.0
SWIGLU_ALPHA = 1.702
RMS_EPS = 1e-6
N_MOD = 6

LANES = 128
VMEM_LIMIT_BYTES = 56 * 1024 * 1024

TM_PROJ = 512
TQ_BAND = 256
T_FLASH = 1024
FLASH_PANEL = 512
FLASH_CHUNK = 256
FLASH_SUB = 128
FLASH_DV = 2 * HEAD_DIM
FLASH_ONES = 16
FLASH_AUG = 3
FLASH_OWN_LOG2 = 100.0
FLASH_ZERO_LOG2 = 150.0
FLASH_LOOKAHEAD = 2
FLASH_SAFE_LOG2 = 60.0
LOG2E = math.log2(math.e)
TM_ROUTE = 512
TM_DISPATCH = 512
BLK_FFN = 512
TM_COMBINE = 256
DMA_ISSUE_UNROLL = 8
N_DMA_THREADS = 2

HIGHEST = lax.Precision.HIGHEST
_NT = (((1,), (1,)), ((), ()))


def _cparams(sem):
    return pltpu.CompilerParams(dimension_semantics=sem, vmem_limit_bytes=VMEM_LIMIT_BYTES)


def _pack_halves(a, b):
    ua = lax.bitcast_convert_type(a.astype(BF16).astype(F32), U32)
    ub = lax.bitcast_convert_type(b.astype(BF16).astype(F32), U32)
    return ua | (ub >> 16)


def _unpack_halves(w):
    a = lax.bitcast_convert_type(w & jnp.uint32(0xFFFF0000), F32)
    b = lax.bitcast_convert_type(w << 16, F32)
    return a, b


def _rms_scale(x):
    return lax.rsqrt(jnp.mean(x * x, axis=-1, keepdims=True) + RMS_EPS)


def _ada_body(c_ref, w_ref, b_ref, o_ref):
    c = c_ref[...]
    cond = c / (1.0 + jnp.exp(-c))
    o_ref[0] = jnp.dot(cond, w_ref[0], preferred_element_type=F32, precision=HIGHEST) + b_ref[0]


def ada_modulation(c, w_ada, b_ada):
    depth, d, n = w_ada.shape
    bsz = c.shape[0]
    tn = 1536
    assert n % tn == 0
    return pl.pallas_call(
        _ada_body,
        grid=(depth, n // tn),
        in_specs=[
            pl.BlockSpec((bsz, d), lambda l, j: (0, 0)),
            pl.BlockSpec((1, d, tn), lambda l, j: (l, 0, j)),
            pl.BlockSpec((1, 1, tn), lambda l, j: (l, 0, j)),
        ],
        out_specs=pl.BlockSpec((1, bsz, tn), lambda l, j: (l, 0, j)),
        out_shape=jax.ShapeDtypeStruct((depth, bsz, n), F32),
        compiler_params=_cparams(("parallel", "parallel")),
        name="ada_modulation",
    )(c, w_ada, b_ada.reshape(depth, 1, n))


def _project_rows(x, sc, sh, g, w_ref, o_ref, n_ref, norm_cols):
    h = x * _rms_scale(x) * g
    h = h * (1.0 + sc) + sh
    out = jnp.dot(h.astype(BF16), w_ref[...], preferred_element_type=F32).astype(BF16)
    o_ref[...] = out
    lane = lax.broadcasted_iota(I32, (1, len(norm_cols)), 1)
    acc = jnp.zeros((1, len(norm_cols)), F32)
    for j, c0 in enumerate(norm_cols):
        v = out[:, c0:c0 + FLASH_DV].astype(F32)
        n2 = jnp.max(jnp.sum(v * v, axis=1, keepdims=True), axis=0, keepdims=True)
        acc = jnp.where(lane == j, jnp.sqrt(n2), acc)
    n_ref[0] = acc


def _inproj_body(x_ref, sc_ref, sh_ref, g_ref, w_ref, o_ref, n_ref, *, norm_cols):
    _project_rows(x_ref[...], sc_ref[0], sh_ref[0], g_ref[...], w_ref, o_ref, n_ref, norm_cols)


def in_projection(x2d, scale, shift, gain, w_bf16, seq, norm_cols):
    t, d = x2d.shape
    n = w_bf16.shape[1]
    tm = min(TM_PROJ, seq)
    assert seq % tm == 0 and t % tm == 0
    mod_spec = pl.BlockSpec((1, 1, d), lambda i: ((i * tm) // seq, 0, 0))
    return pl.pallas_call(
        functools.partial(_inproj_body, norm_cols=norm_cols),
        grid=(t // tm,),
        in_specs=[
            pl.BlockSpec((tm, d), lambda i: (i, 0)),
            mod_spec, mod_spec,
            pl.BlockSpec((1, d), lambda i: (0, 0)),
            pl.BlockSpec((d, n), lambda i: (0, 0)),
        ],
        out_specs=[pl.BlockSpec((tm, n), lambda i: (i, 0)),
                   pl.BlockSpec((1, 1, len(norm_cols)), lambda i: (i, 0, 0))],
        out_shape=[jax.ShapeDtypeStruct((t, n), BF16),
                   jax.ShapeDtypeStruct((t // tm, 1, len(norm_cols)), F32)],
        compiler_params=_cparams(("parallel",)),
        name="in_projection",
    )(x2d, scale, shift, gain, w_bf16)


def _offset_bias_body(c_ref, slope_ref, o_ref, *, pairs):
    vi = pl.program_id(0)
    hd = pl.program_id(1)
    tq, win = o_ref.shape[2], o_ref.shape[3]
    off = (lax.broadcasted_iota(I32, (tq, win), 1) - lax.broadcasted_iota(I32, (tq, win), 0)) + c_ref[vi]
    dist = jnp.abs(off)
    mult = jnp.zeros((tq, win), F32)
    for window, dil in pairs:
        hit = jnp.logical_and(dist <= window // 2, (dist & (dil - 1)) == 0)
        mult = mult + jnp.where(hit, 1.0, 0.0)
    o_ref[0, 0] = jnp.log(mult) - slope_ref[hd] * dist.astype(F32)


def _window_variants(seq, tq, reach):
    win = tq + 2 * reach
    nq, edge = seq // tq, -(-reach // tq)
    assert seq % tq == 0 and reach % 16 == 0 and win <= seq and nq >= 2 * edge + 1
    rel = [min(max(i * tq - reach, 0), seq - win) - i * tq for i in range(nq)]
    assert all(r == -reach for r in rel[edge:nq - edge])
    starts = rel[:edge] + [-reach] + rel[nq - edge:]

    def variant(i):
        return jnp.where(i < edge, i, jnp.where(i >= nq - edge, i - (nq - 2 * edge - 1), edge))

    return win, starts, variant


def _window_body(c_ref, sink_ref, q_ref, k_ref, v_ref, tab_ref, o_ref, *, tq, win, variant):
    j = pl.program_id(1)
    i = pl.program_id(2)
    ks = pl.multiple_of(i * tq + c_ref[variant(i)], 16)
    q = q_ref[0]
    k = k_ref[0, pl.ds(ks, win), :]
    v = v_ref[0, pl.ds(ks, win), :]
    lane_lo = lax.broadcasted_iota(I32, (1, LANES), 1) < HEAD_DIM
    outs = []
    for hf in range(2):
        qm = jnp.where(lane_lo if hf == 0 else jnp.logical_not(lane_lo), q, jnp.zeros_like(q))
        s = lax.dot_general(qm, k, _NT, preferred_element_type=F32) + tab_ref[0, hf]
        sink = sink_ref[j, hf]
        m = jnp.maximum(jnp.max(s, axis=-1, keepdims=True), sink)
        p = jnp.exp(s - m)
        l = jnp.sum(p, axis=-1, keepdims=True) + jnp.exp(sink - m)
        outs.append(jnp.dot(p.astype(BF16), v, preferred_element_type=F32) / l)
    o_ref[0] = jnp.where(lane_lo, outs[0], outs[1]).astype(o_ref.dtype)


def window_attention(proj3d, q_blk0, k_blk0, v_blk0, kv_step, n_blk, slopes, sinks, pairs):
    bsz, seq, _ = proj3d.shape
    assert all(d & (d - 1) == 0 for _, d in pairs)
    reach = max(w // 2 for w, _ in pairs)
    tq = TQ_BAND
    win, starts, variant = _window_variants(seq, tq, reach)
    starts = jnp.array(starts, I32)
    smem = pl.BlockSpec(memory_space=pltpu.SMEM)
    tables = pl.pallas_call(
        functools.partial(_offset_bias_body, pairs=pairs),
        grid=(starts.shape[0], 2 * n_blk),
        in_specs=[smem, smem],
        out_specs=pl.BlockSpec((1, 1, tq, win), lambda vi, hd: (vi, hd, 0, 0)),
        out_shape=jax.ShapeDtypeStruct((starts.shape[0], 2 * n_blk, tq, win), F32),
        compiler_params=_cparams(("parallel", "parallel")),
        name="offset_bias_tables",
    )(starts, slopes)
    return pl.pallas_call(
        functools.partial(_window_body, tq=tq, win=win, variant=variant),
        grid=(bsz, n_blk, seq // tq),
        in_specs=[
            smem, smem,
            pl.BlockSpec((1, tq, LANES), lambda b, j, i: (b, i, q_blk0 + j)),
            pl.BlockSpec((1, seq, LANES), lambda b, j, i: (b, 0, k_blk0 + kv_step * j)),
            pl.BlockSpec((1, seq, LANES), lambda b, j, i: (b, 0, v_blk0 + kv_step * j)),
            pl.BlockSpec((1, 2, tq, win), lambda b, j, i: (variant(i), j, 0, 0)),
        ],
        out_specs=pl.BlockSpec((1, tq, LANES), lambda b, j, i: (b, i, j)),
        out_shape=jax.ShapeDtypeStruct((bsz, seq, n_blk * LANES), BF16),
        compiler_params=_cparams(("parallel", "parallel", "parallel")),
        name="window_attention",
    )(starts, sinks, proj3d, proj3d, proj3d, tables)


def _flash_body(kte_ref, skip_ref, slope_ref, lami_ref, qn_ref, kn_ref, qt_ref, k_ref, kaug_ref, vt_ref, diag_ref,
                lq1_ref, lk1_ref, lq2_ref, lk2_ref, sub_ref, o_ref, m_s, acc_s, s_scr, mmin_s, *, t, step_id):
    b = pl.program_id(0)
    h = pl.program_id(1)
    qi = pl.program_id(2)
    ki = pl.program_id(3)
    nt = pl.num_programs(3)
    kt = lax.rem(qi + ki, nt)

    @pl.when(ki == 0)
    def _():
        m_s[...] = jnp.full(m_s.shape, -jnp.inf, F32)
        acc_s[...] = jnp.zeros(acc_s.shape, F32)
        mmin_s[0] = jnp.float32(0.0)

    slope = slope_ref[h]
    n_chunk = t // FLASH_CHUNK
    n_panel = t // FLASH_PANEL
    sub_per_panel = FLASH_PANEL // FLASH_CHUNK
    row = lax.broadcasted_iota(I32, (FLASH_DV, 1), 0)
    row_lo = row < HEAD_DIM
    row_aug = lax.broadcasted_iota(I32, (FLASH_ONES, 1), 0) < FLASH_AUG
    lane_pos = lax.broadcasted_iota(I32, (1, FLASH_CHUNK), 1).astype(F32)
    ones_rows = jnp.ones((FLASH_ONES, FLASH_CHUNK), BF16)

    def with_sign(qtm, sign):
        aug = jnp.where(row_aug, sign, 0.0).astype(BF16)
        pad = jnp.zeros((FLASH_DV - FLASH_ONES, sign.shape[1]), BF16)
        return jnp.concatenate([qtm, aug, pad], axis=0)

    def keys(kc):
        rows = slice(kc * FLASH_CHUNK, (kc + 1) * FLASH_CHUNK)
        return jnp.concatenate([k_ref[0, rows, :], kaug_ref[0, rows, :]], axis=1)

    def values(kc):
        return jnp.concatenate([vt_ref[0, 0, :, kc * FLASH_CHUNK:(kc + 1) * FLASH_CHUNK], ones_rows], axis=0)
    blocks = [(qp, mp) for qp in range(n_panel) for mp in range(2)]

    def fold8(x, op):
        return functools.reduce(op, [x[r:r + 8] for r in range(0, x.shape[0], 8)])

    def pair_terms(qp, kc):
        signs, shifts = [], []
        for j in range(sub_per_panel):
            gap = (qi * n_chunk + qp * sub_per_panel + j) - (kt * n_chunk + kc)
            sign = jnp.sign(gap).astype(F32)
            signs.append(jnp.full((1, FLASH_CHUNK), sign, F32))
            shifts.append(-slope * ((jnp.abs(gap) * FLASH_CHUNK).astype(F32) + sign * lane_pos))
        return jnp.concatenate(signs, axis=1), jnp.concatenate(shifts, axis=1)

    def masked_queries(bi):
        qp, mp = blocks[bi]
        qt = qt_ref[0, 0, :, qp * FLASH_PANEL:(qp + 1) * FLASH_PANEL]
        return jnp.where(row_lo if mp == 0 else jnp.logical_not(row_lo), qt, jnp.zeros_like(qt))

    def score_chunk(bi, kc, qtm, mx):
        qp, _ = blocks[bi]
        k0 = kc * FLASH_CHUNK
        sign, lane_shift = pair_terms(qp, kc)
        s = jnp.dot(keys(kc), with_sign(qtm, sign), preferred_element_type=F32)
        j = kc - qp * sub_per_panel
        if 0 <= j < sub_per_panel:
            cols = [s[:, i * FLASH_CHUNK:(i + 1) * FLASH_CHUNK] for i in range(sub_per_panel)]
            cols[j] = cols[j] + diag_ref[0, 0]
            s = jnp.concatenate(cols, axis=1)
        s_scr[bi % 2, k0:k0 + FLASH_CHUNK, :] = s
        cm = fold8(s, jnp.maximum) + lane_shift
        return cm if mx is None else jnp.maximum(mx, cm)

    def prob_chunk(bi, kc, m_new, acc):
        qp, _ = blocks[bi]
        k0 = kc * FLASH_CHUNK
        shift = m_new - pair_terms(qp, kc)[1]
        parts = []
        for r0 in range(k0, k0 + FLASH_CHUNK, FLASH_SUB):
            parts.append(jnp.exp2((s_scr[bi % 2, r0:r0 + FLASH_SUB, :] - shift).astype(BF16)))
        pv = jnp.dot(values(kc), jnp.concatenate(parts, axis=0), preferred_element_type=F32)
        return pv if acc is None else acc + pv

    def two_pass_tile():
        qtm_next = masked_queries(0)
        mx_next = None
        for kc in range(n_chunk):
            mx_next = score_chunk(0, kc, qtm_next, mx_next)
        for bi, (qp, mp) in enumerate(blocks):
            qs = slice(qp * FLASH_PANEL, (qp + 1) * FLASH_PANEL)
            m_old = m_s[mp, :, qs]
            m_new = jnp.maximum(m_old, jnp.max(mx_next, axis=0, keepdims=True))
            alpha = jnp.exp2(m_old - m_new)
            acc = None
            has_next = bi + 1 < len(blocks)
            if has_next:
                qtm_next = masked_queries(bi + 1)
            mx_next = None
            for kc in range(n_chunk):
                if has_next:
                    mx_next = score_chunk(bi + 1, kc, qtm_next, mx_next)
                acc = prob_chunk(bi, kc, m_new, acc)
            acc_s[mp, :, qs] = alpha * acc_s[mp, :, qs] + acc
            m_s[mp, :, qs] = m_new
        mmin_s[0] = jnp.min(m_s[...])

    def one_pass_tile(diagonal):
        items = [(bi, kc) for bi in range(len(blocks)) for kc in range(n_chunk)]

        def scores(bi, kc):
            qp, mp = blocks[bi]
            sign, lane_shift = pair_terms(qp, kc)
            s = jnp.dot(keys(kc), with_sign(masked_queries(bi), sign), preferred_element_type=F32)
            j = kc - qp * sub_per_panel
            if diagonal and 0 <= j < sub_per_panel:
                cols = [s[:, c * FLASH_CHUNK:(c + 1) * FLASH_CHUNK] for c in range(sub_per_panel)]
                cols[j] = cols[j] + diag_ref[0, 0]
                s = jnp.concatenate(cols, axis=1)
            return s, m_s[mp, :, qp * FLASH_PANEL:(qp + 1) * FLASH_PANEL] - lane_shift

        ahead = [scores(*items[i]) for i in range(FLASH_LOOKAHEAD)]
        acc = None
        for n, (bi, kc) in enumerate(items):
            s, shift = ahead.pop(0)
            if n + FLASH_LOOKAHEAD < len(items):
                ahead.append(scores(*items[n + FLASH_LOOKAHEAD]))
            p = jnp.exp2(s - shift).astype(BF16)
            pv = jnp.dot(values(kc), p, preferred_element_type=F32)
            acc = pv if kc == 0 else acc + pv
            if kc == n_chunk - 1:
                qp, mp = blocks[bi]
                qs = slice(qp * FLASH_PANEL, (qp + 1) * FLASH_PANEL)
                acc_s[mp, :, qs] = acc_s[mp, :, qs] + acc

    bound = qn_ref[(b * pl.num_programs(1) + h) * nt + qi] * kn_ref[(b * pl.num_programs(1) + h) * nt + kt]
    min_dist = jnp.maximum((jnp.abs(qi - kt) - 1) * t + 1, 0).astype(F32)
    headroom = bound - mmin_s[0]
    all_zero = jnp.logical_and(ki > 0, headroom - slope * min_dist <= -FLASH_ZERO_LOG2)
    all_zero = jnp.logical_or(all_zero, skip_ref[step_id(b, h, qi, ki)] == 1)
    safe = jnp.logical_and(ki > 0, headroom <= FLASH_SAFE_LOG2)
    own_one_pass = jnp.logical_and(ki == 0, 2.0 * bound <= FLASH_OWN_LOG2)

    @pl.when(own_one_pass)
    def _():
        m_s[...] = jnp.zeros(m_s.shape, F32) + bound
        mmin_s[0] = bound
        one_pass_tile(True)

    pl.when(jnp.logical_and(safe, jnp.logical_not(all_zero)))(functools.partial(one_pass_tile, False))
    pl.when(jnp.logical_not(jnp.logical_or(jnp.logical_or(safe, all_zero), own_one_pass)))(two_pass_tile)

    @pl.when(ki == pl.num_programs(3) - 1)
    def _():
        lam_init = lami_ref[0]
        lam = (jnp.exp(jnp.sum(lq1_ref[...] * lk1_ref[...], axis=-1, keepdims=True))
               - jnp.exp(jnp.sum(lq2_ref[...] * lk2_ref[...], axis=-1, keepdims=True)) + lam_init)
        o1 = acc_s[0, :FLASH_DV, :] / acc_s[0, FLASH_DV:FLASH_DV + 1, :]
        o2 = acc_s[1, :FLASH_DV, :] / acc_s[1, FLASH_DV:FLASH_DV + 1, :]
        o = o1 - lam * o2
        r = lax.rsqrt(jnp.mean(o * o, axis=0, keepdims=True) + RMS_EPS)
        y = o * r * sub_ref[...] * (1.0 - lam_init)
        o_ref[0] = y.T.astype(o_ref.dtype)


def _split_bf16(x, n):
    parts = []
    for _ in range(n):
        p = x.astype(BF16)
        parts.append(p)
        x = x - p.astype(F32)
    return parts


def alibi_key_columns(slopes, t):
    pos = (jnp.arange(t) % FLASH_CHUNK).astype(F32)
    aug = jnp.stack(_split_bf16(slopes[:, None] * pos[None, :], FLASH_AUG), axis=-1)
    return jnp.concatenate([aug, jnp.zeros((slopes.shape[0], t, FLASH_DV - FLASH_AUG), BF16)], axis=-1)


def diff_attention(qt, proj3d, k_blk0, vt, norms, slopes, lam_init, lq1, lk1, lq2, lk2, subln):
    bsz, nh, _, seq = qt.shape
    t = min(T_FLASH, seq)
    assert seq % t == 0 and t % FLASH_PANEL == 0 and FLASH_PANEL % FLASH_CHUNK == 0
    nt = seq // t
    norms = norms.reshape(bsz, nt, 2, nh).transpose(2, 0, 3, 1)
    qn, kn = norms[0], norms[1]
    tile = jnp.arange(nt)
    kt = (tile[:, None] + tile[None, :]) % nt
    min_dist = jnp.maximum((jnp.abs(tile[:, None] - kt) - 1) * t + 1, 0).astype(F32)
    upper = qn[..., :, None] * jnp.take(kn, kt, axis=-1) - slopes[None, :, None, None] * min_dist
    lower_ref = -(qn * kn)[..., :, None]
    skip = jnp.logical_and(upper - lower_ref <= -FLASH_ZERO_LOG2, kt != tile[:, None])
    kt_cols = [jnp.broadcast_to(kt[:, 0], skip.shape[:-1])]
    for ki in range(1, nt):
        kt_cols.append(jnp.where(skip[..., ki], kt_cols[-1], kt[:, ki]))
    kt_eff = jnp.stack(kt_cols, axis=-1).astype(I32).reshape(-1)
    skip = skip.astype(I32).reshape(-1)

    def step_id(b, h, qi, ki):
        return ((b * nh + h) * nt + qi) * nt + ki

    k_cols = alibi_key_columns(slopes, t)
    dv = FLASH_DV
    dva = FLASH_DV + FLASH_ONES
    r = jnp.arange(FLASH_CHUNK, dtype=F32)
    diag = -slopes[:, None, None] * jnp.abs(r[None, :] - r[:, None])[None]
    diag = jnp.stack([jnp.zeros_like(diag), diag], axis=1)
    smem = pl.BlockSpec(memory_space=pltpu.SMEM)
    vec = lambda n: pl.BlockSpec((1, n), lambda b, h, qi, ki, kte, sk: (0, 0))
    grid_spec = pltpu.PrefetchScalarGridSpec(
        num_scalar_prefetch=2,
        grid=(bsz, nh, nt, nt),
        in_specs=[
            smem, smem, smem, smem,
            pl.BlockSpec((1, 1, dv, t), lambda b, h, qi, ki, kte, sk: (b, h, 0, qi)),
            pl.BlockSpec((1, t, dv), lambda b, h, qi, ki, kte, sk: (b, kte[step_id(b, h, qi, ki)], k_blk0 + h)),
            pl.BlockSpec((1, t, dv), lambda b, h, qi, ki, kte, sk: (h, 0, 0)),
            pl.BlockSpec((1, 1, dv, t), lambda b, h, qi, ki, kte, sk: (b, h, 0, kte[step_id(b, h, qi, ki)])),
            pl.BlockSpec((1, 1, FLASH_CHUNK, FLASH_CHUNK),
                         lambda b, h, qi, ki, kte, sk: (h, (ki == 0).astype(I32), 0, 0)),
            vec(HEAD_DIM), vec(HEAD_DIM), vec(HEAD_DIM), vec(HEAD_DIM),
            pl.BlockSpec((dv, 1), lambda b, h, qi, ki, kte, sk: (0, 0)),
        ],
        out_specs=pl.BlockSpec((1, t, LANES), lambda b, h, qi, ki, kte, sk: (b, qi, h)),
        scratch_shapes=[
            pltpu.VMEM((2, 1, t), F32),
            pltpu.VMEM((2, dva, t), F32),
            pltpu.VMEM((2, t, FLASH_PANEL), F32),
            pltpu.SMEM((1,), F32),
        ],
    )
    return pl.pallas_call(
        functools.partial(_flash_body, t=t, step_id=step_id),
        grid_spec=grid_spec,
        out_shape=jax.ShapeDtypeStruct((bsz, seq, nh * dv), BF16),
        compiler_params=_cparams(("parallel", "parallel", "parallel", "arbitrary")),
        name="diff_attention",
    )(kt_eff, skip, slopes, lam_init, qn.reshape(-1), kn.reshape(-1), qt, proj3d, k_cols, vt, diag,
      lq1, lk1, lq2, lk2, subln)


def _outproj_body(x_ref, oa_ref, ob_ref, oc_ref, wa_ref, wb_ref, wc_ref, g1_ref, sc_ref, sh_ref,
                  n2_ref, wr_ref, br_ref, x1_ref, hp_ref, lg_ref):
    mix = (jnp.dot(oa_ref[...], wa_ref[...], preferred_element_type=F32)
           + jnp.dot(ob_ref[...], wb_ref[...], preferred_element_type=F32)
           + jnp.dot(oc_ref[...], wc_ref[...], preferred_element_type=F32))
    x1 = x_ref[...] + g1_ref[0] * mix
    x1_ref[...] = x1
    h = x1 * _rms_scale(x1) * n2_ref[...]
    h = h * (1.0 + sc_ref[0]) + sh_ref[0]
    half = h.shape[1] // 2
    hp_ref[...] = _pack_halves(h[:, :half], h[:, half:])
    lg_ref[...] = lax.dot_general(wr_ref[...], h, _NT, preferred_element_type=F32,
                                  precision=HIGHEST) + br_ref[...]


def out_projection(x2d, oa, ob, oc, wa, wb, wc, gate1, scale2, shift2, gain2, w_router_t, b_router, seq):
    t, d = x2d.shape
    e = w_router_t.shape[0]
    tm = min(TM_PROJ, seq)
    mod_spec = pl.BlockSpec((1, 1, d), lambda i: ((i * tm) // seq, 0, 0))
    row = lambda w: pl.BlockSpec((tm, w), lambda i: (i, 0))
    full = lambda a: pl.BlockSpec(a.shape, lambda i: (0, 0))
    return pl.pallas_call(
        _outproj_body,
        grid=(t // tm,),
        in_specs=[row(d), row(oa.shape[1]), row(ob.shape[1]), row(oc.shape[1]),
                  full(wa), full(wb), full(wc), mod_spec, mod_spec, mod_spec,
                  pl.BlockSpec((1, d), lambda i: (0, 0)), full(w_router_t),
                  pl.BlockSpec((e, 1), lambda i: (0, 0))],
        out_specs=[row(d), row(d // 2), pl.BlockSpec((e, tm), lambda i: (0, i))],
        out_shape=[jax.ShapeDtypeStruct((t, d), F32),
                   jax.ShapeDtypeStruct((t, d // 2), U32),
                   jax.ShapeDtypeStruct((e, t), F32)],
        compiler_params=_cparams(("parallel",)),
        name="out_projection",
    )(x2d, oa, ob, oc, wa, wb, wc, gate1, scale2, shift2, gain2, w_router_t, b_router)


def _router_body(lg_ref, tri_ref, eidx_ref, gate_ref, rank_ref, cnt_ref, base_s):
    step = pl.program_id(0)

    @pl.when(step == 0)
    def _():
        base_s[...] = jnp.zeros(base_s.shape, F32)

    work = lg_ref[...]
    n_e, tm = work.shape
    eio = lax.broadcasted_iota(I32, (n_e, tm), 0).astype(F32)
    vals, hots = [], []
    for k in range(TOP_K):
        mx = jnp.max(work, axis=0, keepdims=True)
        idx = jnp.min(jnp.where(work == mx, eio, float(n_e)), axis=0, keepdims=True)
        hot = eio == idx
        vals.append(mx)
        hots.append(hot)
        eidx_ref[k:k + 1, :] = idx.astype(I32)
        work = jnp.where(hot, -jnp.inf, work)
    exps = [jnp.exp(v - vals[0]) for v in vals]
    den = functools.reduce(lambda a, b: a + b, exps)
    for k in range(TOP_K):
        gate_ref[k:k + 1, :] = exps[k] / den
    any_hot = functools.reduce(lambda a, b: a + b, [h.astype(F32) for h in hots])
    prefix = jnp.dot(any_hot.astype(BF16), tri_ref[...], preferred_element_type=F32)
    pos = base_s[...] + prefix
    for k in range(TOP_K):
        rank_ref[k:k + 1, :] = jnp.sum(jnp.where(hots[k], pos, 0.0), axis=0, keepdims=True).astype(I32)
    base_s[...] = base_s[...] + jnp.sum(any_hot, axis=1, keepdims=True)
    cnt_ref[...] = base_s[...].astype(I32)


def route(logits_t):
    n_e, t = logits_t.shape
    tm = min(TM_ROUTE, t)
    assert t % tm == 0
    r = jnp.arange(tm)
    tri = (r[:, None] < r[None, :]).astype(BF16)
    out4 = lambda dt: jax.ShapeDtypeStruct((TOP_K, t), dt)
    spec4 = pl.BlockSpec((TOP_K, tm), lambda i: (0, i))
    return pl.pallas_call(
        _router_body,
        grid=(t // tm,),
        in_specs=[pl.BlockSpec((n_e, tm), lambda i: (0, i)),
                  pl.BlockSpec((tm, tm), lambda i: (0, 0))],
        out_specs=[spec4, spec4, spec4, pl.BlockSpec((n_e, 1), lambda i: (0, 0))],
        out_shape=[out4(I32), out4(F32), out4(I32), jax.ShapeDtypeStruct((n_e, 1), I32)],
        scratch_shapes=[pltpu.VMEM((n_e, 1), F32)],
        compiler_params=_cparams(("arbitrary",)),
        name="route",
    )(logits_t, tri)


def _dispatch_body(d0_ref, d1_ref, d2_ref, d3_ref, h_ref, init_ref, xs_ref, sem, *, tm):
    del init_ref
    dest_refs = (d0_ref, d1_ref, d2_ref, d3_ref)

    def issue(tok, carry):
        for k in range(TOP_K):
            pltpu.make_async_copy(h_ref.at[pl.ds(tok, 1)], xs_ref.at[pl.ds(dest_refs[k][tok], 1)], sem).start(priority=k % N_DMA_THREADS)
        return carry

    lax.fori_loop(0, tm, issue, 0, unroll=DMA_ISSUE_UNROLL)
    for k in range(TOP_K):
        pltpu.make_async_copy(h_ref, xs_ref.at[pl.ds(0, tm)], sem).wait()


def dispatch_rows(h_packed, dest, n_rows):
    t, w = h_packed.shape
    tm = min(TM_DISPATCH, t)
    assert TOP_K == 4
    nsteps = t // tm
    flat = dest.reshape(-1)
    return pl.pallas_call(
        functools.partial(_dispatch_body, tm=tm),
        grid=(nsteps,),
        in_specs=[pl.BlockSpec((tm,), functools.partial(lambda k, i: (k * nsteps + i,), k), memory_space=pltpu.SMEM)
                  for k in range(TOP_K)] + [
                  pl.BlockSpec((tm, w), lambda i: (i, 0)),
                  pl.BlockSpec(memory_space=pl.ANY)],
        out_specs=pl.BlockSpec(memory_space=pl.ANY),
        out_shape=jax.ShapeDtypeStruct((n_rows, w), U32),
        input_output_aliases={TOP_K + 1: 0},
        scratch_shapes=[pltpu.SemaphoreType.DMA],
        compiler_params=_cparams(("arbitrary",)),
        name="dispatch_rows",
    )(flat, flat, flat, flat, h_packed, jnp.zeros((n_rows, w), U32))


def _ffn_body(be_ref, bv_ref, nu_ref, xs_ref, wg_ref, bg_ref, wu_ref, bu_ref, wd_ref, bd_ref, o_ref,
              wg_s, wu_s, wd_s):
    i = pl.program_id(0)

    @pl.when(i < nu_ref[0])
    def _():
        @pl.when(jnp.logical_or(i == 0, be_ref[i] != be_ref[jnp.maximum(i - 1, 0)]))
        def _():
            wg_s[...] = wg_ref[0, 0].astype(BF16)
            wu_s[...] = wu_ref[0, 0].astype(BF16)
            wd_s[...] = wd_ref[0, 0].astype(BF16)

        xa, xb = _unpack_halves(xs_ref[...])
        blk, half = xa.shape
        valid = lax.broadcasted_iota(I32, (blk, 1), 0) < bv_ref[i]
        xa = jnp.where(valid, xa, 0.0).astype(BF16)
        xb = jnp.where(valid, xb, 0.0).astype(BF16)

        def proj(w_s, b_ref):
            return (jnp.dot(xa, w_s[:half, :], preferred_element_type=F32)
                    + jnp.dot(xb, w_s[half:, :], preferred_element_type=F32) + b_ref[0, 0])

        a = jnp.minimum(proj(wg_s, bg_ref), SWIGLU_LIMIT)
        u = jnp.clip(proj(wu_s, bu_ref), -SWIGLU_LIMIT, SWIGLU_LIMIT)
        hid = a * (1.0 / (1.0 + jnp.exp(-SWIGLU_ALPHA * a))) * (u + 1.0)
        out = jnp.dot(hid.astype(BF16), wd_s[...], preferred_element_type=F32) + bd_ref[0, 0]
        d = out.shape[1] // 2
        o_ref[...] = _pack_halves(out[:, :d], out[:, d:])

    @pl.when(i >= nu_ref[0])
    def _():
        o_ref[...] = jnp.zeros(o_ref.shape, o_ref.dtype)


def expert_ffn(xs, blk_expert, blk_valid, n_used, layer, wg, bg, wu, bu, wd, bd):
    n_rows, w = xs.shape
    _, n_e, d, f = wg.shape
    nb = n_rows // BLK_FFN
    row_map = lambda i, be, bv, nu: (jnp.minimum(i, nu[0] - 1), 0)
    exp_map = lambda i, be, bv, nu: (layer, be[i], 0, 0)
    grid_spec = pltpu.PrefetchScalarGridSpec(
        num_scalar_prefetch=3,
        grid=(nb,),
        in_specs=[pl.BlockSpec((BLK_FFN, w), row_map),
                  pl.BlockSpec((1, 1, d, f), exp_map), pl.BlockSpec((1, 1, 1, f), exp_map),
                  pl.BlockSpec((1, 1, d, f), exp_map), pl.BlockSpec((1, 1, 1, f), exp_map),
                  pl.BlockSpec((1, 1, f, d), exp_map), pl.BlockSpec((1, 1, 1, d), exp_map)],
        out_specs=pl.BlockSpec((BLK_FFN, d // 2), lambda i, be, bv, nu: (i, 0)),
        scratch_shapes=[pltpu.VMEM((d, f), BF16), pltpu.VMEM((d, f), BF16), pltpu.VMEM((f, d), BF16)],
    )
    return pl.pallas_call(
        _ffn_body,
        grid_spec=grid_spec,
        out_shape=jax.ShapeDtypeStruct((n_rows, d // 2), U32),
        compiler_params=_cparams(("arbitrary",)),
        name="expert_ffn",
    )(blk_expert, blk_valid, n_used, xs, wg, bg, wu, bu, wd, bd)


def _combine_body(dest_ref, x_ref, gt_ref, g2_ref, nf_ref, ys_ref, o_ref, buf, sem, *, tm, final):
    def issue(tok, carry):
        for k in range(TOP_K):
            pltpu.make_async_copy(ys_ref.at[pl.ds(dest_ref[k, tok], 1)], buf.at[k, pl.ds(tok, 1)], sem).start(priority=k % N_DMA_THREADS)
        return carry

    lax.fori_loop(0, tm, issue, 0, unroll=DMA_ISSUE_UNROLL)
    for k in range(TOP_K):
        pltpu.make_async_copy(ys_ref.at[pl.ds(0, tm)], buf.at[k], sem).wait()
    gt = gt_ref[...]
    lo = hi = None
    for k in range(TOP_K):
        a, b = _unpack_halves(buf[k])
        g = gt[:, k:k + 1]
        lo = g * a if lo is None else lo + g * a
        hi = g * b if hi is None else hi + g * b
    half = lo.shape[1]
    g2 = g2_ref[0]
    x_lo = x_ref[:, :half] + g2[:, :half] * lo
    x_hi = x_ref[:, half:] + g2[:, half:] * hi
    if final:
        ms = (jnp.sum(x_lo * x_lo, axis=-1, keepdims=True)
              + jnp.sum(x_hi * x_hi, axis=-1, keepdims=True)) / (2 * half)
        r = lax.rsqrt(ms + RMS_EPS)
        x_lo = x_lo * r * nf_ref[:, :half]
        x_hi = x_hi * r * nf_ref[:, half:]
    o_ref[:, :half] = x_lo
    o_ref[:, half:] = x_hi


def combine(x1, ys, dest, gates_t, gate2, norm_f, seq, final):
    t, d = x1.shape
    tm = min(TM_COMBINE, seq)
    return pl.pallas_call(
        functools.partial(_combine_body, tm=tm, final=final),
        grid=(t // tm,),
        in_specs=[pl.BlockSpec((TOP_K, tm), lambda i: (0, i), memory_space=pltpu.SMEM),
                  pl.BlockSpec((tm, d), lambda i: (i, 0)),
                  pl.BlockSpec((tm, TOP_K), lambda i: (i, 0)),
                  pl.BlockSpec((1, 1, d), lambda i: ((i * tm) // seq, 0, 0)),
                  pl.BlockSpec((1, d), lambda i: (0, 0)),
                  pl.BlockSpec(memory_space=pl.ANY)],
        out_specs=pl.BlockSpec((tm, d), lambda i: (i, 0)),
        out_shape=jax.ShapeDtypeStruct((t, d), F32),
        scratch_shapes=[pltpu.VMEM((TOP_K, tm, d // 2), U32), pltpu.SemaphoreType.DMA],
        compiler_params=_cparams(("arbitrary",)),
        name="combine",
    )(dest, x1, gates_t, gate2, norm_f, ys)


def _combine_project_body(dcur_ref, dnext_ref, x_ref, gt_ref, g2_ref, sc_ref, sh_ref, gn_ref, w_ref, ys_ref,
                          x2_ref, proj_ref, nrm_ref, buf0, buf1, sem, *, tm, norm_cols):
    i = pl.program_id(0)
    bufs = (buf0, buf1)

    def start(dref, col, tok, k, s):
        pltpu.make_async_copy(ys_ref.at[pl.ds(dref[k, col + tok], 1)], bufs[s].at[k, pl.ds(tok, 1)], sem.at[s]).start(priority=k % N_DMA_THREADS)

    def wait(s):
        for k in range(TOP_K):
            pltpu.make_async_copy(ys_ref.at[pl.ds(0, tm)], bufs[s].at[k], sem.at[s]).wait()

    def issue_inline(dref, col, s):
        for tok in range(tm):
            for k in range(TOP_K):
                start(dref, col, tok, k, s)

    def compute(s):
        rows = slice(s * tm, (s + 1) * tm)
        gt = gt_ref[rows, :]
        lo = hi = None
        for k in range(TOP_K):
            a, b = _unpack_halves(bufs[s][k])
            g = gt[:, k:k + 1]
            lo = g * a if lo is None else lo + g * a
            hi = g * b if hi is None else hi + g * b
        half = lo.shape[1]
        g2 = g2_ref[0]
        x2 = jnp.concatenate([x_ref[rows, :half] + g2[:, :half] * lo, x_ref[rows, half:] + g2[:, half:] * hi],
                             axis=1)
        x2_ref[rows, :] = x2
        _project_rows(x2, sc_ref[0], sh_ref[0], gn_ref[...], w_ref, proj_ref.at[rows, :], nrm_ref.at[s:s + 1],
                      norm_cols)

    @pl.when(i == 0)
    def _():
        def issue(tok, carry):
            for k in range(TOP_K):
                start(dcur_ref, 0, tok, k, 0)
            return carry

        lax.fori_loop(0, tm, issue, 0, unroll=DMA_ISSUE_UNROLL)

    wait(0)
    issue_inline(dcur_ref, tm, 1)
    compute(0)
    wait(1)
    issue_inline(dnext_ref, 0, 0)
    compute(1)

    @pl.when(i == pl.num_programs(0) - 1)
    def _():
        wait(0)


def combine_project(x1, ys, dest, gates_t, gate2, scale, shift, gain, w_bf16, seq, norm_cols):
    t, d = x1.shape
    n = w_bf16.shape[1]
    tm = min(TM_COMBINE, seq // 2)
    nblocks = t // tm
    nsteps = nblocks // 2
    assert t % (2 * tm) == 0 and seq % (2 * tm) == 0
    mod_spec = pl.BlockSpec((1, 1, d), lambda i: ((i * 2 * tm) // seq, 0, 0))
    return pl.pallas_call(
        functools.partial(_combine_project_body, tm=tm, norm_cols=norm_cols),
        grid=(nsteps,),
        in_specs=[pl.BlockSpec((TOP_K, 2 * tm), lambda i: (0, i), memory_space=pltpu.SMEM),
                  pl.BlockSpec((TOP_K, tm), lambda i: (0, jnp.minimum(2 * i + 2, nblocks - 1)),
                               memory_space=pltpu.SMEM),
                  pl.BlockSpec((2 * tm, d), lambda i: (i, 0)),
                  pl.BlockSpec((2 * tm, TOP_K), lambda i: (i, 0)),
                  mod_spec, mod_spec, mod_spec,
                  pl.BlockSpec((1, d), lambda i: (0, 0)),
                  pl.BlockSpec((d, n), lambda i: (0, 0)),
                  pl.BlockSpec(memory_space=pl.ANY)],
        out_specs=[pl.BlockSpec((2 * tm, d), lambda i: (i, 0)),
                   pl.BlockSpec((2 * tm, n), lambda i: (i, 0)),
                   pl.BlockSpec((2, 1, len(norm_cols)), lambda i: (i, 0, 0))],
        out_shape=[jax.ShapeDtypeStruct((t, d), F32),
                   jax.ShapeDtypeStruct((t, n), BF16),
                   jax.ShapeDtypeStruct((nblocks, 1, len(norm_cols)), F32)],
        scratch_shapes=[pltpu.VMEM((TOP_K, tm, d // 2), U32), pltpu.VMEM((TOP_K, tm, d // 2), U32),
                        pltpu.SemaphoreType.DMA((2,))],
        compiler_params=_cparams(("arbitrary",)),
        name="combine_project",
    )(dest, dest, x1, gates_t, gate2, scale, shift, gain, w_bf16, ys)


def _alibi_slopes(n):
    return jnp.exp2(-8.0 * jnp.arange(1, n + 1, dtype=F32) / n)


def _routing_tables(counts, eidx, rank, nb):
    counts = counts.reshape(-1)
    padded = (counts + BLK_FFN - 1) // BLK_FFN * BLK_FFN
    pad_end = jnp.cumsum(padded)
    pad_start = pad_end - padded
    e_ids = jnp.arange(N_EXPERTS, dtype=I32)[:, None, None]
    dest = rank + jnp.sum(jnp.where(eidx[None] == e_ids, pad_start[:, None, None], 0), axis=0)
    blk_row0 = jnp.arange(nb, dtype=I32) * BLK_FFN
    blk_expert = jnp.minimum(jnp.sum(pad_end[None, :] <= blk_row0[:, None], axis=1), N_EXPERTS - 1).astype(I32)
    blk_valid = jnp.clip(counts[blk_expert] - (blk_row0 - pad_start[blk_expert]), 0, BLK_FFN).astype(I32)
    n_used = (pad_end[-1:] // BLK_FFN).astype(I32)
    return dest.astype(I32), blk_expert, blk_valid, n_used


def kernel(x, c, w_ada, b_ada, norm1, w_in, sinks, lambda_q1, lambda_k1, lambda_q2, lambda_k2, subln,
           w_out, norm2, w_router, b_router, w_gate, b_gate, w_up, b_up, w_down, b_down, norm_f):
    bsz, seq, d = x.shape
    depth = w_ada.shape[0]
    t = bsz * seq
    wa_w, wb_w = N_HEADS_A * HEAD_DIM, N_HEADS_B * HEAD_DIM
    kvb_w = N_KV_B * HEAD_DIM
    wc_w = N_HEADS_C * 2 * HEAD_DIM
    off = [0]
    for wdt in (wa_w, wa_w, wa_w, wb_w, kvb_w, kvb_w, wc_w, wc_w, wc_w):
        off.append(off[-1] + wdt)
    blk = [o // LANES for o in off]
    norm_cols = tuple(off[i] + hh * FLASH_DV for i in (6, 7) for hh in range(N_HEADS_C))

    group = N_HEADS_B // N_KV_B
    b_perm = [kv * group + g for g in range(group) for kv in range(N_KV_B)]
    col_perm = jnp.concatenate([jnp.arange(h * HEAD_DIM, (h + 1) * HEAD_DIM) for h in b_perm])
    q_scale = jnp.ones((off[-1],), F32)
    for s0, s1, extra in ((off[0], off[1], 1.0), (off[3], off[4], 1.0), (off[6], off[7], LOG2E)):
        q_scale = q_scale.at[s0:s1].set(HEAD_DIM ** -0.5 * extra)

    slopes_a = _alibi_slopes(N_HEADS_A)
    slopes_b = _alibi_slopes(N_HEADS_B)[jnp.array(b_perm)]
    slopes_c = _alibi_slopes(N_HEADS_C) * LOG2E

    mod_all = ada_modulation(c, w_ada, b_ada)
    x2d = x.reshape(t, d)
    nb = (t * TOP_K) // BLK_FFN + N_EXPERTS
    n_rows = nb * BLK_FFN

    def layer_mod(layer):
        mod = mod_all[layer].reshape(bsz, N_MOD, 1, d)
        return [mod[:, i] for i in range(N_MOD)]

    def in_weights(layer):
        w_l = w_in[layer] * q_scale[None, :]
        w_l = jnp.concatenate([w_l[:, :off[3]], w_l[:, off[3]:off[4]][:, col_perm], w_l[:, off[4]:]], axis=1)
        return w_l.astype(BF16)

    proj = blk_norms = None
    for layer in range(depth):
        sh1, sc1, g1, sh2, sc2, g2 = layer_mod(layer)
        if layer == 0:
            proj, blk_norms = in_projection(x2d, sc1, sh1, norm1[layer].reshape(1, d), in_weights(layer), seq,
                                            norm_cols)
        proj3d = proj.reshape(bsz, seq, off[-1])

        n_blk_a = wa_w // LANES
        o_a = window_attention(proj3d, blk[0], blk[1], blk[2], 1, n_blk_a, slopes_a,
                               jnp.full((n_blk_a, 2), -jnp.inf, F32), DILATED_PAIRS).reshape(t, wa_w)

        sink_l = sinks[layer].astype(F32)[jnp.array(b_perm)].reshape(group, N_KV_B)
        o_b = window_attention(proj3d, blk[3], blk[4], blk[5], 0, group, slopes_b, sink_l,
                               ((2 * SWA_HALF, 1),)).reshape(t, wb_w)

        lam_init = jnp.full((1,), 0.8 - 0.6 * math.exp(-0.3 * layer), F32)
        heads_t = lambda i: (proj3d[:, :, off[i]:off[i + 1]].reshape(bsz, seq, N_HEADS_C, FLASH_DV)
                             .transpose(0, 2, 3, 1))
        t_flash = min(T_FLASH, seq)
        norms = blk_norms.reshape(bsz, seq // t_flash, -1, 2 * N_HEADS_C).max(axis=2, keepdims=True)
        o_c = diff_attention(heads_t(6), proj3d, blk[7], heads_t(8), norms, slopes_c, lam_init,
                             lambda_q1[layer].reshape(1, -1), lambda_k1[layer].reshape(1, -1),
                             lambda_q2[layer].reshape(1, -1), lambda_k2[layer].reshape(1, -1),
                             subln[layer].reshape(-1, 1)).reshape(t, wc_w)

        wo = w_out[layer]
        wo_b = wo[wa_w:wa_w + wb_w].reshape(N_HEADS_B, HEAD_DIM, d)[jnp.array(b_perm)].reshape(wb_w, d)
        x1, h_packed, logits_t = out_projection(
            x2d, o_a, o_b, o_c, wo[:wa_w].astype(BF16), wo_b.astype(BF16), wo[wa_w + wb_w:].astype(BF16),
            g1, sc2, sh2, norm2[layer].reshape(1, d), w_router[layer].T, b_router[layer].reshape(-1, 1), seq)

        eidx, gates, rank, counts = route(logits_t)
        dest, blk_expert, blk_valid, n_used = _routing_tables(counts, eidx, rank, nb)
        xs = dispatch_rows(h_packed, dest, n_rows)
        ys = expert_ffn(xs, blk_expert, blk_valid, n_used, layer,
                        w_gate, b_gate[:, :, None, :], w_up, b_up[:, :, None, :], w_down, b_down[:, :, None, :])
        if layer == depth - 1:
            x2d = combine(x1, ys, dest, gates.T, g2, norm_f.reshape(1, d), seq, True)
        else:
            nsh1, nsc1 = layer_mod(layer + 1)[:2]
            x2d, proj, blk_norms = combine_project(x1, ys, dest, gates.T, g2, nsc1, nsh1,
                                                   norm1[layer + 1].reshape(1, d), in_weights(layer + 1), seq,
                                                   norm_cols)

    return x2d.reshape(bsz, seq, d)
```

```python
import functools
import math

import jax
import jax.numpy as jnp
from jax import lax
from jax.experimental import pallas as pl
from jax.experimental.pallas import tpu as pltpu

F32 = jnp.float32
BF16 = jnp.bfloat16
I32 = jnp.int32
U32 = jnp.uint32

HEAD_DIM = 64
DILATED_PAIRS = ((128, 1), (512, 4), (2048, 16))
N_HEADS_A = 4
N_HEADS_B = 4
N_KV_B = 2
SWA_HALF = 128
N_HEADS_C = 4
N_EXPERTS = 32
TOP_K = 4
SWIGLU_LIMIT = 7.0
SWIGLU_ALPHA = 1.702
RMS_EPS = 1e-6
N_MOD = 6

LANES = 128
VMEM_LIMIT_BYTES = 56 * 1024 * 1024

TM_PROJ = 512
TQ_BAND = 256
T_FLASH = 1024
FLASH_PANEL = 512
FLASH_CHUNK = 256
FLASH_SUB = 128
FLASH_DV = 2 * HEAD_DIM
FLASH_ONES = 16
FLASH_AUG = 3
FLASH_OWN_LOG2 = 100.0
FLASH_ZERO_LOG2 = 150.0
FLASH_LOOKAHEAD = 2
FLASH_SAFE_LOG2 = 60.0
LOG2E = math.log2(math.e)
TM_ROUTE = 512
TM_DISPATCH = 512
BLK_FFN = 512
TM_COMBINE = 256
DMA_ISSUE_UNROLL = 8
N_DMA_THREADS = 2

HIGHEST = lax.Precision.HIGHEST
_NT = (((1,), (1,)), ((), ()))


def _cparams(sem):
    return pltpu.CompilerParams(dimension_semantics=sem, vmem_limit_bytes=VMEM_LIMIT_BYTES)


def _pack_halves(a, b):
    ua = lax.bitcast_convert_type(a.astype(BF16).astype(F32), U32)
    ub = lax.bitcast_convert_type(b.astype(BF16).astype(F32), U32)
    return ua | (ub >> 16)


def _unpack_halves(w):
    a = lax.bitcast_convert_type(w & jnp.uint32(0xFFFF0000), F32)
    b = lax.bitcast_convert_type(w << 16, F32)
    return a, b


def _rms_scale(x):
    return lax.rsqrt(jnp.mean(x * x, axis=-1, keepdims=True) + RMS_EPS)


def _ada_body(c_ref, w_ref, b_ref, o_ref):
    c = c_ref[...]
    cond = c / (1.0 + jnp.exp(-c))
    o_ref[0] = jnp.dot(cond, w_ref[0], preferred_element_type=F32, precision=HIGHEST) + b_ref[0]


def ada_modulation(c, w_ada, b_ada):
    depth, d, n = w_ada.shape
    bsz = c.shape[0]
    tn = 1536
    assert n % tn == 0
    return pl.pallas_call(
        _ada_body,
        grid=(depth, n // tn),
        in_specs=[
            pl.BlockSpec((bsz, d), lambda l, j: (0, 0)),
            pl.BlockSpec((1, d, tn), lambda l, j: (l, 0, j)),
            pl.BlockSpec((1, 1, tn), lambda l, j: (l, 0, j)),
        ],
        out_specs=pl.BlockSpec((1, bsz, tn), lambda l, j: (l, 0, j)),
        out_shape=jax.ShapeDtypeStruct((depth, bsz, n), F32),
        compiler_params=_cparams(("parallel", "parallel")),
        name="ada_modulation",
    )(c, w_ada, b_ada.reshape(depth, 1, n))


def _project_rows(x, sc, sh, g, w_ref, o_ref, n_ref, norm_cols):
    h = x * _rms_scale(x) * g
    h = h * (1.0 + sc) + sh
    out = jnp.dot(h.astype(BF16), w_ref[...], preferred_element_type=F32).astype(BF16)
    o_ref[...] = out
    lane = lax.broadcasted_iota(I32, (1, len(norm_cols)), 1)
    acc = jnp.zeros((1, len(norm_cols)), F32)
    for j, c0 in enumerate(norm_cols):
        v = out[:, c0:c0 + FLASH_DV].astype(F32)
        n2 = jnp.max(jnp.sum(v * v, axis=1, keepdims=True), axis=0, keepdims=True)
        acc = jnp.where(lane == j, jnp.sqrt(n2), acc)
    n_ref[0] = acc


def _inproj_body(x_ref, sc_ref, sh_ref, g_ref, w_ref, o_ref, n_ref, *, norm_cols):
    _project_rows(x_ref[...], sc_ref[0], sh_ref[0], g_ref[...], w_ref, o_ref, n_ref, norm_cols)


def in_projection(x2d, scale, shift, gain, w_bf16, seq, norm_cols):
    t, d = x2d.shape
    n = w_bf16.shape[1]
    tm = min(TM_PROJ, seq)
    assert seq % tm == 0 and t % tm == 0
    mod_spec = pl.BlockSpec((1, 1, d), lambda i: ((i * tm) // seq, 0, 0))
    return pl.pallas_call(
        functools.partial(_inproj_body, norm_cols=norm_cols),
        grid=(t // tm,),
        in_specs=[
            pl.BlockSpec((tm, d), lambda i: (i, 0)),
            mod_spec, mod_spec,
            pl.BlockSpec((1, d), lambda i: (0, 0)),
            pl.BlockSpec((d, n), lambda i: (0, 0)),
        ],
        out_specs=[pl.BlockSpec((tm, n), lambda i: (i, 0)),
                   pl.BlockSpec((1, 1, len(norm_cols)), lambda i: (i, 0, 0))],
        out_shape=[jax.ShapeDtypeStruct((t, n), BF16),
                   jax.ShapeDtypeStruct((t // tm, 1, len(norm_cols)), F32)],
        compiler_params=_cparams(("parallel",)),
        name="in_projection",
    )(x2d, scale, shift, gain, w_bf16)


def _offset_bias_body(c_ref, slope_ref, o_ref, *, pairs):
    vi = pl.program_id(0)
    hd = pl.program_id(1)
    tq, win = o_ref.shape[2], o_ref.shape[3]
    off = (lax.broadcasted_iota(I32, (tq, win), 1) - lax.broadcasted_iota(I32, (tq, win), 0)) + c_ref[vi]
    dist = jnp.abs(off)
    mult = jnp.zeros((tq, win), F32)
    for window, dil in pairs:
        hit = jnp.logical_and(dist <= window // 2, (dist & (dil - 1)) == 0)
        mult = mult + jnp.where(hit, 1.0, 0.0)
    o_ref[0, 0] = jnp.log(mult) - slope_ref[hd] * dist.astype(F32)


def _window_variants(seq, tq, reach):
    win = tq + 2 * reach
    nq, edge = seq // tq, -(-reach // tq)
    assert seq % tq == 0 and reach % 16 == 0 and win <= seq and nq >= 2 * edge + 1
    rel = [min(max(i * tq - reach, 0), seq - win) - i * tq for i in range(nq)]
    assert all(r == -reach for r in rel[edge:nq - edge])
    starts = rel[:edge] + [-reach] + rel[nq - edge:]

    def variant(i):
        return jnp.where(i < edge, i, jnp.where(i >= nq - edge, i - (nq - 2 * edge - 1), edge))

    return win, starts, variant


def _window_body(c_ref, sink_ref, q_ref, k_ref, v_ref, tab_ref, o_ref, *, tq, win, variant):
    j = pl.program_id(1)
    i = pl.program_id(2)
    ks = pl.multiple_of(i * tq + c_ref[variant(i)], 16)
    q = q_ref[0]
    k = k_ref[0, pl.ds(ks, win), :]
    v = v_ref[0, pl.ds(ks, win), :]
    lane_lo = lax.broadcasted_iota(I32, (1, LANES), 1) < HEAD_DIM
    outs = []
    for hf in range(2):
        qm = jnp.where(lane_lo if hf == 0 else jnp.logical_not(lane_lo), q, jnp.zeros_like(q))
        s = lax.dot_general(qm, k, _NT, preferred_element_type=F32) + tab_ref[0, hf]
        sink = sink_ref[j, hf]
        m = jnp.maximum(jnp.max(s, axis=-1, keepdims=True), sink)
        p = jnp.exp(s - m)
        l = jnp.sum(p, axis=-1, keepdims=True) + jnp.exp(sink - m)
        outs.append(jnp.dot(p.astype(BF16), v, preferred_element_type=F32) / l)
    o_ref[0] = jnp.where(lane_lo, outs[0], outs[1]).astype(o_ref.dtype)


def window_attention(proj3d, q_blk0, k_blk0, v_blk0, kv_step, n_blk, slopes, sinks, pairs):
    bsz, seq, _ = proj3d.shape
    assert all(d & (d - 1) == 0 for _, d in pairs)
    reach = max(w // 2 for w, _ in pairs)
    tq = TQ_BAND
    win, starts, variant = _window_variants(seq, tq, reach)
    starts = jnp.array(starts, I32)
    smem = pl.BlockSpec(memory_space=pltpu.SMEM)
    tables = pl.pallas_call(
        functools.partial(_offset_bias_body, pairs=pairs),
        grid=(starts.shape[0], 2 * n_blk),
        in_specs=[smem, smem],
        out_specs=pl.BlockSpec((1, 1, tq, win), lambda vi, hd: (vi, hd, 0, 0)),
        out_shape=jax.ShapeDtypeStruct((starts.shape[0], 2 * n_blk, tq, win), F32),
        compiler_params=_cparams(("parallel", "parallel")),
        name="offset_bias_tables",
    )(starts, slopes)
    return pl.pallas_call(
        functools.partial(_window_body, tq=tq, win=win, variant=variant),
        grid=(bsz, n_blk, seq // tq),
        in_specs=[
            smem, smem,
            pl.BlockSpec((1, tq, LANES), lambda b, j, i: (b, i, q_blk0 + j)),
            pl.BlockSpec((1, seq, LANES), lambda b, j, i: (b, 0, k_blk0 + kv_step * j)),
            pl.BlockSpec((1, seq, LANES), lambda b, j, i: (b, 0, v_blk0 + kv_step * j)),
            pl.BlockSpec((1, 2, tq, win), lambda b, j, i: (variant(i), j, 0, 0)),
        ],
        out_specs=pl.BlockSpec((1, tq, LANES), lambda b, j, i: (b, i, j)),
        out_shape=jax.ShapeDtypeStruct((bsz, seq, n_blk * LANES), BF16),
        compiler_params=_cparams(("parallel", "parallel", "parallel")),
        name="window_attention",
    )(starts, sinks, proj3d, proj3d, proj3d, tables)


def _flash_body(kte_ref, skip_ref, slope_ref, lami_ref, qn_ref, kn_ref, qt_ref, k_ref, kaug_ref, vt_ref, diag_ref,
                lq1_ref, lk1_ref, lq2_ref, lk2_ref, sub_ref, o_ref, m_s, acc_s, s_scr, mmin_s, *, t, step_id):
    b = pl.program_id(0)
    h = pl.program_id(1)
    qi = pl.program_id(2)
    ki = pl.program_id(3)
    nt = pl.num_programs(3)
    kt = lax.rem(qi + ki, nt)

    @pl.when(ki == 0)
    def _():
        m_s[...] = jnp.full(m_s.shape, -jnp.inf, F32)
        acc_s[...] = jnp.zeros(acc_s.shape, F32)
        mmin_s[0] = jnp.float32(0.0)

    slope = slope_ref[h]
    n_chunk = t // FLASH_CHUNK
    n_panel = t // FLASH_PANEL
    sub_per_panel = FLASH_PANEL // FLASH_CHUNK
    row = lax.broadcasted_iota(I32, (FLASH_DV, 1), 0)
    row_lo = row < HEAD_DIM
    row_aug = lax.broadcasted_iota(I32, (FLASH_ONES, 1), 0) < FLASH_AUG
    lane_pos = lax.broadcasted_iota(I32, (1, FLASH_CHUNK), 1).astype(F32)
    ones_rows = jnp.ones((FLASH_ONES, FLASH_CHUNK), BF16)

    def with_sign(qtm, sign):
        aug = jnp.where(row_aug, sign, 0.0).astype(BF16)
        pad = jnp.zeros((FLASH_DV - FLASH_ONES, sign.shape[1]), BF16)
        return jnp.concatenate([qtm, aug, pad], axis=0)

    def keys(kc):
        rows = slice(kc * FLASH_CHUNK, (kc + 1) * FLASH_CHUNK)
        return jnp.concatenate([k_ref[0, rows, :], kaug_ref[0, rows, :]], axis=1)

    def values(kc):
        return jnp.concatenate([vt_ref[0, 0, :, kc * FLASH_CHUNK:(kc + 1) * FLASH_CHUNK], ones_rows], axis=0)
    blocks = [(qp, mp) for qp in range(n_panel) for mp in range(2)]

    def fold8(x, op):
        return functools.reduce(op, [x[r:r + 8] for r in range(0, x.shape[0], 8)])

    def pair_terms(qp, kc):
        signs, shifts = [], []
        for j in range(sub_per_panel):
            gap = (qi * n_chunk + qp * sub_per_panel + j) - (kt * n_chunk + kc)
            sign = jnp.sign(gap).astype(F32)
            signs.append(jnp.full((1, FLASH_CHUNK), sign, F32))
            shifts.append(-slope * ((jnp.abs(gap) * FLASH_CHUNK).astype(F32) + sign * lane_pos))
        return jnp.concatenate(signs, axis=1), jnp.concatenate(shifts, axis=1)

    def masked_queries(bi):
        qp, mp = blocks[bi]
        qt = qt_ref[0, 0, :, qp * FLASH_PANEL:(qp + 1) * FLASH_PANEL]
        return jnp.where(row_lo if mp == 0 else jnp.logical_not(row_lo), qt, jnp.zeros_like(qt))

    def score_chunk(bi, kc, qtm, mx):
        qp, _ = blocks[bi]
        k0 = kc * FLASH_CHUNK
        sign, lane_shift = pair_terms(qp, kc)
        s = jnp.dot(keys(kc), with_sign(qtm, sign), preferred_element_type=F32)
        j = kc - qp * sub_per_panel
        if 0 <= j < sub_per_panel:
            cols = [s[:, i * FLASH_CHUNK:(i + 1) * FLASH_CHUNK] for i in range(sub_per_panel)]
            cols[j] = cols[j] + diag_ref[0, 0]
            s = jnp.concatenate(cols, axis=1)
        s_scr[bi % 2, k0:k0 + FLASH_CHUNK, :] = s
        cm = fold8(s, jnp.maximum) + lane_shift
        return cm if mx is None else jnp.maximum(mx, cm)

    def prob_chunk(bi, kc, m_new, acc):
        qp, _ = blocks[bi]
        k0 = kc * FLASH_CHUNK
        shift = m_new - pair_terms(qp, kc)[1]
        parts = []
        for r0 in range(k0, k0 + FLASH_CHUNK, FLASH_SUB):
            parts.append(jnp.exp2((s_scr[bi % 2, r0:r0 + FLASH_SUB, :] - shift).astype(BF16)))
        pv = jnp.dot(values(kc), jnp.concatenate(parts, axis=0), preferred_element_type=F32)
        return pv if acc is None else acc + pv

    def two_pass_tile():
        qtm_next = masked_queries(0)
        mx_next = None
        for kc in range(n_chunk):
            mx_next = score_chunk(0, kc, qtm_next, mx_next)
        for bi, (qp, mp) in enumerate(blocks):
            qs = slice(qp * FLASH_PANEL, (qp + 1) * FLASH_PANEL)
            m_old = m_s[mp, :, qs]
            m_new = jnp.maximum(m_old, jnp.max(mx_next, axis=0, keepdims=True))
            alpha = jnp.exp2(m_old - m_new)
            acc = None
            has_next = bi + 1 < len(blocks)
            if has_next:
                qtm_next = masked_queries(bi + 1)
            mx_next = None
            for kc in range(n_chunk):
                if has_next:
                    mx_next = score_chunk(bi + 1, kc, qtm_next, mx_next)
                acc = prob_chunk(bi, kc, m_new, acc)
            acc_s[mp, :, qs] = alpha * acc_s[mp, :, qs] + acc
            m_s[mp, :, qs] = m_new
        mmin_s[0] = jnp.min(m_s[...])

    def one_pass_tile(diagonal):
        items = [(bi, kc) for bi in range(len(blocks)) for kc in range(n_chunk)]

        def scores(bi, kc):
            qp, mp = blocks[bi]
            sign, lane_shift = pair_terms(qp, kc)
            s = jnp.dot(keys(kc), with_sign(masked_queries(bi), sign), preferred_element_type=F32)
            j = kc - qp * sub_per_panel
            if diagonal and 0 <= j < sub_per_panel:
                cols = [s[:, c * FLASH_CHUNK:(c + 1) * FLASH_CHUNK] for c in range(sub_per_panel)]
                cols[j] = cols[j] + diag_ref[0, 0]
                s = jnp.concatenate(cols, axis=1)
            return s, m_s[mp, :, qp * FLASH_PANEL:(qp + 1) * FLASH_PANEL] - lane_shift

        ahead = [scores(*items[i]) for i in range(FLASH_LOOKAHEAD)]
        acc = None
        for n, (bi, kc) in enumerate(items):
            s, shift = ahead.pop(0)
            if n + FLASH_LOOKAHEAD < len(items):
                ahead.append(scores(*items[n + FLASH_LOOKAHEAD]))
            p = jnp.exp2(s - shift).astype(BF16)
            pv = jnp.dot(values(kc), p, preferred_element_type=F32)
            acc = pv if kc == 0 else acc + pv
            if kc == n_chunk - 1:
                qp, mp = blocks[bi]
                qs = slice(qp * FLASH_PANEL, (qp + 1) * FLASH_PANEL)
                acc_s[mp, :, qs] = acc_s[mp, :, qs] + acc

    bound = qn_ref[(b * pl.num_programs(1) + h) * nt + qi] * kn_ref[(b * pl.num_programs(1) + h) * nt + kt]
    min_dist = jnp.maximum((jnp.abs(qi - kt) - 1) * t + 1, 0).astype(F32)
    headroom = bound - mmin_s[0]
    all_zero = jnp.logical_and(ki > 0, headroom - slope * min_dist <= -FLASH_ZERO_LOG2)
    all_zero = jnp.logical_or(all_zero, skip_ref[step_id(b, h, qi, ki)] == 1)
    safe = jnp.logical_and(ki > 0, headroom <= FLASH_SAFE_LOG2)
    own_one_pass = jnp.logical_and(ki == 0, 2.0 * bound <= FLASH_OWN_LOG2)

    @pl.when(own_one_pass)
    def _():
        m_s[...] = jnp.zeros(m_s.shape, F32) + bound
        mmin_s[0] = bound
        one_pass_tile(True)

    pl.when(jnp.logical_and(safe, jnp.logical_not(all_zero)))(functools.partial(one_pass_tile, False))
    pl.when(jnp.logical_not(jnp.logical_or(jnp.logical_or(safe, all_zero), own_one_pass)))(two_pass_tile)

    @pl.when(ki == pl.num_programs(3) - 1)
    def _():
        lam_init = lami_ref[0]
        lam = (jnp.exp(jnp.sum(lq1_ref[...] * lk1_ref[...], axis=-1, keepdims=True))
               - jnp.exp(jnp.sum(lq2_ref[...] * lk2_ref[...], axis=-1, keepdims=True)) + lam_init)
        o1 = acc_s[0, :FLASH_DV, :] / acc_s[0, FLASH_DV:FLASH_DV + 1, :]
        o2 = acc_s[1, :FLASH_DV, :] / acc_s[1, FLASH_DV:FLASH_DV + 1, :]
        o = o1 - lam * o2
        r = lax.rsqrt(jnp.mean(o * o, axis=0, keepdims=True) + RMS_EPS)
        y = o * r * sub_ref[...] * (1.0 - lam_init)
        o_ref[0] = y.T.astype(o_ref.dtype)


def _split_bf16(x, n):
    parts = []
    for _ in range(n):
        p = x.astype(BF16)
        parts.append(p)
        x = x - p.astype(F32)
    return parts


def alibi_key_columns(slopes, t):
    pos = (jnp.arange(t) % FLASH_CHUNK).astype(F32)
    aug = jnp.stack(_split_bf16(slopes[:, None] * pos[None, :], FLASH_AUG), axis=-1)
    return jnp.concatenate([aug, jnp.zeros((slopes.shape[0], t, FLASH_DV - FLASH_AUG), BF16)], axis=-1)


def diff_attention(qt, proj3d, k_blk0, vt, norms, slopes, lam_init, lq1, lk1, lq2, lk2, subln):
    bsz, nh, _, seq = qt.shape
    t = min(T_FLASH, seq)
    assert seq % t == 0 and t % FLASH_PANEL == 0 and FLASH_PANEL % FLASH_CHUNK == 0
    nt = seq // t
    norms = norms.reshape(bsz, nt, 2, nh).transpose(2, 0, 3, 1)
    qn, kn = norms[0], norms[1]
    tile = jnp.arange(nt)
    kt = (tile[:, None] + tile[None, :]) % nt
    min_dist = jnp.maximum((jnp.abs(tile[:, None] - kt) - 1) * t + 1, 0).astype(F32)
    upper = qn[..., :, None] * jnp.take(kn, kt, axis=-1) - slopes[None, :, None, None] * min_dist
    lower_ref = -(qn * kn)[..., :, None]
    skip = jnp.logical_and(upper - lower_ref <= -FLASH_ZERO_LOG2, kt != tile[:, None])
    kt_cols = [jnp.broadcast_to(kt[:, 0], skip.shape[:-1])]
    for ki in range(1, nt):
        kt_cols.append(jnp.where(skip[..., ki], kt_cols[-1], kt[:, ki]))
    kt_eff = jnp.stack(kt_cols, axis=-1).astype(I32).reshape(-1)
    skip = skip.astype(I32).reshape(-1)

    def step_id(b, h, qi, ki):
        return ((b * nh + h) * nt + qi) * nt + ki

    k_cols = alibi_key_columns(slopes, t)
    dv = FLASH_DV
    dva = FLASH_DV + FLASH_ONES
    r = jnp.arange(FLASH_CHUNK, dtype=F32)
    diag = -slopes[:, None, None] * jnp.abs(r[None, :] - r[:, None])[None]
    diag = jnp.stack([jnp.zeros_like(diag), diag], axis=1)
    smem = pl.BlockSpec(memory_space=pltpu.SMEM)
    vec = lambda n: pl.BlockSpec((1, n), lambda b, h, qi, ki, kte, sk: (0, 0))
    grid_spec = pltpu.PrefetchScalarGridSpec(
        num_scalar_prefetch=2,
        grid=(bsz, nh, nt, nt),
        in_specs=[
            smem, smem, smem, smem,
            pl.BlockSpec((1, 1, dv, t), lambda b, h, qi, ki, kte, sk: (b, h, 0, qi)),
            pl.BlockSpec((1, t, dv), lambda b, h, qi, ki, kte, sk: (b, kte[step_id(b, h, qi, ki)], k_blk0 + h)),
            pl.BlockSpec((1, t, dv), lambda b, h, qi, ki, kte, sk: (h, 0, 0)),
            pl.BlockSpec((1, 1, dv, t), lambda b, h, qi, ki, kte, sk: (b, h, 0, kte[step_id(b, h, qi, ki)])),
            pl.BlockSpec((1, 1, FLASH_CHUNK, FLASH_CHUNK),
                         lambda b, h, qi, ki, kte, sk: (h, (ki == 0).astype(I32), 0, 0)),
            vec(HEAD_DIM), vec(HEAD_DIM), vec(HEAD_DIM), vec(HEAD_DIM),
            pl.BlockSpec((dv, 1), lambda b, h, qi, ki, kte, sk: (0, 0)),
        ],
        out_specs=pl.BlockSpec((1, t, LANES), lambda b, h, qi, ki, kte, sk: (b, qi, h)),
        scratch_shapes=[
            pltpu.VMEM((2, 1, t), F32),
            pltpu.VMEM((2, dva, t), F32),
            pltpu.VMEM((2, t, FLASH_PANEL), F32),
            pltpu.SMEM((1,), F32),
        ],
    )
    return pl.pallas_call(
        functools.partial(_flash_body, t=t, step_id=step_id),
        grid_spec=grid_spec,
        out_shape=jax.ShapeDtypeStruct((bsz, seq, nh * dv), BF16),
        compiler_params=_cparams(("parallel", "parallel", "parallel", "arbitrary")),
        name="diff_attention",
    )(kt_eff, skip, slopes, lam_init, qn.reshape(-1), kn.reshape(-1), qt, proj3d, k_cols, vt, diag,
      lq1, lk1, lq2, lk2, subln)


def _outproj_body(x_ref, oa_ref, ob_ref, oc_ref, wa_ref, wb_ref, wc_ref, g1_ref, sc_ref, sh_ref,
                  n2_ref, wr_ref, br_ref, x1_ref, hp_ref, lg_ref):
    mix = (jnp.dot(oa_ref[...], wa_ref[...], preferred_element_type=F32)
           + jnp.dot(ob_ref[...], wb_ref[...], preferred_element_type=F32)
           + jnp.dot(oc_ref[...], wc_ref[...], preferred_element_type=F32))
    x1 = x_ref[...] + g1_ref[0] * mix
    x1_ref[...] = x1
    h = x1 * _rms_scale(x1) * n2_ref[...]
    h = h * (1.0 + sc_ref[0]) + sh_ref[0]
    half = h.shape[1] // 2
    hp_ref[...] = _pack_halves(h[:, :half], h[:, half:])
    lg_ref[...] = lax.dot_general(wr_ref[...], h, _NT, preferred_element_type=F32,
                                  precision=HIGHEST) + br_ref[...]


def out_projection(x2d, oa, ob, oc, wa, wb, wc, gate1, scale2, shift2, gain2, w_router_t, b_router, seq):
    t, d = x2d.shape
    e = w_router_t.shape[0]
    tm = min(TM_PROJ, seq)
    mod_spec = pl.BlockSpec((1, 1, d), lambda i: ((i * tm) // seq, 0, 0))
    row = lambda w: pl.BlockSpec((tm, w), lambda i: (i, 0))
    full = lambda a: pl.BlockSpec(a.shape, lambda i: (0, 0))
    return pl.pallas_call(
        _outproj_body,
        grid=(t // tm,),
        in_specs=[row(d), row(oa.shape[1]), row(ob.shape[1]), row(oc.shape[1]),
                  full(wa), full(wb), full(wc), mod_spec, mod_spec, mod_spec,
                  pl.BlockSpec((1, d), lambda i: (0, 0)), full(w_router_t),
                  pl.BlockSpec((e, 1), lambda i: (0, 0))],
        out_specs=[row(d), row(d // 2), pl.BlockSpec((e, tm), lambda i: (0, i))],
        out_shape=[jax.ShapeDtypeStruct((t, d), F32),
                   jax.ShapeDtypeStruct((t, d // 2), U32),
                   jax.ShapeDtypeStruct((e, t), F32)],
        compiler_params=_cparams(("parallel",)),
        name="out_projection",
    )(x2d, oa, ob, oc, wa, wb, wc, gate1, scale2, shift2, gain2, w_router_t, b_router)


def _router_body(lg_ref, tri_ref, eidx_ref, gate_ref, rank_ref, cnt_ref, base_s):
    step = pl.program_id(0)

    @pl.when(step == 0)
    def _():
        base_s[...] = jnp.zeros(base_s.shape, F32)

    work = lg_ref[...]
    n_e, tm = work.shape
    eio = lax.broadcasted_iota(I32, (n_e, tm), 0).astype(F32)
    vals, hots = [], []
    for k in range(TOP_K):
        mx = jnp.max(work, axis=0, keepdims=True)
        idx = jnp.min(jnp.where(work == mx, eio, float(n_e)), axis=0, keepdims=True)
        hot = eio == idx
        vals.append(mx)
        hots.append(hot)
        eidx_ref[k:k + 1, :] = idx.astype(I32)
        work = jnp.where(hot, -jnp.inf, work)
    exps = [jnp.exp(v - vals[0]) for v in vals]
    den = functools.reduce(lambda a, b: a + b, exps)
    for k in range(TOP_K):
        gate_ref[k:k + 1, :] = exps[k] / den
    any_hot = functools.reduce(lambda a, b: a + b, [h.astype(F32) for h in hots])
    prefix = jnp.dot(any_hot.astype(BF16), tri_ref[...], preferred_element_type=F32)
    pos = base_s[...] + prefix
    for k in range(TOP_K):
        rank_ref[k:k + 1, :] = jnp.sum(jnp.where(hots[k], pos, 0.0), axis=0, keepdims=True).astype(I32)
    base_s[...] = base_s[...] + jnp.sum(any_hot, axis=1, keepdims=True)
    cnt_ref[...] = base_s[...].astype(I32)


def route(logits_t):
    n_e, t = logits_t.shape
    tm = min(TM_ROUTE, t)
    assert t % tm == 0
    r = jnp.arange(tm)
    tri = (r[:, None] < r[None, :]).astype(BF16)
    out4 = lambda dt: jax.ShapeDtypeStruct((TOP_K, t), dt)
    spec4 = pl.BlockSpec((TOP_K, tm), lambda i: (0, i))
    return pl.pallas_call(
        _router_body,
        grid=(t // tm,),
        in_specs=[pl.BlockSpec((n_e, tm), lambda i: (0, i)),
                  pl.BlockSpec((tm, tm), lambda i: (0, 0))],
        out_specs=[spec4, spec4, spec4, pl.BlockSpec((n_e, 1), lambda i: (0, 0))],
        out_shape=[out4(I32), out4(F32), out4(I32), jax.ShapeDtypeStruct((n_e, 1), I32)],
        scratch_shapes=[pltpu.VMEM((n_e, 1), F32)],
        compiler_params=_cparams(("arbitrary",)),
        name="route",
    )(logits_t, tri)


def _dispatch_body(d0_ref, d1_ref, d2_ref, d3_ref, h_ref, init_ref, xs_ref, sem, *, tm):
    del init_ref
    dest_refs = (d0_ref, d1_ref, d2_ref, d3_ref)

    def issue(tok, carry):
        for k in range(TOP_K):
            pltpu.make_async_copy(h_ref.at[pl.ds(tok, 1)], xs_ref.at[pl.ds(dest_refs[k][tok], 1)], sem).start(priority=k % N_DMA_THREADS)
        return carry

    lax.fori_loop(0, tm, issue, 0, unroll=DMA_ISSUE_UNROLL)
    for k in range(TOP_K):
        pltpu.make_async_copy(h_ref, xs_ref.at[pl.ds(0, tm)], sem).wait()


def dispatch_rows(h_packed, dest, n_rows):
    t, w = h_packed.shape
    tm = min(TM_DISPATCH, t)
    assert TOP_K == 4
    nsteps = t // tm
    flat = dest.reshape(-1)
    return pl.pallas_call(
        functools.partial(_dispatch_body, tm=tm),
        grid=(nsteps,),
        in_specs=[pl.BlockSpec((tm,), functools.partial(lambda k, i: (k * nsteps + i,), k), memory_space=pltpu.SMEM)
                  for k in range(TOP_K)] + [
                  pl.BlockSpec((tm, w), lambda i: (i, 0)),
                  pl.BlockSpec(memory_space=pl.ANY)],
        out_specs=pl.BlockSpec(memory_space=pl.ANY),
        out_shape=jax.ShapeDtypeStruct((n_rows, w), U32),
        input_output_aliases={TOP_K + 1: 0},
        scratch_shapes=[pltpu.SemaphoreType.DMA],
        compiler_params=_cparams(("arbitrary",)),
        name="dispatch_rows",
    )(flat, flat, flat, flat, h_packed, jnp.zeros((n_rows, w), U32))


def _ffn_body(be_ref, bv_ref, nu_ref, xs_ref, wg_ref, bg_ref, wu_ref, bu_ref, wd_ref, bd_ref, o_ref,
              wg_s, wu_s, wd_s):
    i = pl.program_id(0)

    @pl.when(i < nu_ref[0])
    def _():
        @pl.when(jnp.logical_or(i == 0, be_ref[i] != be_ref[jnp.maximum(i - 1, 0)]))
        def _():
            wg_s[...] = wg_ref[0, 0].astype(BF16)
            wu_s[...] = wu_ref[0, 0].astype(BF16)
            wd_s[...] = wd_ref[0, 0].astype(BF16)

        xa, xb = _unpack_halves(xs_ref[...])
        blk, half = xa.shape
        valid = lax.broadcasted_iota(I32, (blk, 1), 0) < bv_ref[i]
        xa = jnp.where(valid, xa, 0.0).astype(BF16)
        xb = jnp.where(valid, xb, 0.0).astype(BF16)

        def proj(w_s, b_ref):
            return (jnp.dot(xa, w_s[:half, :], preferred_element_type=F32)
                    + jnp.dot(xb, w_s[half:, :], preferred_element_type=F32) + b_ref[0, 0])

        a = jnp.minimum(proj(wg_s, bg_ref), SWIGLU_LIMIT)
        u = jnp.clip(proj(wu_s, bu_ref), -SWIGLU_LIMIT, SWIGLU_LIMIT)
        hid = a * (1.0 / (1.0 + jnp.exp(-SWIGLU_ALPHA * a))) * (u + 1.0)
        out = jnp.dot(hid.astype(BF16), wd_s[...], preferred_element_type=F32) + bd_ref[0, 0]
        d = out.shape[1] // 2
        o_ref[...] = _pack_halves(out[:, :d], out[:, d:])

    @pl.when(i >= nu_ref[0])
    def _():
        o_ref[...] = jnp.zeros(o_ref.shape, o_ref.dtype)


def expert_ffn(xs, blk_expert, blk_valid, n_used, layer, wg, bg, wu, bu, wd, bd):
    n_rows, w = xs.shape
    _, n_e, d, f = wg.shape
    nb = n_rows // BLK_FFN
    row_map = lambda i, be, bv, nu: (jnp.minimum(i, nu[0] - 1), 0)
    exp_map = lambda i, be, bv, nu: (layer, be[i], 0, 0)
    grid_spec = pltpu.PrefetchScalarGridSpec(
        num_scalar_prefetch=3,
        grid=(nb,),
        in_specs=[pl.BlockSpec((BLK_FFN, w), row_map),
                  pl.BlockSpec((1, 1, d, f), exp_map), pl.BlockSpec((1, 1, 1, f), exp_map),
                  pl.BlockSpec((1, 1, d, f), exp_map), pl.BlockSpec((1, 1, 1, f), exp_map),
                  pl.BlockSpec((1, 1, f, d), exp_map), pl.BlockSpec((1, 1, 1, d), exp_map)],
        out_specs=pl.BlockSpec((BLK_FFN, d // 2), lambda i, be, bv, nu: (i, 0)),
        scratch_shapes=[pltpu.VMEM((d, f), BF16), pltpu.VMEM((d, f), BF16), pltpu.VMEM((f, d), BF16)],
    )
    return pl.pallas_call(
        _ffn_body,
        grid_spec=grid_spec,
        out_shape=jax.ShapeDtypeStruct((n_rows, d // 2), U32),
        compiler_params=_cparams(("arbitrary",)),
        name="expert_ffn",
    )(blk_expert, blk_valid, n_used, xs, wg, bg, wu, bu, wd, bd)


def _combine_steps(dcur_ref, dnext_ref, x_ref, gt_ref, g2_ref, ys_ref, bufs, sem, tm, finish):
    i = pl.program_id(0)

    def start(dref, col, tok, k, s):
        pltpu.make_async_copy(ys_ref.at[pl.ds(dref[k, col + tok], 1)], bufs[s].at[k, pl.ds(tok, 1)], sem.at[s]).start(priority=k % N_DMA_THREADS)

    def wait(s):
        for k in range(TOP_K):
            pltpu.make_async_copy(ys_ref.at[pl.ds(0, tm)], bufs[s].at[k], sem.at[s]).wait()

    def issue_inline(dref, col, s):
        for tok in range(tm):
            for k in range(TOP_K):
                start(dref, col, tok, k, s)

    def compute(s):
        rows = slice(s * tm, (s + 1) * tm)
        gt = gt_ref[rows, :]
        lo = hi = None
        for k in range(TOP_K):
            a, b = _unpack_halves(bufs[s][k])
            g = gt[:, k:k + 1]
            lo = g * a if lo is None else lo + g * a
            hi = g * b if hi is None else hi + g * b
        half = lo.shape[1]
        g2 = g2_ref[0]
        x2 = jnp.concatenate([x_ref[rows, :half] + g2[:, :half] * lo, x_ref[rows, half:] + g2[:, half:] * hi],
                             axis=1)
        finish(s, rows, x2)

    @pl.when(i == 0)
    def _():
        def issue(tok, carry):
            for k in range(TOP_K):
                start(dcur_ref, 0, tok, k, 0)
            return carry

        lax.fori_loop(0, tm, issue, 0, unroll=DMA_ISSUE_UNROLL)

    wait(0)
    issue_inline(dcur_ref, tm, 1)
    compute(0)
    wait(1)
    issue_inline(dnext_ref, 0, 0)
    compute(1)

    @pl.when(i == pl.num_programs(0) - 1)
    def _():
        wait(0)


def _combine_project_body(dcur_ref, dnext_ref, x_ref, gt_ref, g2_ref, sc_ref, sh_ref, gn_ref, w_ref, ys_ref,
                          x2_ref, proj_ref, nrm_ref, buf0, buf1, sem, *, tm, norm_cols):
    def finish(s, rows, x2):
        x2_ref[rows, :] = x2
        _project_rows(x2, sc_ref[0], sh_ref[0], gn_ref[...], w_ref, proj_ref.at[rows, :], nrm_ref.at[s:s + 1],
                      norm_cols)

    _combine_steps(dcur_ref, dnext_ref, x_ref, gt_ref, g2_ref, ys_ref, (buf0, buf1), sem, tm, finish)


def _combine_final_body(dcur_ref, dnext_ref, x_ref, gt_ref, g2_ref, nf_ref, ys_ref, o_ref, buf0, buf1, sem, *, tm):
    def finish(s, rows, x2):
        o_ref[rows, :] = x2 * _rms_scale(x2) * nf_ref[...]

    _combine_steps(dcur_ref, dnext_ref, x_ref, gt_ref, g2_ref, ys_ref, (buf0, buf1), sem, tm, finish)


def combine(x1, ys, dest, gates_t, gate2, seq, project=None, norm_f=None, norm_cols=()):
    t, d = x1.shape
    tm = min(TM_COMBINE, seq // 2)
    nblocks = t // tm
    nsteps = nblocks // 2
    assert t % (2 * tm) == 0 and seq % (2 * tm) == 0
    mod_spec = pl.BlockSpec((1, 1, d), lambda i: ((i * 2 * tm) // seq, 0, 0))
    row = lambda w: pl.BlockSpec((2 * tm, w), lambda i: (i, 0))
    vec_spec = pl.BlockSpec((1, d), lambda i: (0, 0))
    head_specs = [pl.BlockSpec((TOP_K, 2 * tm), lambda i: (0, i), memory_space=pltpu.SMEM),
                  pl.BlockSpec((TOP_K, tm), lambda i: (0, jnp.minimum(2 * i + 2, nblocks - 1)),
                               memory_space=pltpu.SMEM),
                  row(d), row(TOP_K), mod_spec]
    any_spec = pl.BlockSpec(memory_space=pl.ANY)
    scratch = [pltpu.VMEM((TOP_K, tm, d // 2), U32), pltpu.VMEM((TOP_K, tm, d // 2), U32),
               pltpu.SemaphoreType.DMA((2,))]
    if project is None:
        return pl.pallas_call(
            functools.partial(_combine_final_body, tm=tm),
            grid=(nsteps,),
            in_specs=head_specs + [vec_spec, any_spec],
            out_specs=row(d),
            out_shape=jax.ShapeDtypeStruct((t, d), F32),
            scratch_shapes=scratch,
            compiler_params=_cparams(("arbitrary",)),
            name="combine_final",
        )(dest, dest, x1, gates_t, gate2, norm_f, ys)
    scale, shift, gain, w_bf16 = project
    n = w_bf16.shape[1]
    return pl.pallas_call(
        functools.partial(_combine_project_body, tm=tm, norm_cols=norm_cols),
        grid=(nsteps,),
        in_specs=head_specs + [mod_spec, mod_spec, vec_spec, pl.BlockSpec((d, n), lambda i: (0, 0)), any_spec],
        out_specs=[row(d), row(n), pl.BlockSpec((2, 1, len(norm_cols)), lambda i: (i, 0, 0))],
        out_shape=[jax.ShapeDtypeStruct((t, d), F32),
                   jax.ShapeDtypeStruct((t, n), BF16),
                   jax.ShapeDtypeStruct((nblocks, 1, len(norm_cols)), F32)],
        scratch_shapes=scratch,
        compiler_params=_cparams(("arbitrary",)),
        name="combine_project",
    )(dest, dest, x1, gates_t, gate2, scale, shift, gain, w_bf16, ys)


def _alibi_slopes(n):
    return jnp.exp2(-8.0 * jnp.arange(1, n + 1, dtype=F32) / n)


def _routing_tables(counts, eidx, rank, nb):
    counts = counts.reshape(-1)
    padded = (counts + BLK_FFN - 1) // BLK_FFN * BLK_FFN
    pad_end = jnp.cumsum(padded)
    pad_start = pad_end - padded
    e_ids = jnp.arange(N_EXPERTS, dtype=I32)[:, None, None]
    dest = rank + jnp.sum(jnp.where(eidx[None] == e_ids, pad_start[:, None, None], 0), axis=0)
    blk_row0 = jnp.arange(nb, dtype=I32) * BLK_FFN
    blk_expert = jnp.minimum(jnp.sum(pad_end[None, :] <= blk_row0[:, None], axis=1), N_EXPERTS - 1).astype(I32)
    blk_valid = jnp.clip(counts[blk_expert] - (blk_row0 - pad_start[blk_expert]), 0, BLK_FFN).astype(I32)
    n_used = (pad_end[-1:] // BLK_FFN).astype(I32)
    return dest.astype(I32), blk_expert, blk_valid, n_used


def kernel(x, c, w_ada, b_ada, norm1, w_in, sinks, lambda_q1, lambda_k1, lambda_q2, lambda_k2, subln,
           w_out, norm2, w_router, b_router, w_gate, b_gate, w_up, b_up, w_down, b_down, norm_f):
    bsz, seq, d = x.shape
    depth = w_ada.shape[0]
    t = bsz * seq
    wa_w, wb_w = N_HEADS_A * HEAD_DIM, N_HEADS_B * HEAD_DIM
    kvb_w = N_KV_B * HEAD_DIM
    wc_w = N_HEADS_C * 2 * HEAD_DIM
    off = [0]
    for wdt in (wa_w, wa_w, wa_w, wb_w, kvb_w, kvb_w, wc_w, wc_w, wc_w):
        off.append(off[-1] + wdt)
    blk = [o // LANES for o in off]
    norm_cols = tuple(off[i] + hh * FLASH_DV for i in (6, 7) for hh in range(N_HEADS_C))

    group = N_HEADS_B // N_KV_B
    b_perm = [kv * group + g for g in range(group) for kv in range(N_KV_B)]
    col_perm = jnp.concatenate([jnp.arange(h * HEAD_DIM, (h + 1) * HEAD_DIM) for h in b_perm])
    q_scale = jnp.ones((off[-1],), F32)
    for s0, s1, extra in ((off[0], off[1], 1.0), (off[3], off[4], 1.0), (off[6], off[7], LOG2E)):
        q_scale = q_scale.at[s0:s1].set(HEAD_DIM ** -0.5 * extra)

    slopes_a = _alibi_slopes(N_HEADS_A)
    slopes_b = _alibi_slopes(N_HEADS_B)[jnp.array(b_perm)]
    slopes_c = _alibi_slopes(N_HEADS_C) * LOG2E

    mod_all = ada_modulation(c, w_ada, b_ada)
    x2d = x.reshape(t, d)
    nb = (t * TOP_K) // BLK_FFN + N_EXPERTS
    n_rows = nb * BLK_FFN

    def layer_mod(layer):
        mod = mod_all[layer].reshape(bsz, N_MOD, 1, d)
        return [mod[:, i] for i in range(N_MOD)]

    def in_weights(layer):
        w_l = w_in[layer] * q_scale[None, :]
        w_l = jnp.concatenate([w_l[:, :off[3]], w_l[:, off[3]:off[4]][:, col_perm], w_l[:, off[4]:]], axis=1)
        return w_l.astype(BF16)

    proj = blk_norms = None
    for layer in range(depth):
        sh1, sc1, g1, sh2, sc2, g2 = layer_mod(layer)
        if layer == 0:
            proj, blk_norms = in_projection(x2d, sc1, sh1, norm1[layer].reshape(1, d), in_weights(layer), seq,
                                            norm_cols)
        proj3d = proj.reshape(bsz, seq, off[-1])

        n_blk_a = wa_w // LANES
        o_a = window_attention(proj3d, blk[0], blk[1], blk[2], 1, n_blk_a, slopes_a,
                               jnp.full((n_blk_a, 2), -jnp.inf, F32), DILATED_PAIRS).reshape(t, wa_w)

        sink_l = sinks[layer].astype(F32)[jnp.array(b_perm)].reshape(group, N_KV_B)
        o_b = window_attention(proj3d, blk[3], blk[4], blk[5], 0, group, slopes_b, sink_l,
                               ((2 * SWA_HALF, 1),)).reshape(t, wb_w)

        lam_init = jnp.full((1,), 0.8 - 0.6 * math.exp(-0.3 * layer), F32)
        heads_t = lambda i: (proj3d[:, :, off[i]:off[i + 1]].reshape(bsz, seq, N_HEADS_C, FLASH_DV)
                             .transpose(0, 2, 3, 1))
        t_flash = min(T_FLASH, seq)
        norms = blk_norms.reshape(bsz, seq // t_flash, -1, 2 * N_HEADS_C).max(axis=2, keepdims=True)
        o_c = diff_attention(heads_t(6), proj3d, blk[7], heads_t(8), norms, slopes_c, lam_init,
                             lambda_q1[layer].reshape(1, -1), lambda_k1[layer].reshape(1, -1),
                             lambda_q2[layer].reshape(1, -1), lambda_k2[layer].reshape(1, -1),
                             subln[layer].reshape(-1, 1)).reshape(t, wc_w)

        wo = w_out[layer]
        wo_b = wo[wa_w:wa_w + wb_w].reshape(N_HEADS_B, HEAD_DIM, d)[jnp.array(b_perm)].reshape(wb_w, d)
        x1, h_packed, logits_t = out_projection(
            x2d, o_a, o_b, o_c, wo[:wa_w].astype(BF16), wo_b.astype(BF16), wo[wa_w + wb_w:].astype(BF16),
            g1, sc2, sh2, norm2[layer].reshape(1, d), w_router[layer].T, b_router[layer].reshape(-1, 1), seq)

        eidx, gates, rank, counts = route(logits_t)
        dest, blk_expert, blk_valid, n_used = _routing_tables(counts, eidx, rank, nb)
        xs = dispatch_rows(h_packed, dest, n_rows)
        ys = expert_ffn(xs, blk_expert, blk_valid, n_used, layer,
                        w_gate, b_gate[:, :, None, :], w_up, b_up[:, :, None, :], w_down, b_down[:, :, None, :])
        if layer == depth - 1:
            x2d = combine(x1, ys, dest, gates.T, g2, seq, norm_f=norm_f.reshape(1, d))
        else:
            nsh1, nsc1 = layer_mod(layer + 1)[:2]
            x2d, proj, blk_norms = combine(
                x1, ys, dest, gates.T, g2, seq, norm_cols=norm_cols,
                project=(nsc1, nsh1, norm1[layer + 1].reshape(1, d), in_weights(layer + 1)))

    return x2d.reshape(bsz, seq, d)
```

```python
import functools
import math

import jax
import jax.numpy as jnp
from jax import lax
from jax.experimental import pallas as pl
from jax.experimental.pallas import tpu as pltpu

F32 = jnp.float32
BF16 = jnp.bfloat16
I32 = jnp.int32
U32 = jnp.uint32

HEAD_DIM = 64
DILATED_PAIRS = ((128, 1), (512, 4), (2048, 16))
N_HEADS_A = 4
N_HEADS_B = 4
N_KV_B = 2
SWA_HALF = 128
N_HEADS_C = 4
N_EXPERTS = 32
TOP_K = 4
SWIGLU_LIMIT = 7.0
SWIGLU_ALPHA = 1.702
RMS_EPS = 1e-6
N_MOD = 6

LANES = 128
VMEM_LIMIT_BYTES = 56 * 1024 * 1024

TM_PROJ = 512
TQ_BAND = 256
T_FLASH = 1024
FLASH_PANEL = 512
FLASH_CHUNK = 256
FLASH_SUB = 128
FLASH_DV = 2 * HEAD_DIM
FLASH_ONES = 16
FLASH_AUG = 3
FLASH_OWN_LOG2 = 100.0
FLASH_ZERO_LOG2 = 150.0
FLASH_LOOKAHEAD = 2
FLASH_SAFE_LOG2 = 60.0
LOG2E = math.log2(math.e)
TM_ROUTE = 512
TM_DISPATCH = 512
BLK_FFN = 512
TM_COMBINE = 256
DMA_ISSUE_UNROLL = 8
N_DMA_THREADS = 2

HIGHEST = lax.Precision.HIGHEST
_NT = (((1,), (1,)), ((), ()))


def _cparams(sem):
    return pltpu.CompilerParams(dimension_semantics=sem, vmem_limit_bytes=VMEM_LIMIT_BYTES)


def _pack_halves(a, b):
    ua = lax.bitcast_convert_type(a.astype(BF16).astype(F32), U32)
    ub = lax.bitcast_convert_type(b.astype(BF16).astype(F32), U32)
    return ua | (ub >> 16)


def _unpack_halves(w):
    a = lax.bitcast_convert_type(w & jnp.uint32(0xFFFF0000), F32)
    b = lax.bitcast_convert_type(w << 16, F32)
    return a, b


def _rms_scale(x):
    return lax.rsqrt(jnp.mean(x * x, axis=-1, keepdims=True) + RMS_EPS)


def _ada_body(c_ref, w_ref, b_ref, o_ref):
    c = c_ref[...]
    cond = c / (1.0 + jnp.exp(-c))
    o_ref[0] = jnp.dot(cond, w_ref[0], preferred_element_type=F32, precision=HIGHEST) + b_ref[0]


def ada_modulation(c, w_ada, b_ada):
    depth, d, n = w_ada.shape
    bsz = c.shape[0]
    tn = 1536
    assert n % tn == 0
    return pl.pallas_call(
        _ada_body,
        grid=(depth, n // tn),
        in_specs=[
            pl.BlockSpec((bsz, d), lambda l, j: (0, 0)),
            pl.BlockSpec((1, d, tn), lambda l, j: (l, 0, j)),
            pl.BlockSpec((1, 1, tn), lambda l, j: (l, 0, j)),
        ],
        out_specs=pl.BlockSpec((1, bsz, tn), lambda l, j: (l, 0, j)),
        out_shape=jax.ShapeDtypeStruct((depth, bsz, n), F32),
        compiler_params=_cparams(("parallel", "parallel")),
        name="ada_modulation",
    )(c, w_ada, b_ada.reshape(depth, 1, n))


def _project_rows(x, sc, sh, g, w_ref, o_ref, n_ref, norm_cols):
    h = x * _rms_scale(x) * g
    h = h * (1.0 + sc) + sh
    out = jnp.dot(h.astype(BF16), w_ref[...], preferred_element_type=F32).astype(BF16)
    o_ref[...] = out
    lane = lax.broadcasted_iota(I32, (1, len(norm_cols)), 1)
    acc = jnp.zeros((1, len(norm_cols)), F32)
    for j, c0 in enumerate(norm_cols):
        v = out[:, c0:c0 + FLASH_DV].astype(F32)
        n2 = jnp.max(jnp.sum(v * v, axis=1, keepdims=True), axis=0, keepdims=True)
        acc = jnp.where(lane == j, jnp.sqrt(n2), acc)
    n_ref[0] = acc


def _inproj_body(x_ref, sc_ref, sh_ref, g_ref, w_ref, o_ref, n_ref, *, norm_cols):
    _project_rows(x_ref[...], sc_ref[0], sh_ref[0], g_ref[...], w_ref, o_ref, n_ref, norm_cols)


def in_projection(x2d, scale, shift, gain, w_bf16, seq, norm_cols):
    t, d = x2d.shape
    n = w_bf16.shape[1]
    tm = min(TM_PROJ, seq)
    assert seq % tm == 0 and t % tm == 0
    mod_spec = pl.BlockSpec((1, 1, d), lambda i: ((i * tm) // seq, 0, 0))
    return pl.pallas_call(
        functools.partial(_inproj_body, norm_cols=norm_cols),
        grid=(t // tm,),
        in_specs=[
            pl.BlockSpec((tm, d), lambda i: (i, 0)),
            mod_spec, mod_spec,
            pl.BlockSpec((1, d), lambda i: (0, 0)),
            pl.BlockSpec((d, n), lambda i: (0, 0)),
        ],
        out_specs=[pl.BlockSpec((tm, n), lambda i: (i, 0)),
                   pl.BlockSpec((1, 1, len(norm_cols)), lambda i: (i, 0, 0))],
        out_shape=[jax.ShapeDtypeStruct((t, n), BF16),
                   jax.ShapeDtypeStruct((t // tm, 1, len(norm_cols)), F32)],
        compiler_params=_cparams(("parallel",)),
        name="in_projection",
    )(x2d, scale, shift, gain, w_bf16)


def _offset_bias_body(c_ref, slope_ref, o_ref, *, pairs):
    vi = pl.program_id(0)
    hd = pl.program_id(1)
    tq, win = o_ref.shape[2], o_ref.shape[3]
    off = (lax.broadcasted_iota(I32, (tq, win), 1) - lax.broadcasted_iota(I32, (tq, win), 0)) + c_ref[vi]
    dist = jnp.abs(off)
    mult = jnp.zeros((tq, win), F32)
    for window, dil in pairs:
        hit = jnp.logical_and(dist <= window // 2, (dist & (dil - 1)) == 0)
        mult = mult + jnp.where(hit, 1.0, 0.0)
    o_ref[0, 0] = jnp.log(mult) - slope_ref[hd] * dist.astype(F32)


def _window_variants(seq, tq, reach):
    win = tq + 2 * reach
    nq, edge = seq // tq, -(-reach // tq)
    assert seq % tq == 0 and reach % 16 == 0 and win <= seq and nq >= 2 * edge + 1
    rel = [min(max(i * tq - reach, 0), seq - win) - i * tq for i in range(nq)]
    assert all(r == -reach for r in rel[edge:nq - edge])
    starts = rel[:edge] + [-reach] + rel[nq - edge:]

    def variant(i):
        return jnp.where(i < edge, i, jnp.where(i >= nq - edge, i - (nq - 2 * edge - 1), edge))

    return win, starts, variant


def _window_body(c_ref, sink_ref, q_ref, k_ref, v_ref, tab_ref, o_ref, *, tq, win, variant):
    j = pl.program_id(1)
    i = pl.program_id(2)
    ks = pl.multiple_of(i * tq + c_ref[variant(i)], 16)
    q = q_ref[0]
    k = k_ref[0, pl.ds(ks, win), :]
    v = v_ref[0, pl.ds(ks, win), :]
    lane_lo = lax.broadcasted_iota(I32, (1, LANES), 1) < HEAD_DIM
    outs = []
    for hf in range(2):
        qm = jnp.where(lane_lo if hf == 0 else jnp.logical_not(lane_lo), q, jnp.zeros_like(q))
        s = lax.dot_general(qm, k, _NT, preferred_element_type=F32) + tab_ref[0, hf]
        sink = sink_ref[j, hf]
        m = jnp.maximum(jnp.max(s, axis=-1, keepdims=True), sink)
        p = jnp.exp(s - m)
        l = jnp.sum(p, axis=-1, keepdims=True) + jnp.exp(sink - m)
        outs.append(jnp.dot(p.astype(BF16), v, preferred_element_type=F32) / l)
    o_ref[0] = jnp.where(lane_lo, outs[0], outs[1]).astype(o_ref.dtype)


def window_attention(proj3d, q_blk0, k_blk0, v_blk0, kv_step, n_blk, slopes, sinks, pairs):
    bsz, seq, _ = proj3d.shape
    assert all(d & (d - 1) == 0 for _, d in pairs)
    reach = max(w // 2 for w, _ in pairs)
    tq = TQ_BAND
    win, starts, variant = _window_variants(seq, tq, reach)
    starts = jnp.array(starts, I32)
    smem = pl.BlockSpec(memory_space=pltpu.SMEM)
    tables = pl.pallas_call(
        functools.partial(_offset_bias_body, pairs=pairs),
        grid=(starts.shape[0], 2 * n_blk),
        in_specs=[smem, smem],
        out_specs=pl.BlockSpec((1, 1, tq, win), lambda vi, hd: (vi, hd, 0, 0)),
        out_shape=jax.ShapeDtypeStruct((starts.shape[0], 2 * n_blk, tq, win), F32),
        compiler_params=_cparams(("parallel", "parallel")),
        name="offset_bias_tables",
    )(starts, slopes)
    return pl.pallas_call(
        functools.partial(_window_body, tq=tq, win=win, variant=variant),
        grid=(bsz, n_blk, seq // tq),
        in_specs=[
            smem, smem,
            pl.BlockSpec((1, tq, LANES), lambda b, j, i: (b, i, q_blk0 + j)),
            pl.BlockSpec((1, seq, LANES), lambda b, j, i: (b, 0, k_blk0 + kv_step * j)),
            pl.BlockSpec((1, seq, LANES), lambda b, j, i: (b, 0, v_blk0 + kv_step * j)),
            pl.BlockSpec((1, 2, tq, win), lambda b, j, i: (variant(i), j, 0, 0)),
        ],
        out_specs=pl.BlockSpec((1, tq, LANES), lambda b, j, i: (b, i, j)),
        out_shape=jax.ShapeDtypeStruct((bsz, seq, n_blk * LANES), BF16),
        compiler_params=_cparams(("parallel", "parallel", "parallel")),
        name="window_attention",
    )(starts, sinks, proj3d, proj3d, proj3d, tables)


def _flash_body(kte_ref, skip_ref, slope_ref, lami_ref, qn_ref, kn_ref, qt_ref, k_ref, kaug_ref, vt_ref, diag_ref,
                lq1_ref, lk1_ref, lq2_ref, lk2_ref, sub_ref, o_ref, m_s, acc_s, s_scr, mmin_s, *, t, step_id):
    b = pl.program_id(0)
    h = pl.program_id(1)
    qi = pl.program_id(2)
    ki = pl.program_id(3)
    nt = pl.num_programs(3)
    kt = lax.rem(qi + ki, nt)

    @pl.when(ki == 0)
    def _():
        m_s[...] = jnp.full(m_s.shape, -jnp.inf, F32)
        acc_s[...] = jnp.zeros(acc_s.shape, F32)
        mmin_s[0] = jnp.float32(0.0)

    slope = slope_ref[h]
    n_chunk = t // FLASH_CHUNK
    n_panel = t // FLASH_PANEL
    sub_per_panel = FLASH_PANEL // FLASH_CHUNK
    row = lax.broadcasted_iota(I32, (FLASH_DV, 1), 0)
    row_lo = row < HEAD_DIM
    row_aug = lax.broadcasted_iota(I32, (FLASH_ONES, 1), 0) < FLASH_AUG
    lane_pos = lax.broadcasted_iota(I32, (1, FLASH_CHUNK), 1).astype(F32)
    ones_rows = jnp.ones((FLASH_ONES, FLASH_CHUNK), BF16)

    def with_sign(qtm, sign):
        aug = jnp.where(row_aug, sign, 0.0).astype(BF16)
        pad = jnp.zeros((FLASH_DV - FLASH_ONES, sign.shape[1]), BF16)
        return jnp.concatenate([qtm, aug, pad], axis=0)

    def keys(kc):
        rows = slice(kc * FLASH_CHUNK, (kc + 1) * FLASH_CHUNK)
        return jnp.concatenate([k_ref[0, rows, :], kaug_ref[0, rows, :]], axis=1)

    def values(kc):
        return jnp.concatenate([vt_ref[0, 0, :, kc * FLASH_CHUNK:(kc + 1) * FLASH_CHUNK], ones_rows], axis=0)
    blocks = [(qp, mp) for qp in range(n_panel) for mp in range(2)]

    def fold8(x, op):
        return functools.reduce(op, [x[r:r + 8] for r in range(0, x.shape[0], 8)])

    def pair_terms(qp, kc):
        signs, shifts = [], []
        for j in range(sub_per_panel):
            gap = (qi * n_chunk + qp * sub_per_panel + j) - (kt * n_chunk + kc)
            sign = jnp.sign(gap).astype(F32)
            signs.append(jnp.full((1, FLASH_CHUNK), sign, F32))
            shifts.append(-slope * ((jnp.abs(gap) * FLASH_CHUNK).astype(F32) + sign * lane_pos))
        return jnp.concatenate(signs, axis=1), jnp.concatenate(shifts, axis=1)

    def masked_queries(bi):
        qp, mp = blocks[bi]
        qt = qt_ref[0, 0, :, qp * FLASH_PANEL:(qp + 1) * FLASH_PANEL]
        return jnp.where(row_lo if mp == 0 else jnp.logical_not(row_lo), qt, jnp.zeros_like(qt))

    def score_chunk(bi, kc, qtm, mx):
        qp, _ = blocks[bi]
        k0 = kc * FLASH_CHUNK
        sign, lane_shift = pair_terms(qp, kc)
        s = jnp.dot(keys(kc), with_sign(qtm, sign), preferred_element_type=F32)
        j = kc - qp * sub_per_panel
        if 0 <= j < sub_per_panel:
            cols = [s[:, i * FLASH_CHUNK:(i + 1) * FLASH_CHUNK] for i in range(sub_per_panel)]
            cols[j] = cols[j] + diag_ref[0, 0]
            s = jnp.concatenate(cols, axis=1)
        s_scr[bi % 2, k0:k0 + FLASH_CHUNK, :] = s
        cm = fold8(s, jnp.maximum) + lane_shift
        return cm if mx is None else jnp.maximum(mx, cm)

    def prob_chunk(bi, kc, m_new, acc):
        qp, _ = blocks[bi]
        k0 = kc * FLASH_CHUNK
        shift = m_new - pair_terms(qp, kc)[1]
        parts = []
        for r0 in range(k0, k0 + FLASH_CHUNK, FLASH_SUB):
            parts.append(jnp.exp2((s_scr[bi % 2, r0:r0 + FLASH_SUB, :] - shift).astype(BF16)))
        pv = jnp.dot(values(kc), jnp.concatenate(parts, axis=0), preferred_element_type=F32)
        return pv if acc is None else acc + pv

    def two_pass_tile():
        qtm_next = masked_queries(0)
        mx_next = None
        for kc in range(n_chunk):
            mx_next = score_chunk(0, kc, qtm_next, mx_next)
        for bi, (qp, mp) in enumerate(blocks):
            qs = slice(qp * FLASH_PANEL, (qp + 1) * FLASH_PANEL)
            m_old = m_s[mp, :, qs]
            m_new = jnp.maximum(m_old, jnp.max(mx_next, axis=0, keepdims=True))
            alpha = jnp.exp2(m_old - m_new)
            acc = None
            has_next = bi + 1 < len(blocks)
            if has_next:
                qtm_next = masked_queries(bi + 1)
            mx_next = None
            for kc in range(n_chunk):
                if has_next:
                    mx_next = score_chunk(bi + 1, kc, qtm_next, mx_next)
                acc = prob_chunk(bi, kc, m_new, acc)
            acc_s[mp, :, qs] = alpha * acc_s[mp, :, qs] + acc
            m_s[mp, :, qs] = m_new
        mmin_s[0] = jnp.min(m_s[...])

    def one_pass_tile(diagonal):
        items = [(bi, kc) for bi in range(len(blocks)) for kc in range(n_chunk)]

        def scores(bi, kc):
            qp, mp = blocks[bi]
            sign, lane_shift = pair_terms(qp, kc)
            s = jnp.dot(keys(kc), with_sign(masked_queries(bi), sign), preferred_element_type=F32)
            j = kc - qp * sub_per_panel
            if diagonal and 0 <= j < sub_per_panel:
                cols = [s[:, c * FLASH_CHUNK:(c + 1) * FLASH_CHUNK] for c in range(sub_per_panel)]
                cols[j] = cols[j] + diag_ref[0, 0]
                s = jnp.concatenate(cols, axis=1)
            return s, m_s[mp, :, qp * FLASH_PANEL:(qp + 1) * FLASH_PANEL] - lane_shift

        ahead = [scores(*items[i]) for i in range(FLASH_LOOKAHEAD)]
        acc = None
        for n, (bi, kc) in enumerate(items):
            s, shift = ahead.pop(0)
            if n + FLASH_LOOKAHEAD < len(items):
                ahead.append(scores(*items[n + FLASH_LOOKAHEAD]))
            p = jnp.exp2(s - shift).astype(BF16)
            pv = jnp.dot(values(kc), p, preferred_element_type=F32)
            acc = pv if kc == 0 else acc + pv
            if kc == n_chunk - 1:
                qp, mp = blocks[bi]
                qs = slice(qp * FLASH_PANEL, (qp + 1) * FLASH_PANEL)
                acc_s[mp, :, qs] = acc_s[mp, :, qs] + acc

    bound = qn_ref[(b * pl.num_programs(1) + h) * nt + qi] * kn_ref[(b * pl.num_programs(1) + h) * nt + kt]
    min_dist = jnp.maximum((jnp.abs(qi - kt) - 1) * t + 1, 0).astype(F32)
    headroom = bound - mmin_s[0]
    all_zero = jnp.logical_and(ki > 0, headroom - slope * min_dist <= -FLASH_ZERO_LOG2)
    all_zero = jnp.logical_or(all_zero, skip_ref[step_id(b, h, qi, ki)] == 1)
    safe = jnp.logical_and(ki > 0, headroom <= FLASH_SAFE_LOG2)
    own_one_pass = jnp.logical_and(ki == 0, 2.0 * bound <= FLASH_OWN_LOG2)

    @pl.when(own_one_pass)
    def _():
        m_s[...] = jnp.zeros(m_s.shape, F32) + bound
        mmin_s[0] = bound
        one_pass_tile(True)

    pl.when(jnp.logical_and(safe, jnp.logical_not(all_zero)))(functools.partial(one_pass_tile, False))
    pl.when(jnp.logical_not(jnp.logical_or(jnp.logical_or(safe, all_zero), own_one_pass)))(two_pass_tile)

    @pl.when(ki == pl.num_programs(3) - 1)
    def _():
        lam_init = lami_ref[0]
        lam = (jnp.exp(jnp.sum(lq1_ref[...] * lk1_ref[...], axis=-1, keepdims=True))
               - jnp.exp(jnp.sum(lq2_ref[...] * lk2_ref[...], axis=-1, keepdims=True)) + lam_init)
        o1 = acc_s[0, :FLASH_DV, :] / acc_s[0, FLASH_DV:FLASH_DV + 1, :]
        o2 = acc_s[1, :FLASH_DV, :] / acc_s[1, FLASH_DV:FLASH_DV + 1, :]
        o = o1 - lam * o2
        r = lax.rsqrt(jnp.mean(o * o, axis=0, keepdims=True) + RMS_EPS)
        y = o * r * sub_ref[...] * (1.0 - lam_init)
        o_ref[0] = y.T.astype(o_ref.dtype)


def _split_bf16(x, n):
    parts = []
    for _ in range(n):
        p = x.astype(BF16)
        parts.append(p)
        x = x - p.astype(F32)
    return parts


def alibi_key_columns(slopes, t):
    pos = (jnp.arange(t) % FLASH_CHUNK).astype(F32)
    aug = jnp.stack(_split_bf16(slopes[:, None] * pos[None, :], FLASH_AUG), axis=-1)
    return jnp.concatenate([aug, jnp.zeros((slopes.shape[0], t, FLASH_DV - FLASH_AUG), BF16)], axis=-1)


def diff_attention(qt, proj3d, k_blk0, vt, norms, slopes, lam_init, lq1, lk1, lq2, lk2, subln):
    bsz, nh, _, seq = qt.shape
    t = min(T_FLASH, seq)
    assert seq % t == 0 and t % FLASH_PANEL == 0 and FLASH_PANEL % FLASH_CHUNK == 0
    nt = seq // t
    norms = norms.reshape(bsz, nt, 2, nh).transpose(2, 0, 3, 1)
    qn, kn = norms[0], norms[1]
    tile = jnp.arange(nt)
    kt = (tile[:, None] + tile[None, :]) % nt
    min_dist = jnp.maximum((jnp.abs(tile[:, None] - kt) - 1) * t + 1, 0).astype(F32)
    upper = qn[..., :, None] * jnp.take(kn, kt, axis=-1) - slopes[None, :, None, None] * min_dist
    lower_ref = -(qn * kn)[..., :, None]
    skip = jnp.logical_and(upper - lower_ref <= -FLASH_ZERO_LOG2, kt != tile[:, None])
    kt_cols = [jnp.broadcast_to(kt[:, 0], skip.shape[:-1])]
    for ki in range(1, nt):
        kt_cols.append(jnp.where(skip[..., ki], kt_cols[-1], kt[:, ki]))
    kt_eff = jnp.stack(kt_cols, axis=-1).astype(I32).reshape(-1)
    skip = skip.astype(I32).reshape(-1)

    def step_id(b, h, qi, ki):
        return ((b * nh + h) * nt + qi) * nt + ki

    k_cols = alibi_key_columns(slopes, t)
    dv = FLASH_DV
    dva = FLASH_DV + FLASH_ONES
    r = jnp.arange(FLASH_CHUNK, dtype=F32)
    diag = -slopes[:, None, None] * jnp.abs(r[None, :] - r[:, None])[None]
    diag = jnp.stack([jnp.zeros_like(diag), diag], axis=1)
    smem = pl.BlockSpec(memory_space=pltpu.SMEM)
    vec = lambda n: pl.BlockSpec((1, n), lambda b, h, qi, ki, kte, sk: (0, 0))
    grid_spec = pltpu.PrefetchScalarGridSpec(
        num_scalar_prefetch=2,
        grid=(bsz, nh, nt, nt),
        in_specs=[
            smem, smem, smem, smem,
            pl.BlockSpec((1, 1, dv, t), lambda b, h, qi, ki, kte, sk: (b, h, 0, qi)),
            pl.BlockSpec((1, t, dv), lambda b, h, qi, ki, kte, sk: (b, kte[step_id(b, h, qi, ki)], k_blk0 + h)),
            pl.BlockSpec((1, t, dv), lambda b, h, qi, ki, kte, sk: (h, 0, 0)),
            pl.BlockSpec((1, 1, dv, t), lambda b, h, qi, ki, kte, sk: (b, h, 0, kte[step_id(b, h, qi, ki)])),
            pl.BlockSpec((1, 1, FLASH_CHUNK, FLASH_CHUNK),
                         lambda b, h, qi, ki, kte, sk: (h, (ki == 0).astype(I32), 0, 0)),
            vec(HEAD_DIM), vec(HEAD_DIM), vec(HEAD_DIM), vec(HEAD_DIM),
            pl.BlockSpec((dv, 1), lambda b, h, qi, ki, kte, sk: (0, 0)),
        ],
        out_specs=pl.BlockSpec((1, t, LANES), lambda b, h, qi, ki, kte, sk: (b, qi, h)),
        scratch_shapes=[
            pltpu.VMEM((2, 1, t), F32),
            pltpu.VMEM((2, dva, t), F32),
            pltpu.VMEM((2, t, FLASH_PANEL), F32),
            pltpu.SMEM((1,), F32),
        ],
    )
    return pl.pallas_call(
        functools.partial(_flash_body, t=t, step_id=step_id),
        grid_spec=grid_spec,
        out_shape=jax.ShapeDtypeStruct((bsz, seq, nh * dv), BF16),
        compiler_params=_cparams(("parallel", "parallel", "parallel", "arbitrary")),
        name="diff_attention",
    )(kt_eff, skip, slopes, lam_init, qn.reshape(-1), kn.reshape(-1), qt, proj3d, k_cols, vt, diag,
      lq1, lk1, lq2, lk2, subln)


def _outproj_body(x_ref, oa_ref, ob_ref, oc_ref, wa_ref, wb_ref, wc_ref, g1_ref, sc_ref, sh_ref,
                  n2_ref, wr_ref, br_ref, x1_ref, hp_ref, lg_ref):
    mix = (jnp.dot(oa_ref[...], wa_ref[...], preferred_element_type=F32)
           + jnp.dot(ob_ref[...], wb_ref[...], preferred_element_type=F32)
           + jnp.dot(oc_ref[...], wc_ref[...], preferred_element_type=F32))
    x1 = x_ref[...] + g1_ref[0] * mix
    x1_ref[...] = x1
    h = x1 * _rms_scale(x1) * n2_ref[...]
    h = h * (1.0 + sc_ref[0]) + sh_ref[0]
    half = h.shape[1] // 2
    hp_ref[...] = _pack_halves(h[:, :half], h[:, half:])
    lg_ref[...] = lax.dot_general(wr_ref[...], h, _NT, preferred_element_type=F32,
                                  precision=HIGHEST) + br_ref[...]


def out_projection(x2d, oa, ob, oc, wa, wb, wc, gate1, scale2, shift2, gain2, w_router_t, b_router, seq):
    t, d = x2d.shape
    e = w_router_t.shape[0]
    tm = min(TM_PROJ, seq)
    mod_spec = pl.BlockSpec((1, 1, d), lambda i: ((i * tm) // seq, 0, 0))
    row = lambda w: pl.BlockSpec((tm, w), lambda i: (i, 0))
    full = lambda a: pl.BlockSpec(a.shape, lambda i: (0, 0))
    return pl.pallas_call(
        _outproj_body,
        grid=(t // tm,),
        in_specs=[row(d), row(oa.shape[1]), row(ob.shape[1]), row(oc.shape[1]),
                  full(wa), full(wb), full(wc), mod_spec, mod_spec, mod_spec,
                  pl.BlockSpec((1, d), lambda i: (0, 0)), full(w_router_t),
                  pl.BlockSpec((e, 1), lambda i: (0, 0))],
        out_specs=[row(d), row(d // 2), pl.BlockSpec((e, tm), lambda i: (0, i))],
        out_shape=[jax.ShapeDtypeStruct((t, d), F32),
                   jax.ShapeDtypeStruct((t, d // 2), U32),
                   jax.ShapeDtypeStruct((e, t), F32)],
        compiler_params=_cparams(("parallel",)),
        name="out_projection",
    )(x2d, oa, ob, oc, wa, wb, wc, gate1, scale2, shift2, gain2, w_router_t, b_router)


def _router_body(lg_ref, tri_ref, eidx_ref, gate_ref, rank_ref, cnt_ref, base_s):
    step = pl.program_id(0)

    @pl.when(step == 0)
    def _():
        base_s[...] = jnp.zeros(base_s.shape, F32)

    work = lg_ref[...]
    n_e, tm = work.shape
    eio = lax.broadcasted_iota(I32, (n_e, tm), 0).astype(F32)
    vals, hots = [], []
    for k in range(TOP_K):
        mx = jnp.max(work, axis=0, keepdims=True)
        idx = jnp.min(jnp.where(work == mx, eio, float(n_e)), axis=0, keepdims=True)
        hot = eio == idx
        vals.append(mx)
        hots.append(hot)
        eidx_ref[k:k + 1, :] = idx.astype(I32)
        work = jnp.where(hot, -jnp.inf, work)
    exps = [jnp.exp(v - vals[0]) for v in vals]
    den = functools.reduce(lambda a, b: a + b, exps)
    for k in range(TOP_K):
        gate_ref[k:k + 1, :] = exps[k] / den
    any_hot = functools.reduce(lambda a, b: a + b, [h.astype(F32) for h in hots])
    prefix = jnp.dot(any_hot.astype(BF16), tri_ref[...], preferred_element_type=F32)
    pos = base_s[...] + prefix
    for k in range(TOP_K):
        rank_ref[k:k + 1, :] = jnp.sum(jnp.where(hots[k], pos, 0.0), axis=0, keepdims=True).astype(I32)
    base_s[...] = base_s[...] + jnp.sum(any_hot, axis=1, keepdims=True)
    cnt_ref[...] = base_s[...].astype(I32)


def route(logits_t):
    n_e, t = logits_t.shape
    tm = min(TM_ROUTE, t)
    assert t % tm == 0
    r = jnp.arange(tm)
    tri = (r[:, None] < r[None, :]).astype(BF16)
    out4 = lambda dt: jax.ShapeDtypeStruct((TOP_K, t), dt)
    spec4 = pl.BlockSpec((TOP_K, tm), lambda i: (0, i))
    return pl.pallas_call(
        _router_body,
        grid=(t // tm,),
        in_specs=[pl.BlockSpec((n_e, tm), lambda i: (0, i)),
                  pl.BlockSpec((tm, tm), lambda i: (0, 0))],
        out_specs=[spec4, spec4, spec4, pl.BlockSpec((n_e, 1), lambda i: (0, 0))],
        out_shape=[out4(I32), out4(F32), out4(I32), jax.ShapeDtypeStruct((n_e, 1), I32)],
        scratch_shapes=[pltpu.VMEM((n_e, 1), F32)],
        compiler_params=_cparams(("arbitrary",)),
        name="route",
    )(logits_t, tri)


def _dispatch_body(bv_ref, d0_ref, d1_ref, d2_ref, d3_ref, h_ref, xs_ref, zbuf, sem, zsem, *, tm, nb):
    dest_refs = (d0_ref, d1_ref, d2_ref, d3_ref)

    @pl.when(pl.program_id(0) == 0)
    def _():
        zbuf[...] = jnp.zeros(zbuf.shape, zbuf.dtype)

        def zero_block(bk, n):
            partial = bv_ref[bk] < BLK_FFN

            @pl.when(partial)
            def _():
                rows = pl.ds(pl.multiple_of(bk * BLK_FFN, BLK_FFN), BLK_FFN)
                pltpu.make_async_copy(zbuf, xs_ref.at[rows], zsem).start()

            return n + partial.astype(I32)

        n_zeroed = lax.fori_loop(0, nb, zero_block, jnp.int32(0))

        def wait_block(_, carry):
            pltpu.make_async_copy(zbuf, xs_ref.at[pl.ds(0, BLK_FFN)], zsem).wait()
            return carry

        lax.fori_loop(0, n_zeroed, wait_block, 0)

    def issue(tok, carry):
        for k in range(TOP_K):
            pltpu.make_async_copy(h_ref.at[pl.ds(tok, 1)], xs_ref.at[pl.ds(dest_refs[k][tok], 1)], sem).start(priority=k % N_DMA_THREADS)
        return carry

    lax.fori_loop(0, tm, issue, 0, unroll=DMA_ISSUE_UNROLL)
    for k in range(TOP_K):
        pltpu.make_async_copy(h_ref, xs_ref.at[pl.ds(0, tm)], sem).wait()


def dispatch_rows(h_packed, dest, blk_valid, n_rows):
    t, w = h_packed.shape
    tm = min(TM_DISPATCH, t)
    assert TOP_K == 4 and n_rows % BLK_FFN == 0
    nsteps = t // tm
    flat = dest.reshape(-1)
    grid_spec = pltpu.PrefetchScalarGridSpec(
        num_scalar_prefetch=1,
        grid=(nsteps,),
        in_specs=[pl.BlockSpec((tm,), functools.partial(lambda k, i, bv: (k * nsteps + i,), k),
                               memory_space=pltpu.SMEM) for k in range(TOP_K)] + [
                  pl.BlockSpec((tm, w), lambda i, bv: (i, 0))],
        out_specs=pl.BlockSpec(memory_space=pl.ANY),
        scratch_shapes=[pltpu.VMEM((BLK_FFN, w), U32), pltpu.SemaphoreType.DMA, pltpu.SemaphoreType.DMA],
    )
    return pl.pallas_call(
        functools.partial(_dispatch_body, tm=tm, nb=n_rows // BLK_FFN),
        grid_spec=grid_spec,
        out_shape=jax.ShapeDtypeStruct((n_rows, w), U32),
        compiler_params=_cparams(("arbitrary",)),
        name="dispatch_rows",
    )(blk_valid, flat, flat, flat, flat, h_packed)


def _ffn_body(be_ref, bv_ref, nu_ref, xs_ref, wg_ref, bg_ref, wu_ref, bu_ref, wd_ref, bd_ref, o_ref,
              wg_s, wu_s, wd_s):
    i = pl.program_id(0)

    @pl.when(i < nu_ref[0])
    def _():
        @pl.when(jnp.logical_or(i == 0, be_ref[i] != be_ref[jnp.maximum(i - 1, 0)]))
        def _():
            wg_s[...] = wg_ref[0, 0].astype(BF16)
            wu_s[...] = wu_ref[0, 0].astype(BF16)
            wd_s[...] = wd_ref[0, 0].astype(BF16)

        xa, xb = _unpack_halves(xs_ref[...])
        blk, half = xa.shape
        valid = lax.broadcasted_iota(I32, (blk, 1), 0) < bv_ref[i]
        xa = jnp.where(valid, xa, 0.0).astype(BF16)
        xb = jnp.where(valid, xb, 0.0).astype(BF16)

        def proj(w_s, b_ref):
            return (jnp.dot(xa, w_s[:half, :], preferred_element_type=F32)
                    + jnp.dot(xb, w_s[half:, :], preferred_element_type=F32) + b_ref[0, 0])

        a = jnp.minimum(proj(wg_s, bg_ref), SWIGLU_LIMIT)
        u = jnp.clip(proj(wu_s, bu_ref), -SWIGLU_LIMIT, SWIGLU_LIMIT)
        hid = a * (1.0 / (1.0 + jnp.exp(-SWIGLU_ALPHA * a))) * (u + 1.0)
        out = jnp.dot(hid.astype(BF16), wd_s[...], preferred_element_type=F32) + bd_ref[0, 0]
        d = out.shape[1] // 2
        o_ref[...] = _pack_halves(out[:, :d], out[:, d:])

    @pl.when(i >= nu_ref[0])
    def _():
        o_ref[...] = jnp.zeros(o_ref.shape, o_ref.dtype)


def expert_ffn(xs, blk_expert, blk_valid, n_used, layer, wg, bg, wu, bu, wd, bd):
    n_rows, w = xs.shape
    _, n_e, d, f = wg.shape
    nb = n_rows // BLK_FFN
    row_map = lambda i, be, bv, nu: (jnp.minimum(i, nu[0] - 1), 0)
    exp_map = lambda i, be, bv, nu: (layer, be[i], 0, 0)
    grid_spec = pltpu.PrefetchScalarGridSpec(
        num_scalar_prefetch=3,
        grid=(nb,),
        in_specs=[pl.BlockSpec((BLK_FFN, w), row_map),
                  pl.BlockSpec((1, 1, d, f), exp_map), pl.BlockSpec((1, 1, 1, f), exp_map),
                  pl.BlockSpec((1, 1, d, f), exp_map), pl.BlockSpec((1, 1, 1, f), exp_map),
                  pl.BlockSpec((1, 1, f, d), exp_map), pl.BlockSpec((1, 1, 1, d), exp_map)],
        out_specs=pl.BlockSpec((BLK_FFN, d // 2), lambda i, be, bv, nu: (i, 0)),
        scratch_shapes=[pltpu.VMEM((d, f), BF16), pltpu.VMEM((d, f), BF16), pltpu.VMEM((f, d), BF16)],
    )
    return pl.pallas_call(
        _ffn_body,
        grid_spec=grid_spec,
        out_shape=jax.ShapeDtypeStruct((n_rows, d // 2), U32),
        compiler_params=_cparams(("arbitrary",)),
        name="expert_ffn",
    )(blk_expert, blk_valid, n_used, xs, wg, bg, wu, bu, wd, bd)


def _combine_steps(dcur_ref, dnext_ref, x_ref, gt_ref, g2_ref, ys_ref, bufs, sem, tm, finish):
    i = pl.program_id(0)

    def start(dref, col, tok, k, s):
        pltpu.make_async_copy(ys_ref.at[pl.ds(dref[k, col + tok], 1)], bufs[s].at[k, pl.ds(tok, 1)], sem.at[s]).start(priority=k % N_DMA_THREADS)

    def wait(s):
        for k in range(TOP_K):
            pltpu.make_async_copy(ys_ref.at[pl.ds(0, tm)], bufs[s].at[k], sem.at[s]).wait()

    def issue_inline(dref, col, s):
        for tok in range(tm):
            for k in range(TOP_K):
                start(dref, col, tok, k, s)

    def compute(s):
        rows = slice(s * tm, (s + 1) * tm)
        gt = gt_ref[rows, :]
        lo = hi = None
        for k in range(TOP_K):
            a, b = _unpack_halves(bufs[s][k])
            g = gt[:, k:k + 1]
            lo = g * a if lo is None else lo + g * a
            hi = g * b if hi is None else hi + g * b
        half = lo.shape[1]
        g2 = g2_ref[0]
        x2 = jnp.concatenate([x_ref[rows, :half] + g2[:, :half] * lo, x_ref[rows, half:] + g2[:, half:] * hi],
                             axis=1)
        finish(s, rows, x2)

    @pl.when(i == 0)
    def _():
        def issue(tok, carry):
            for k in range(TOP_K):
                start(dcur_ref, 0, tok, k, 0)
            return carry

        lax.fori_loop(0, tm, issue, 0, unroll=DMA_ISSUE_UNROLL)

    wait(0)
    issue_inline(dcur_ref, tm, 1)
    compute(0)
    wait(1)
    issue_inline(dnext_ref, 0, 0)
    compute(1)

    @pl.when(i == pl.num_programs(0) - 1)
    def _():
        wait(0)


def _combine_project_body(dcur_ref, dnext_ref, x_ref, gt_ref, g2_ref, sc_ref, sh_ref, gn_ref, w_ref, ys_ref,
                          x2_ref, proj_ref, nrm_ref, buf0, buf1, sem, *, tm, norm_cols):
    def finish(s, rows, x2):
        x2_ref[rows, :] = x2
        _project_rows(x2, sc_ref[0], sh_ref[0], gn_ref[...], w_ref, proj_ref.at[rows, :], nrm_ref.at[s:s + 1],
                      norm_cols)

    _combine_steps(dcur_ref, dnext_ref, x_ref, gt_ref, g2_ref, ys_ref, (buf0, buf1), sem, tm, finish)


def _combine_final_body(dcur_ref, dnext_ref, x_ref, gt_ref, g2_ref, nf_ref, ys_ref, o_ref, buf0, buf1, sem, *, tm):
    def finish(s, rows, x2):
        o_ref[rows, :] = x2 * _rms_scale(x2) * nf_ref[...]

    _combine_steps(dcur_ref, dnext_ref, x_ref, gt_ref, g2_ref, ys_ref, (buf0, buf1), sem, tm, finish)


def combine(x1, ys, dest, gates_t, gate2, seq, project=None, norm_f=None, norm_cols=()):
    t, d = x1.shape
    tm = min(TM_COMBINE, seq // 2)
    nblocks = t // tm
    nsteps = nblocks // 2
    assert t % (2 * tm) == 0 and seq % (2 * tm) == 0
    mod_spec = pl.BlockSpec((1, 1, d), lambda i: ((i * 2 * tm) // seq, 0, 0))
    row = lambda w: pl.BlockSpec((2 * tm, w), lambda i: (i, 0))
    vec_spec = pl.BlockSpec((1, d), lambda i: (0, 0))
    head_specs = [pl.BlockSpec((TOP_K, 2 * tm), lambda i: (0, i), memory_space=pltpu.SMEM),
                  pl.BlockSpec((TOP_K, tm), lambda i: (0, jnp.minimum(2 * i + 2, nblocks - 1)),
                               memory_space=pltpu.SMEM),
                  row(d), row(TOP_K), mod_spec]
    any_spec = pl.BlockSpec(memory_space=pl.ANY)
    scratch = [pltpu.VMEM((TOP_K, tm, d // 2), U32), pltpu.VMEM((TOP_K, tm, d // 2), U32),
               pltpu.SemaphoreType.DMA((2,))]
    if project is None:
        return pl.pallas_call(
            functools.partial(_combine_final_body, tm=tm),
            grid=(nsteps,),
            in_specs=head_specs + [vec_spec, any_spec],
            out_specs=row(d),
            out_shape=jax.ShapeDtypeStruct((t, d), F32),
            scratch_shapes=scratch,
            compiler_params=_cparams(("arbitrary",)),
            name="combine_final",
        )(dest, dest, x1, gates_t, gate2, norm_f, ys)
    scale, shift, gain, w_bf16 = project
    n = w_bf16.shape[1]
    return pl.pallas_call(
        functools.partial(_combine_project_body, tm=tm, norm_cols=norm_cols),
        grid=(nsteps,),
        in_specs=head_specs + [mod_spec, mod_spec, vec_spec, pl.BlockSpec((d, n), lambda i: (0, 0)), any_spec],
        out_specs=[row(d), row(n), pl.BlockSpec((2, 1, len(norm_cols)), lambda i: (i, 0, 0))],
        out_shape=[jax.ShapeDtypeStruct((t, d), F32),
                   jax.ShapeDtypeStruct((t, n), BF16),
                   jax.ShapeDtypeStruct((nblocks, 1, len(norm_cols)), F32)],
        scratch_shapes=scratch,
        compiler_params=_cparams(("arbitrary",)),
        name="combine_project",
    )(dest, dest, x1, gates_t, gate2, scale, shift, gain, w_bf16, ys)


def _alibi_slopes(n):
    return jnp.exp2(-8.0 * jnp.arange(1, n + 1, dtype=F32) / n)


def _routing_tables(counts, eidx, rank, nb):
    counts = counts.reshape(-1)
    padded = (counts + BLK_FFN - 1) // BLK_FFN * BLK_FFN
    pad_end = jnp.cumsum(padded)
    pad_start = pad_end - padded
    e_ids = jnp.arange(N_EXPERTS, dtype=I32)[:, None, None]
    dest = rank + jnp.sum(jnp.where(eidx[None] == e_ids, pad_start[:, None, None], 0), axis=0)
    blk_row0 = jnp.arange(nb, dtype=I32) * BLK_FFN
    blk_expert = jnp.minimum(jnp.sum(pad_end[None, :] <= blk_row0[:, None], axis=1), N_EXPERTS - 1).astype(I32)
    blk_valid = jnp.clip(counts[blk_expert] - (blk_row0 - pad_start[blk_expert]), 0, BLK_FFN).astype(I32)
    n_used = (pad_end[-1:] // BLK_FFN).astype(I32)
    return dest.astype(I32), blk_expert, blk_valid, n_used


def kernel(x, c, w_ada, b_ada, norm1, w_in, sinks, lambda_q1, lambda_k1, lambda_q2, lambda_k2, subln,
           w_out, norm2, w_router, b_router, w_gate, b_gate, w_up, b_up, w_down, b_down, norm_f):
    bsz, seq, d = x.shape
    depth = w_ada.shape[0]
    t = bsz * seq
    wa_w, wb_w = N_HEADS_A * HEAD_DIM, N_HEADS_B * HEAD_DIM
    kvb_w = N_KV_B * HEAD_DIM
    wc_w = N_HEADS_C * 2 * HEAD_DIM
    off = [0]
    for wdt in (wa_w, wa_w, wa_w, wb_w, kvb_w, kvb_w, wc_w, wc_w, wc_w):
        off.append(off[-1] + wdt)
    blk = [o // LANES for o in off]
    norm_cols = tuple(off[i] + hh * FLASH_DV for i in (6, 7) for hh in range(N_HEADS_C))

    group = N_HEADS_B // N_KV_B
    b_perm = [kv * group + g for g in range(group) for kv in range(N_KV_B)]
    col_perm = jnp.concatenate([jnp.arange(h * HEAD_DIM, (h + 1) * HEAD_DIM) for h in b_perm])
    q_scale = jnp.ones((off[-1],), F32)
    for s0, s1, extra in ((off[0], off[1], 1.0), (off[3], off[4], 1.0), (off[6], off[7], LOG2E)):
        q_scale = q_scale.at[s0:s1].set(HEAD_DIM ** -0.5 * extra)

    slopes_a = _alibi_slopes(N_HEADS_A)
    slopes_b = _alibi_slopes(N_HEADS_B)[jnp.array(b_perm)]
    slopes_c = _alibi_slopes(N_HEADS_C) * LOG2E

    mod_all = ada_modulation(c, w_ada, b_ada)
    x2d = x.reshape(t, d)
    nb = (t * TOP_K) // BLK_FFN + N_EXPERTS
    n_rows = nb * BLK_FFN

    def layer_mod(layer):
        mod = mod_all[layer].reshape(bsz, N_MOD, 1, d)
        return [mod[:, i] for i in range(N_MOD)]

    def in_weights(layer):
        w_l = w_in[layer] * q_scale[None, :]
        w_l = jnp.concatenate([w_l[:, :off[3]], w_l[:, off[3]:off[4]][:, col_perm], w_l[:, off[4]:]], axis=1)
        return w_l.astype(BF16)

    proj = blk_norms = None
    for layer in range(depth):
        sh1, sc1, g1, sh2, sc2, g2 = layer_mod(layer)
        if layer == 0:
            proj, blk_norms = in_projection(x2d, sc1, sh1, norm1[layer].reshape(1, d), in_weights(layer), seq,
                                            norm_cols)
        proj3d = proj.reshape(bsz, seq, off[-1])

        n_blk_a = wa_w // LANES
        o_a = window_attention(proj3d, blk[0], blk[1], blk[2], 1, n_blk_a, slopes_a,
                               jnp.full((n_blk_a, 2), -jnp.inf, F32), DILATED_PAIRS).reshape(t, wa_w)

        sink_l = sinks[layer].astype(F32)[jnp.array(b_perm)].reshape(group, N_KV_B)
        o_b = window_attention(proj3d, blk[3], blk[4], blk[5], 0, group, slopes_b, sink_l,
                               ((2 * SWA_HALF, 1),)).reshape(t, wb_w)

        lam_init = jnp.full((1,), 0.8 - 0.6 * math.exp(-0.3 * layer), F32)
        heads_t = lambda i: (proj3d[:, :, off[i]:off[i + 1]].reshape(bsz, seq, N_HEADS_C, FLASH_DV)
                             .transpose(0, 2, 3, 1))
        t_flash = min(T_FLASH, seq)
        norms = blk_norms.reshape(bsz, seq // t_flash, -1, 2 * N_HEADS_C).max(axis=2, keepdims=True)
        o_c = diff_attention(heads_t(6), proj3d, blk[7], heads_t(8), norms, slopes_c, lam_init,
                             lambda_q1[layer].reshape(1, -1), lambda_k1[layer].reshape(1, -1),
                             lambda_q2[layer].reshape(1, -1), lambda_k2[layer].reshape(1, -1),
                             subln[layer].reshape(-1, 1)).reshape(t, wc_w)

        wo = w_out[layer]
        wo_b = wo[wa_w:wa_w + wb_w].reshape(N_HEADS_B, HEAD_DIM, d)[jnp.array(b_perm)].reshape(wb_w, d)
        x1, h_packed, logits_t = out_projection(
            x2d, o_a, o_b, o_c, wo[:wa_w].astype(BF16), wo_b.astype(BF16), wo[wa_w + wb_w:].astype(BF16),
            g1, sc2, sh2, norm2[layer].reshape(1, d), w_router[layer].T, b_router[layer].reshape(-1, 1), seq)

        eidx, gates, rank, counts = route(logits_t)
        dest, blk_expert, blk_valid, n_used = _routing_tables(counts, eidx, rank, nb)
        xs = dispatch_rows(h_packed, dest, blk_valid, n_rows)
        ys = expert_ffn(xs, blk_expert, blk_valid, n_used, layer,
                        w_gate, b_gate[:, :, None, :], w_up, b_up[:, :, None, :], w_down, b_down[:, :, None, :])
        if layer == depth - 1:
            x2d = combine(x1, ys, dest, gates.T, g2, seq, norm_f=norm_f.reshape(1, d))
        else:
            nsh1, nsc1 = layer_mod(layer + 1)[:2]
            x2d, proj, blk_norms = combine(
                x1, ys, dest, gates.T, g2, seq, norm_cols=norm_cols,
                project=(nsc1, nsh1, norm1[layer + 1].reshape(1, d), in_weights(layer + 1)))

    return x2d.reshape(bsz, seq, d)
```
